```python
import math
import jax, jax.numpy as jnp
from jax import lax
import numpy as np

D_MODEL = 2048
BATCH = 1
SEQ = 16384
DEPTH = 1
DEC_BATCH = 32
DEC_SEQ = 32
PAST_LEN = 1024

CHUNK = 64
LEFT_CHUNKS = 8
BAND_ROWS = LEFT_CHUNKS * CHUNK
MIX_WIDTH = 2 * D_MODEL
D_SSM = MIX_WIDTH // 2
D_ATT = MIX_WIDTH - D_SSM
SSM_HEADDIM = 64
SSM_HEADS = D_SSM // SSM_HEADDIM
SSM_GROUPS = 4
D_STATE = 128
CONV_WIDTH = 4
CONV_DIM = D_SSM + 2 * SSM_GROUPS * D_STATE
SSD_BLOCK = CHUNK
ATT_HEAD_DIM = 128
ATT_HEADS = D_ATT // ATT_HEAD_DIM
REL_CLIP = 128
N_REL = 2 * REL_CLIP + 1
EPS = 1e-6
D_IN_PROJ = D_SSM + CONV_DIM + SSM_HEADS + 4 * D_ATT
OFF_Z = 0
OFF_XBC = OFF_Z + D_SSM
OFF_DT = OFF_XBC + CONV_DIM
OFF_Q = OFF_DT + SSM_HEADS
OFF_K = OFF_Q + D_ATT
OFF_V = OFF_K + D_ATT
OFF_G = OFF_V + D_ATT

kernel_name = "hybrid_ssd_chunkband_stream_step"


def rms_norm(x, w):
    xf = x.astype(jnp.float32)
    xf = xf * lax.rsqrt(jnp.mean(xf * xf, axis=-1, keepdims=True) + EPS)
    return (xf * w.astype(jnp.float32)).astype(x.dtype)


def ssd_chunked(xh, dt, A, Bm, Cm, init_state, block):
    b, L, H, P = xh.shape
    G, N = Bm.shape[2], Bm.shape[3]
    R = H // G
    nb = L // block
    x5 = xh.astype(jnp.float32).reshape(b, nb, block, G, R, P)
    dt5 = dt.reshape(b, nb, block, G, R)
    B5 = Bm.astype(jnp.float32).reshape(b, nb, block, G, N)
    C5 = Cm.astype(jnp.float32).reshape(b, nb, block, G, N)
    acum = jnp.cumsum(dt5 * A.reshape(G, R), axis=2)
    causal = jnp.tril(jnp.ones((block, block), dtype=bool))
    seg = acum[:, :, :, None] - acum[:, :, None, :]
    decay = jnp.exp(jnp.where(causal[:, :, None, None], seg, -jnp.inf))
    cb = jnp.einsum('bcign,bcjgn->bcijg', C5, B5)
    m = cb[..., None] * decay * dt5[:, :, None]
    y_diag = jnp.einsum('bcijgr,bcjgrp->bcigrp', m, x5)

    def step(S, inp):
        Bc, Cc, xc, dtc, acc = inp
        y_off = jnp.einsum('bign,bgrpn->bigrp', Cc, S) * jnp.exp(acc)[..., None]
        w = jnp.exp(acc[:, -1:] - acc) * dtc
        st = jnp.einsum('bjgn,bjgrp->bgrpn', Bc, xc * w[..., None])
        S = S * jnp.exp(acc[:, -1])[..., None, None] + st
        return S, y_off

    xs = (jnp.moveaxis(B5, 1, 0), jnp.moveaxis(C5, 1, 0), jnp.moveaxis(x5, 1, 0),
          jnp.moveaxis(dt5, 1, 0), jnp.moveaxis(acum, 1, 0))
    S0 = init_state.astype(jnp.float32).reshape(b, G, R, P, N)
    S_final, y_off = lax.scan(step, S0, xs)
    y = y_diag + jnp.moveaxis(y_off, 0, 1)
    return y.reshape(b, L, H, P), S_final.reshape(b, H, P, N)


def band_attention(q, k_full, v_full, rel_bias, q_block, hist_rows, hist_valid):
    b, L, H, hd = q.shape
    n_blocks = L // q_block
    span = hist_rows + q_block
    i_idx = jnp.arange(q_block)[:, None]
    j_idx = jnp.arange(span)[None, :]
    rel = jnp.clip(i_idx + hist_rows - j_idx, -REL_CLIP, REL_CLIP) + REL_CLIP
    bias = rel_bias[:, rel].astype(jnp.float32)
    scale = hd ** -0.5
    key_idx = jnp.arange(span)

    def one_block(c):
        start = c * q_block
        qb = lax.dynamic_slice_in_dim(q, start, q_block, axis=1)
        kb = lax.dynamic_slice_in_dim(k_full, start, span, axis=1)
        vb = lax.dynamic_slice_in_dim(v_full, start, span, axis=1)
        s = jnp.einsum('bqhd,bkhd->bhqk', qb, kb).astype(jnp.float32) * scale + bias
        valid = jnp.logical_or(start + key_idx >= hist_rows, hist_valid)
        s = jnp.where(valid[None, None, None, :], s, -jnp.inf)
        p = jax.nn.softmax(s, axis=-1).astype(vb.dtype)
        return jnp.einsum('bhqk,bkhd->bqhd', p, vb)

    out = lax.map(one_block, jnp.arange(n_blocks))
    return jnp.moveaxis(out, 0, 1).reshape(b, L, H, hd)


def hybrid_layer(x, conv_hist, ssm_init, k_hist, v_hist, hist_valid, ssd_block, att_block,
                 norm_w, w_in, conv_w, conv_b, dt_bias, a_log, d_skip, ssm_norm_w,
                 q_norm_w, k_norm_w, rel_bias, w_out):
    b, L, _ = x.shape
    h = rms_norm(x, norm_w)
    proj = h @ w_in
    z = proj[..., OFF_Z:OFF_XBC]
    xbc = proj[..., OFF_XBC:OFF_DT]
    dt_raw = proj[..., OFF_DT:OFF_Q]
    q = proj[..., OFF_Q:OFF_K]
    k = proj[..., OFF_K:OFF_V]
    v = proj[..., OFF_V:OFF_G]
    g = proj[..., OFF_G:]

    xbc_pad = jnp.concatenate([conv_hist.astype(xbc.dtype), xbc], axis=1)
    conv = conv_b
    for t in range(CONV_WIDTH):
        conv = conv + conv_w[t] * xbc_pad[:, t:t + L]
    xbc_act = jax.nn.silu(conv)
    new_conv = xbc_pad[:, -(CONV_WIDTH - 1):]
    gn = SSM_GROUPS * D_STATE
    xs_h = xbc_act[..., :D_SSM].reshape(b, L, SSM_HEADS, SSM_HEADDIM)
    Bm = xbc_act[..., D_SSM:D_SSM + gn].reshape(b, L, SSM_GROUPS, D_STATE)
    Cm = xbc_act[..., D_SSM + gn:].reshape(b, L, SSM_GROUPS, D_STATE)
    dt = jax.nn.softplus(dt_raw.astype(jnp.float32) + dt_bias.astype(jnp.float32))
    A = -jnp.exp(a_log.astype(jnp.float32))
    y_ssm, new_ssm = ssd_chunked(xs_h, dt, A, Bm, Cm, ssm_init, ssd_block)
    y_ssm = y_ssm + d_skip.astype(jnp.float32)[:, None] * xs_h.astype(jnp.float32)
    y_ssm = y_ssm.reshape(b, L, D_SSM) * jax.nn.silu(z.astype(jnp.float32))
    y_ssm = rms_norm(y_ssm.reshape(b, L, SSM_GROUPS, D_SSM // SSM_GROUPS),
                     ssm_norm_w.reshape(SSM_GROUPS, D_SSM // SSM_GROUPS))
    y_ssm = y_ssm.reshape(b, L, D_SSM).astype(x.dtype)

    qh = rms_norm(q.reshape(b, L, ATT_HEADS, ATT_HEAD_DIM), q_norm_w)
    kh = rms_norm(k.reshape(b, L, ATT_HEADS, ATT_HEAD_DIM), k_norm_w)
    vh = v.reshape(b, L, ATT_HEADS, ATT_HEAD_DIM)
    k_full = jnp.concatenate([k_hist.astype(kh.dtype), kh], axis=1)
    v_full = jnp.concatenate([v_hist.astype(vh.dtype), vh], axis=1)
    att = band_attention(qh, k_full, v_full, rel_bias, att_block, k_hist.shape[1], hist_valid)
    y_att = (att.reshape(b, L, D_ATT) * jax.nn.silu(g)).astype(x.dtype)

    mixed = jnp.concatenate([y_ssm, y_att], axis=-1)
    y = x + mixed @ w_out
    return y, new_conv, new_ssm.astype(x.dtype), kh, vh


def setup_inputs(seed: int = 0) -> dict:
    key = jax.random.key(seed)
    ks = jax.random.split(key, 20)
    band_rows = min(BAND_ROWS, PAST_LEN)
    f32 = jnp.float32
    x_prompt = jax.random.normal(ks[0], (BATCH, SEQ, D_MODEL), f32)
    x_sample = jax.random.normal(ks[1], (DEC_BATCH, DEC_SEQ, D_MODEL), f32)
    state_conv = jax.random.normal(ks[2], (DEPTH, DEC_BATCH, CONV_WIDTH - 1, CONV_DIM), f32)
    state_ssm = 0.1 * jax.random.normal(ks[3], (DEPTH, DEC_BATCH, SSM_HEADS, SSM_HEADDIM, D_STATE), f32)
    cache_k = jax.random.normal(ks[4], (DEPTH, DEC_BATCH, band_rows, ATT_HEADS, ATT_HEAD_DIM), f32)
    cache_v = jax.random.normal(ks[5], (DEPTH, DEC_BATCH, band_rows, ATT_HEADS, ATT_HEAD_DIM), f32)
    norm_w = 1.0 + 0.02 * jax.random.normal(ks[6], (DEPTH, D_MODEL), f32)
    w_in = jax.random.normal(ks[7], (DEPTH, D_MODEL, D_IN_PROJ), f32) * D_MODEL ** -0.5
    conv_w = jax.random.normal(ks[8], (DEPTH, CONV_WIDTH, CONV_DIM), f32) * CONV_WIDTH ** -0.5
    conv_b = 0.02 * jax.random.normal(ks[9], (DEPTH, CONV_DIM), f32)
    dt0 = jnp.exp(jax.random.uniform(ks[10], (DEPTH, SSM_HEADS), f32,
                                     minval=math.log(1e-3), maxval=math.log(1e-1)))
    dt_bias = dt0 + jnp.log(-jnp.expm1(-dt0))
    a_log = jnp.log(jax.random.uniform(ks[11], (DEPTH, SSM_HEADS), f32, minval=1.0, maxval=16.0))
    d_skip = 1.0 + 0.1 * jax.random.normal(ks[12], (DEPTH, SSM_HEADS), f32)
    ssm_norm_w = 1.0 + 0.02 * jax.random.normal(ks[13], (DEPTH, D_SSM), f32)
    q_norm_w = 1.0 + 0.02 * jax.random.normal(ks[14], (DEPTH, ATT_HEAD_DIM), f32)
    k_norm_w = 1.0 + 0.02 * jax.random.normal(ks[15], (DEPTH, ATT_HEAD_DIM), f32)
    rel_bias = 0.1 * jax.random.normal(ks[16], (DEPTH, ATT_HEADS, N_REL), f32)
    w_out = jax.random.normal(ks[17], (DEPTH, MIX_WIDTH, D_MODEL), f32) * MIX_WIDTH ** -0.5
    return {"x_prompt": x_prompt, "x_sample": x_sample,
            "state_conv": state_conv, "state_ssm": state_ssm,
            "cache_k": cache_k, "cache_v": cache_v,
            "norm_w": norm_w, "w_in": w_in, "conv_w": conv_w, "conv_b": conv_b,
            "dt_bias": dt_bias, "a_log": a_log, "d_skip": d_skip, "ssm_norm_w": ssm_norm_w,
            "q_norm_w": q_norm_w, "k_norm_w": k_norm_w, "rel_bias": rel_bias, "w_out": w_out}


def reference(x_prompt, x_sample, state_conv, state_ssm, cache_k, cache_v,
              norm_w, w_in, conv_w, conv_b, dt_bias, a_log, d_skip, ssm_norm_w,
              q_norm_w, k_norm_w, rel_bias, w_out):
    bp, Lp, _ = x_prompt.shape
    Ls = x_sample.shape[1]
    keep = min(BAND_ROWS, Lp)
    xp = x_prompt
    xs = x_sample
    p_conv, p_ssm, p_k, p_v = [], [], [], []
    s_conv, s_ssm, s_k, s_v = [], [], [], []
    for l in range(DEPTH):
        params = (norm_w[l], w_in[l], conv_w[l], conv_b[l], dt_bias[l], a_log[l], d_skip[l],
                  ssm_norm_w[l], q_norm_w[l], k_norm_w[l], rel_bias[l], w_out[l])
        zc = jnp.zeros((bp, CONV_WIDTH - 1, CONV_DIM), xp.dtype)
        zs = jnp.zeros((bp, SSM_HEADS, SSM_HEADDIM, D_STATE), jnp.float32)
        zk = jnp.zeros((bp, BAND_ROWS, ATT_HEADS, ATT_HEAD_DIM), xp.dtype)
        xp, c1, s1, k1, v1 = hybrid_layer(xp, zc, zs, zk, zk, False, SSD_BLOCK, CHUNK, *params)
        p_conv.append(c1)
        p_ssm.append(s1)
        p_k.append(k1[:, Lp - keep:])
        p_v.append(v1[:, Lp - keep:])
        xs, c2, s2, k2, v2 = hybrid_layer(xs, state_conv[l], state_ssm[l], cache_k[l], cache_v[l],
                                          True, Ls, Ls, *params)
        s_conv.append(c2)
        s_ssm.append(s2)
        s_k.append(k2)
        s_v.append(v2)
    return (xp, xs,
            jnp.stack(p_conv), jnp.stack(p_ssm), jnp.stack(p_k), jnp.stack(p_v),
            jnp.stack(s_conv), jnp.stack(s_ssm), jnp.stack(s_k), jnp.stack(s_v))
```

```python
import functools

import jax
import jax.numpy as jnp
import numpy as np
from jax import lax
from jax.experimental import pallas as pl
from jax.experimental.pallas import tpu as pltpu

F32 = jnp.float32
BF16 = jnp.bfloat16

D_MODEL = 2048
CHUNK = 64
LEFT_CHUNKS = 8
BAND_ROWS = LEFT_CHUNKS * CHUNK
D_SSM = 2048
D_ATT = 2048
SSM_HEADDIM = 64
SSM_HEADS = 32
SSM_GROUPS = 4
HEADS_PER_GROUP = SSM_HEADS // SSM_GROUPS
GROUP_COLS = HEADS_PER_GROUP * SSM_HEADDIM
D_STATE = 128
CONV_WIDTH = 4
CONV_DIM = D_SSM + 2 * SSM_GROUPS * D_STATE
GROUP_CONV_COLS = GROUP_COLS + 2 * D_STATE
ATT_HEAD_DIM = 128
ATT_HEADS = 16
REL_CLIP = 128
EPS = 1e-6
OFF_Z = 0
OFF_XBC = OFF_Z + D_SSM
OFF_DT = OFF_XBC + CONV_DIM
OFF_Q = OFF_DT + SSM_HEADS
OFF_K = OFF_Q + D_ATT
OFF_V = OFF_K + D_ATT
OFF_G = OFF_V + D_ATT

LANES = 128
SUBLANES = 8
VMEM_LIMIT_BYTES = 56 * 1024 * 1024

P_Q = 0
P_K = P_Q + D_ATT
P_V = P_K + D_ATT
P_G = P_V + D_ATT
P_Z = P_G + D_ATT
P_X = P_Z + D_SSM
P_B = P_X + D_SSM
P_C = P_B + SSM_GROUPS * D_STATE
P_COLS = P_C + SSM_GROUPS * D_STATE
DT_COLS = SSM_GROUPS * LANES

SSD_ROWS = 128
ATT_Q_ROWS = 128
MASK_VALUE = -1e30


def _silu(v):
    return v * (1.0 / (1.0 + jnp.exp(-v)))


def _compiler_params(semantics):
    return pltpu.CompilerParams(dimension_semantics=semantics, vmem_limit_bytes=VMEM_LIMIT_BYTES)


def _inproj_kernel(x_ref, nw_ref, w_ref, wdt_ref, qkw_ref, p_ref, dt_ref, h_scr, *, n_qk_tiles, tn):
    j = pl.program_id(1)

    @pl.when(j == 0)
    def _():
        x = x_ref[...]
        ms = jnp.mean(x * x, axis=-1, keepdims=True)
        h = (x * lax.rsqrt(ms + EPS) * nw_ref[...]).astype(BF16)
        h_scr[...] = h
        dt_ref[...] = jnp.dot(h, wdt_ref[...], preferred_element_type=F32)

    acc = jnp.dot(h_scr[...], w_ref[...], preferred_element_type=F32)

    @pl.when(j < n_qk_tiles)
    def _():
        for hh in range(tn // ATT_HEAD_DIM):
            sl = slice(hh * ATT_HEAD_DIM, (hh + 1) * ATT_HEAD_DIM)
            a = acc[:, sl]
            r = lax.rsqrt(jnp.mean(a * a, axis=-1, keepdims=True) + EPS)
            p_ref[:, sl] = a * r * qkw_ref[:, sl]

    @pl.when(j >= n_qk_tiles)
    def _():
        p_ref[...] = acc


def _in_projection(x, norm_w, w_main, w_dt, qk_w, *, tm, tn):
    t = x.shape[0]
    n_qk_tiles = (2 * D_ATT) // tn
    grid = (t // tm, P_COLS // tn)
    kern = functools.partial(_inproj_kernel, n_qk_tiles=n_qk_tiles, tn=tn)
    return pl.pallas_call(
        kern,
        grid=grid,
        in_specs=[
            pl.BlockSpec((tm, D_MODEL), lambda i, j: (i, 0)),
            pl.BlockSpec((1, D_MODEL), lambda i, j: (0, 0)),
            pl.BlockSpec((D_MODEL, tn), lambda i, j: (0, j)),
            pl.BlockSpec((D_MODEL, DT_COLS), lambda i, j: (0, 0)),
            pl.BlockSpec((1, tn), lambda i, j: (0, jnp.minimum(j, n_qk_tiles - 1))),
        ],
        out_specs=[
            pl.BlockSpec((tm, tn), lambda i, j: (i, j)),
            pl.BlockSpec((tm, DT_COLS), lambda i, j: (i, 0)),
        ],
        out_shape=[
            jax.ShapeDtypeStruct((t, P_COLS), F32),
            jax.ShapeDtypeStruct((t, DT_COLS), F32),
        ],
        scratch_shapes=[pltpu.VMEM((tm, D_MODEL), BF16)],
        compiler_params=_compiler_params(("parallel", "arbitrary")),
        name="in_projection",
    )(x, norm_w, w_main, w_dt, qk_w)


def _ssd_kernel(x_ref, b_ref, c_ref, z_ref, dt_ref, hist_ref, s0_ref, convp_ref, hp_ref, nw_ref,
                y_ref, sout_ref, conv_scr, state_scr, y_scr, xw_scr, *, q_rows, n_chunks):
    c = pl.program_id(2)
    qp = SSD_ROWS
    pad_rows = qp - q_rows

    @pl.when(c == 0)
    def _():
        conv_scr[0:SUBLANES, :] = hist_ref[0, 0]
        state_scr[...] = s0_ref[0]
        if pad_rows:
            conv_scr[SUBLANES + q_rows:, :] = jnp.zeros((pad_rows, GROUP_CONV_COLS), F32)

    conv_scr[SUBLANES:SUBLANES + q_rows, 0:GROUP_COLS] = x_ref[...]
    conv_scr[SUBLANES:SUBLANES + q_rows, GROUP_COLS:GROUP_COLS + D_STATE] = b_ref[...]
    conv_scr[SUBLANES:SUBLANES + q_rows, GROUP_COLS + D_STATE:] = c_ref[...]

    convp = convp_ref[0]
    conv = convp[CONV_WIDTH:CONV_WIDTH + 1, :]
    for tap in range(CONV_WIDTH):
        start = SUBLANES - (CONV_WIDTH - 1) + tap
        conv = conv + convp[tap:tap + 1, :] * conv_scr[start:start + qp, :]
    act = _silu(conv)
    conv_scr[0:SUBLANES, :] = conv_scr[q_rows:q_rows + SUBLANES, :]

    xact = act[:, 0:GROUP_COLS]
    bact = act[:, GROUP_COLS:GROUP_COLS + D_STATE].astype(BF16)
    cact = act[:, GROUP_COLS + D_STATE:].astype(BF16)

    hp = hp_ref[0]
    dt_raw = dt_ref[...]
    if pad_rows:
        dt_raw = jnp.concatenate([dt_raw, jnp.zeros((pad_rows, LANES), F32)], axis=0)
    v = dt_raw + hp[0:1, :]
    dt = jnp.maximum(v, 0.0) + jnp.log1p(jnp.exp(-jnp.abs(v)))
    if pad_rows:
        row_id = lax.broadcasted_iota(jnp.int32, (qp, LANES), 0)
        dt = jnp.where(row_id < q_rows, dt, 0.0)
    a = dt * (-jnp.exp(hp[1:2, :]))

    ii = lax.broadcasted_iota(jnp.int32, (qp, qp), 0)
    jj = lax.broadcasted_iota(jnp.int32, (qp, qp), 1)
    causal = ii >= jj
    tril = causal.astype(F32)
    acum = jnp.dot(tril, a, precision=lax.Precision.HIGHEST, preferred_element_type=F32)
    acum_t = acum.T
    dt_t = dt.T
    last = acum[qp - 1:qp, :]
    e_acum = jnp.exp(acum)
    w_all = jnp.exp(last - acum) * dt
    e_last = jnp.exp(last)

    cb = lax.dot_general(cact, bact, (((1,), (1,)), ((), ())), preferred_element_type=F32)
    state = state_scr[...]
    y_off = lax.dot_general(cact, state.astype(BF16), (((1,), (1,)), ((), ())),
                            preferred_element_type=F32)

    for r in range(HEADS_PER_GROUP):
        sl = slice(r * SSM_HEADDIM, (r + 1) * SSM_HEADDIM)
        col = acum[:, r:r + 1]
        row = acum_t[r:r + 1, :]
        decay = jnp.exp(jnp.where(causal, col - row, -jnp.inf))
        m = cb * decay * dt_t[r:r + 1, :]
        xh = xact[:, sl]
        yd = jnp.dot(m.astype(BF16), xh.astype(BF16), preferred_element_type=F32)
        y_scr[:, sl] = yd + y_off[:, sl] * e_acum[:, r:r + 1] + hp[2:3, r:r + 1] * xh
        xw_scr[:, sl] = xh * w_all[:, r:r + 1]
        state_scr[sl, :] = state[sl, :] * e_last[:, r:r + 1]

    upd = lax.dot_general(xw_scr[...].astype(BF16), bact, (((0,), (0,)), ((), ())),
                          preferred_element_type=F32)
    state_scr[...] = state_scr[...] + upd

    y = y_scr[...]
    if pad_rows:
        y = y[0:q_rows, :]
    yg = y * _silu(z_ref[...])
    rn = lax.rsqrt(jnp.mean(yg * yg, axis=-1, keepdims=True) + EPS)
    y_ref[...] = (yg * rn * nw_ref[0]).astype(y_ref.dtype)

    @pl.when(c == n_chunks - 1)
    def _():
        sout_ref[0] = state_scr[...]


def _ssd_branch(p, dt, hist, s0, convp, hp, ssm_nw, *, n_streams, q_rows, n_chunks):
    t = p.shape[0]
    grid = (n_streams, SSM_GROUPS, n_chunks)

    def rows(b, g, c):
        return b * n_chunks + c

    kern = functools.partial(_ssd_kernel, q_rows=q_rows, n_chunks=n_chunks)
    return pl.pallas_call(
        kern,
        grid=grid,
        in_specs=[
            pl.BlockSpec((q_rows, GROUP_COLS), lambda b, g, c: (rows(b, g, c), P_X // GROUP_COLS + g)),
            pl.BlockSpec((q_rows, D_STATE), lambda b, g, c: (rows(b, g, c), P_B // D_STATE + g)),
            pl.BlockSpec((q_rows, D_STATE), lambda b, g, c: (rows(b, g, c), P_C // D_STATE + g)),
            pl.BlockSpec((q_rows, GROUP_COLS), lambda b, g, c: (rows(b, g, c), P_Z // GROUP_COLS + g)),
            pl.BlockSpec((q_rows, LANES), lambda b, g, c: (rows(b, g, c), g)),
            pl.BlockSpec((1, 1, SUBLANES, GROUP_CONV_COLS), lambda b, g, c: (b, g, 0, 0)),
            pl.BlockSpec((1, GROUP_COLS, D_STATE), lambda b, g, c: (b, g, 0)),
            pl.BlockSpec((1, SUBLANES, GROUP_CONV_COLS), lambda b, g, c: (g, 0, 0)),
            pl.BlockSpec((1, SUBLANES, LANES), lambda b, g, c: (g, 0, 0)),
            pl.BlockSpec((1, 1, GROUP_COLS), lambda b, g, c: (g, 0, 0)),
        ],
        out_specs=[
            pl.BlockSpec((q_rows, GROUP_COLS), lambda b, g, c: (rows(b, g, c), g)),
            pl.BlockSpec((1, GROUP_COLS, D_STATE), lambda b, g, c: (b, g, 0)),
        ],
        out_shape=[
            jax.ShapeDtypeStruct((t, D_SSM), BF16),
            jax.ShapeDtypeStruct((n_streams, D_SSM, D_STATE), F32),
        ],
        scratch_shapes=[
            pltpu.VMEM((SUBLANES + SSD_ROWS, GROUP_CONV_COLS), F32),
            pltpu.VMEM((GROUP_COLS, D_STATE), F32),
            pltpu.VMEM((SSD_ROWS, GROUP_COLS), F32),
            pltpu.VMEM((SSD_ROWS, GROUP_COLS), F32),
        ],
        compiler_params=_compiler_params(("parallel", "parallel", "arbitrary")),
        name="ssd_branch",
    )(p, p, p, p, dt, hist, s0, convp, hp, ssm_nw)


def _softmax_rows(s):
    m = jnp.max(s, axis=-1, keepdims=True)
    e = jnp.exp(s - m)
    return e * (1.0 / jnp.sum(e, axis=-1, keepdims=True))


def _prompt_attn_kernel(q_ref, g_ref, k0, k1, k2, k3, k4, v0, v1, v2, v3, v4, bias_ref, o_ref, *, n_hist_blocks):
    t = pl.program_id(0)
    k_refs = (k0, k1, k2, k3, k4)
    v_refs = (v0, v1, v2, v3, v4)
    scale = ATT_HEAD_DIM ** -0.5
    for h in range(ATT_HEADS):
        sl = slice(h * ATT_HEAD_DIM, (h + 1) * ATT_HEAD_DIM)
        qh = q_ref[:, sl].astype(BF16)
        parts = []
        for blk in range(n_hist_blocks + 1):
            s = lax.dot_general(qh, k_refs[blk][:, sl].astype(BF16), (((1,), (1,)), ((), ())),
                                preferred_element_type=F32)
            parts.append(jnp.where(t + blk >= n_hist_blocks, s, MASK_VALUE))
        s = jnp.concatenate(parts, axis=1) * scale + bias_ref[h]
        p = _softmax_rows(s).astype(BF16)
        o = jnp.zeros((ATT_Q_ROWS, ATT_HEAD_DIM), F32)
        for blk in range(n_hist_blocks + 1):
            o = o + jnp.dot(p[:, blk * ATT_Q_ROWS:(blk + 1) * ATT_Q_ROWS], v_refs[blk][:, sl].astype(BF16),
                            preferred_element_type=F32)
        o_ref[:, sl] = (o * _silu(g_ref[:, sl])).astype(o_ref.dtype)


def _prompt_attention(p, bias):
    t = p.shape[0]
    n_hist_blocks = BAND_ROWS // ATT_Q_ROWS
    n_tiles = t // ATT_Q_ROWS
    span = BAND_ROWS + ATT_Q_ROWS

    def kv_spec(blk, col_block):
        return pl.BlockSpec((ATT_Q_ROWS, D_ATT),
                            lambda i: (jnp.maximum(i - n_hist_blocks + blk, 0), col_block))

    in_specs = [pl.BlockSpec((ATT_Q_ROWS, D_ATT), lambda i: (i, P_Q // D_ATT)),
                pl.BlockSpec((ATT_Q_ROWS, D_ATT), lambda i: (i, P_G // D_ATT))]
    in_specs += [kv_spec(blk, P_K // D_ATT) for blk in range(n_hist_blocks + 1)]
    in_specs += [kv_spec(blk, P_V // D_ATT) for blk in range(n_hist_blocks + 1)]
    in_specs += [pl.BlockSpec((ATT_HEADS, ATT_Q_ROWS, span), lambda i: (0, 0, 0))]
    kern = functools.partial(_prompt_attn_kernel, n_hist_blocks=n_hist_blocks)
    return pl.pallas_call(
        kern,
        grid=(n_tiles,),
        in_specs=in_specs,
        out_specs=pl.BlockSpec((ATT_Q_ROWS, D_ATT), lambda i: (i, 0)),
        out_shape=jax.ShapeDtypeStruct((t, D_ATT), BF16),
        compiler_params=_compiler_params(("parallel",)),
        name="prompt_attention",
    )(*([p] * (2 + 2 * (n_hist_blocks + 1))), bias)


def _sample_attn_kernel(q_ref, kn_ref, vn_ref, g_ref, kc_ref, vc_ref, bias_ref, o_ref, *, q_rows, span_pad):
    scale = ATT_HEAD_DIM ** -0.5
    pad = span_pad - BAND_ROWS - q_rows
    zpad = jnp.zeros((pad, ATT_HEAD_DIM), BF16)
    for h in range(ATT_HEADS):
        sl = slice(h * ATT_HEAD_DIM, (h + 1) * ATT_HEAD_DIM)
        qh = q_ref[:, sl].astype(BF16)
        kcat = jnp.concatenate([kc_ref[0, :, sl].astype(BF16), kn_ref[:, sl].astype(BF16), zpad], axis=0)
        vcat = jnp.concatenate([vc_ref[0, :, sl].astype(BF16), vn_ref[:, sl].astype(BF16), zpad], axis=0)
        s = lax.dot_general(qh, kcat, (((1,), (1,)), ((), ())), preferred_element_type=F32)
        s = s * scale + bias_ref[h]
        p = _softmax_rows(s).astype(BF16)
        o = jnp.dot(p, vcat, preferred_element_type=F32)
        o_ref[:, sl] = (o * _silu(g_ref[:, sl])).astype(o_ref.dtype)


def _sample_attention(p, cache_k, cache_v, bias, *, n_streams, q_rows):
    t = p.shape[0]
    span_pad = bias.shape[-1]
    kern = functools.partial(_sample_attn_kernel, q_rows=q_rows, span_pad=span_pad)

    def new_spec(col_block):
        return pl.BlockSpec((q_rows, D_ATT), lambda b: (b, col_block))

    return pl.pallas_call(
        kern,
        grid=(n_streams,),
        in_specs=[
            new_spec(P_Q // D_ATT), new_spec(P_K // D_ATT), new_spec(P_V // D_ATT), new_spec(P_G // D_ATT),
            pl.BlockSpec((1, BAND_ROWS, D_ATT), lambda b: (b, 0, 0)),
            pl.BlockSpec((1, BAND_ROWS, D_ATT), lambda b: (b, 0, 0)),
            pl.BlockSpec((ATT_HEADS, q_rows, span_pad), lambda b: (0, 0, 0)),
        ],
        out_specs=pl.BlockSpec((q_rows, D_ATT), lambda b: (b, 0)),
        out_shape=jax.ShapeDtypeStruct((t, D_ATT), BF16),
        compiler_params=_compiler_params(("parallel",)),
        name="sample_attention",
    )(p, p, p, p, cache_k, cache_v, bias)


def _outproj_kernel(ys_ref, ya_ref, w1_ref, w2_ref, x_ref, o_ref):
    acc = jnp.dot(ys_ref[...], w1_ref[...], preferred_element_type=F32)
    acc = acc + jnp.dot(ya_ref[...], w2_ref[...], preferred_element_type=F32)
    o_ref[...] = x_ref[...] + acc


def _out_projection(y_ssm, y_att, w_ssm, w_att, x, *, tm, tn):
    t = x.shape[0]
    return pl.pallas_call(
        _outproj_kernel,
        grid=(t // tm, D_MODEL // tn),
        in_specs=[
            pl.BlockSpec((tm, D_SSM), lambda i, j: (i, 0)),
            pl.BlockSpec((tm, D_ATT), lambda i, j: (i, 0)),
            pl.BlockSpec((D_SSM, tn), lambda i, j: (0, j)),
            pl.BlockSpec((D_ATT, tn), lambda i, j: (0, j)),
            pl.BlockSpec((tm, tn), lambda i, j: (i, j)),
        ],
        out_specs=pl.BlockSpec((tm, tn), lambda i, j: (i, j)),
        out_shape=jax.ShapeDtypeStruct((t, D_MODEL), F32),
        compiler_params=_compiler_params(("parallel", "arbitrary")),
        name="out_projection",
    )(y_ssm, y_att, w_ssm, w_att, x)


def _group_conv_columns(v):
    lead = v.shape[:-1]
    xs = v[..., :D_SSM].reshape(*lead, SSM_GROUPS, GROUP_COLS)
    bs = v[..., D_SSM:D_SSM + SSM_GROUPS * D_STATE].reshape(*lead, SSM_GROUPS, D_STATE)
    cs = v[..., D_SSM + SSM_GROUPS * D_STATE:].reshape(*lead, SSM_GROUPS, D_STATE)
    return jnp.concatenate([xs, bs, cs], axis=-1)


def _pad_to(v, size, axis):
    pad = [(0, 0)] * v.ndim
    pad[axis] = (0, size - v.shape[axis])
    return jnp.pad(v, pad)


def _prepare_params(norm_w, w_in, conv_w, conv_b, dt_bias, a_log, d_skip, ssm_norm_w, q_norm_w, k_norm_w, w_out):
    w_main = jnp.concatenate(
        [w_in[:, OFF_Q:OFF_K], w_in[:, OFF_K:OFF_V], w_in[:, OFF_V:OFF_G], w_in[:, OFF_G:],
         w_in[:, OFF_Z:OFF_XBC], w_in[:, OFF_XBC:OFF_DT]], axis=1).astype(BF16)
    w_dt = w_in[:, OFF_DT:OFF_Q].reshape(D_MODEL, SSM_GROUPS, HEADS_PER_GROUP)
    w_dt = _pad_to(w_dt, LANES, 2).reshape(D_MODEL, DT_COLS).astype(BF16)
    qk_w = jnp.concatenate([jnp.tile(q_norm_w, ATT_HEADS), jnp.tile(k_norm_w, ATT_HEADS)]).reshape(1, 2 * D_ATT)
    taps = jnp.moveaxis(_group_conv_columns(conv_w), 1, 0)
    bias = _group_conv_columns(conv_b)[:, None, :]
    convp = _pad_to(jnp.concatenate([taps, bias], axis=1), SUBLANES, 1)

    def per_group(v):
        return _pad_to(v.reshape(SSM_GROUPS, 1, HEADS_PER_GROUP), LANES, 2)

    hp = _pad_to(jnp.concatenate([per_group(dt_bias), per_group(a_log), per_group(d_skip)], axis=1), SUBLANES, 1)
    ssm_nw = ssm_norm_w.reshape(SSM_GROUPS, 1, GROUP_COLS)
    w_ssm = w_out[:D_SSM].astype(BF16)
    w_att = w_out[D_SSM:].astype(BF16)
    return dict(norm_w=norm_w.reshape(1, D_MODEL), w_main=w_main, w_dt=w_dt, qk_w=qk_w, convp=convp, hp=hp,
                ssm_nw=ssm_nw, w_ssm=w_ssm, w_att=w_att)


def _conv_history(conv_rows):
    grouped = jnp.moveaxis(_group_conv_columns(conv_rows), 2, 1)
    return jnp.pad(grouped, ((0, 0), (0, 0), (SUBLANES - (CONV_WIDTH - 1), 0), (0, 0)))


def _rel_bias_table(rel_bias, q_rows, hist_rows, span_pad, band_chunk):
    i_idx = np.arange(q_rows)[:, None]
    j_idx = np.arange(span_pad)[None, :]
    rel = np.clip(i_idx + hist_rows - j_idx, -REL_CLIP, REL_CLIP) + REL_CLIP
    valid = j_idx < hist_rows + q_rows
    if band_chunk is not None:
        start = (i_idx // band_chunk) * band_chunk
        valid = valid & (j_idx >= start) & (j_idx < start + hist_rows + band_chunk)
    bias = rel_bias[:, rel].astype(F32)
    return jnp.where(valid[None], bias, MASK_VALUE)


def _layer(x, hist, s0, cache_k, cache_v, prm, rel_bias, *, n_streams, seq, prompt):
    t = n_streams * seq
    x2 = x.reshape(t, D_MODEL)
    p, dt = _in_projection(x2, prm["norm_w"], prm["w_main"], prm["w_dt"], prm["qk_w"], tm=512, tn=1024)
    if prompt:
        q_rows, n_chunks = SSD_ROWS, seq // SSD_ROWS
    else:
        q_rows, n_chunks = seq, 1
    y_ssm, s_new = _ssd_branch(p, dt, hist, s0, prm["convp"], prm["hp"], prm["ssm_nw"],
                               n_streams=n_streams, q_rows=q_rows, n_chunks=n_chunks)
    if prompt:
        bias = _rel_bias_table(rel_bias, ATT_Q_ROWS, BAND_ROWS, BAND_ROWS + ATT_Q_ROWS, CHUNK)
        y_att = _prompt_attention(p, bias)
    else:
        span_pad = -(-(BAND_ROWS + seq) // LANES) * LANES
        bias = _rel_bias_table(rel_bias, seq, BAND_ROWS, span_pad, None)
        y_att = _sample_attention(p, cache_k, cache_v, bias, n_streams=n_streams, q_rows=seq)
    y = _out_projection(y_ssm, y_att, prm["w_ssm"], prm["w_att"], x2, tm=512, tn=1024)
    p3 = p.reshape(n_streams, seq, P_COLS)
    xbc_tail = p3[:, seq - (CONV_WIDTH - 1):, P_X:]
    new_conv = xbc_tail
    kh = p3[:, :, P_K:P_V].reshape(n_streams, seq, ATT_HEADS, ATT_HEAD_DIM)
    vh = p3[:, :, P_V:P_G].reshape(n_streams, seq, ATT_HEADS, ATT_HEAD_DIM)
    new_ssm = s_new.reshape(n_streams, SSM_HEADS, SSM_HEADDIM, D_STATE)
    return y.reshape(n_streams, seq, D_MODEL), new_conv, new_ssm, kh, vh


def kernel(x_prompt, x_sample, state_conv, state_ssm, cache_k, cache_v, norm_w, w_in, conv_w, conv_b, dt_bias, a_log, d_skip, ssm_norm_w, q_norm_w, k_norm_w, rel_bias, w_out):
    bp, lp, _ = x_prompt.shape
    bs, ls, _ = x_sample.shape
    keep = min(BAND_ROWS, lp)
    assert bp == 1 and lp % SSD_ROWS == 0 and lp % ATT_Q_ROWS == 0 and lp >= BAND_ROWS
    assert ls % SUBLANES == 0 and ls <= SSD_ROWS and cache_k.shape[2] == BAND_ROWS
    assert norm_w.shape[0] == 1
    prm = _prepare_params(norm_w[0], w_in[0], conv_w[0], conv_b[0], dt_bias[0], a_log[0], d_skip[0],
                          ssm_norm_w[0], q_norm_w[0], k_norm_w[0], w_out[0])
    rb = rel_bias[0]

    zero_hist = jnp.zeros((bp, SSM_GROUPS, SUBLANES, GROUP_CONV_COLS), F32)
    zero_state = jnp.zeros((bp, D_SSM, D_STATE), F32)
    yp, c1, s1, k1, v1 = _layer(x_prompt, zero_hist, zero_state, None, None, prm, rb,
                                n_streams=bp, seq=lp, prompt=True)

    hist = _conv_history(state_conv[0])
    s0 = state_ssm[0].reshape(bs, D_SSM, D_STATE)
    ck = cache_k[0].reshape(bs, BAND_ROWS, D_ATT)
    cv = cache_v[0].reshape(bs, BAND_ROWS, D_ATT)
    ys, c2, s2, k2, v2 = _layer(x_sample, hist, s0, ck, cv, prm, rb, n_streams=bs, seq=ls, prompt=False)

    return (yp, ys,
            c1[None], s1[None], k1[:, lp - keep:][None], v1[:, lp - keep:][None],
            c2[None], s2[None], k2[None], v2[None])
```

```python
import functools

import jax
import jax.numpy as jnp
import numpy as np
from jax import lax
from jax.experimental import pallas as pl
from jax.experimental.pallas import tpu as pltpu

F32 = jnp.float32
BF16 = jnp.bfloat16

D_MODEL = 2048
CHUNK = 64
LEFT_CHUNKS = 8
BAND_ROWS = LEFT_CHUNKS * CHUNK
D_SSM = 2048
D_ATT = 2048
SSM_HEADDIM = 64
SSM_HEADS = 32
SSM_GROUPS = 4
HEADS_PER_GROUP = SSM_HEADS // SSM_GROUPS
GROUP_COLS = HEADS_PER_GROUP * SSM_HEADDIM
D_STATE = 128
CONV_WIDTH = 4
CONV_DIM = D_SSM + 2 * SSM_GROUPS * D_STATE
GROUP_CONV_COLS = GROUP_COLS + 2 * D_STATE
ATT_HEAD_DIM = 128
ATT_HEADS = 16
REL_CLIP = 128
N_REL = 2 * REL_CLIP + 1
EPS = 1e-6
OFF_Z = 0
OFF_XBC = OFF_Z + D_SSM
OFF_DT = OFF_XBC + CONV_DIM
OFF_Q = OFF_DT + SSM_HEADS
OFF_K = OFF_Q + D_ATT
OFF_V = OFF_K + D_ATT
OFF_G = OFF_V + D_ATT

LANES = 128
SUBLANES = 8
VMEM_LIMIT_BYTES = 56 * 1024 * 1024

R_G = 0
R_Z = R_G + D_ATT
R_X = R_Z + D_SSM
R_B = R_X + D_SSM
R_C = R_B + SSM_GROUPS * D_STATE
R_COLS = R_C + SSM_GROUPS * D_STATE
DT_COLS = SSM_GROUPS * LANES
QKV_Q, QKV_K, QKV_V = 0, 1, 2

PROJ_TN = 1024
SSD_ROWS = 128
ATT_Q_ROWS = 128
SOFTMAX_SLAB = 32
MASK_VALUE = -1e30


def _silu(v):
    return v * (1.0 / (1.0 + jnp.exp(-v)))


def _compiler_params(semantics):
    return pltpu.CompilerParams(dimension_semantics=semantics, vmem_limit_bytes=VMEM_LIMIT_BYTES)


def _normed_rows(x_ref, nw_ref):
    x = x_ref[...]
    ms = jnp.mean(x * x, axis=-1, keepdims=True)
    return (x * lax.rsqrt(ms + EPS) * nw_ref[...]).astype(BF16)


def _head_norm(a, use_norm, w):
    r = lax.rsqrt(jnp.mean(a * a, axis=-1, keepdims=True) + EPS)
    return a * jnp.where(use_norm, r, 1.0) * jnp.where(use_norm, w, 1.0)


def _inproj_kernel(x_ref, nw_ref, wqkv_ref, wg_ref, wa_ref, wdt_ref, qkw_ref,
                   qkv_ref, rest_ref, dt_ref, h_scr, *, n_qk, n_qkv, n_g):
    j = pl.program_id(1)
    heads_per_tile = PROJ_TN // ATT_HEAD_DIM

    @pl.when(j == 0)
    def _():
        h = _normed_rows(x_ref, nw_ref)
        h_scr[...] = h
        dt_ref[...] = jnp.dot(h, wdt_ref[...], preferred_element_type=F32)

    @pl.when(j < n_qkv)
    def _():
        acc = jnp.dot(h_scr[...], wqkv_ref[...], preferred_element_type=F32)
        for hh in range(heads_per_tile):
            sl = slice(hh * ATT_HEAD_DIM, (hh + 1) * ATT_HEAD_DIM)
            qkv_ref[0, hh] = _head_norm(acc[:, sl], j < n_qk, qkw_ref[:, sl]).astype(BF16)

    @pl.when(jnp.logical_and(j >= n_qkv, j < n_qkv + n_g))
    def _():
        rest_ref[...] = jnp.dot(h_scr[...], wg_ref[...], preferred_element_type=F32)

    @pl.when(j >= n_qkv + n_g)
    def _():
        rest_ref[...] = jnp.dot(h_scr[...], wa_ref[...], preferred_element_type=F32)


def _in_projection(x, norm_w, w_qkv, w_g, w_a, w_dt, qk_w, *, tm):
    t = x.shape[0]
    tn = PROJ_TN
    tiles_per_proj = D_ATT // tn
    n_qk, n_qkv, n_g = 2 * tiles_per_proj, 3 * tiles_per_proj, tiles_per_proj
    n_a = (D_SSM + CONV_DIM) // tn
    heads_per_tile = tn // ATT_HEAD_DIM
    grid = (t // tm, n_qkv + n_g + n_a)
    kern = functools.partial(_inproj_kernel, n_qk=n_qk, n_qkv=n_qkv, n_g=n_g)

    def qkv_index(i, j):
        jj = jnp.minimum(j, n_qkv - 1)
        return (jj // tiles_per_proj, jj % tiles_per_proj, i, 0)

    return pl.pallas_call(
        kern,
        grid=grid,
        in_specs=[
            pl.BlockSpec((tm, D_MODEL), lambda i, j: (i, 0)),
            pl.BlockSpec((1, D_MODEL), lambda i, j: (0, 0)),
            pl.BlockSpec((D_MODEL, tn), lambda i, j: (0, jnp.minimum(j, n_qkv - 1))),
            pl.BlockSpec((D_MODEL, tn), lambda i, j: (0, jnp.clip(j - n_qkv, 0, n_g - 1))),
            pl.BlockSpec((D_MODEL, tn), lambda i, j: (0, jnp.clip(j - n_qkv - n_g, 0, n_a - 1))),
            pl.BlockSpec((D_MODEL, DT_COLS), lambda i, j: (0, 0)),
            pl.BlockSpec((1, tn), lambda i, j: (0, jnp.minimum(j, n_qk - 1))),
        ],
        out_specs=[
            pl.BlockSpec((1, heads_per_tile, tm, ATT_HEAD_DIM), qkv_index),
            pl.BlockSpec((tm, tn), lambda i, j: (i, jnp.maximum(j - n_qkv, 0))),
            pl.BlockSpec((tm, DT_COLS), lambda i, j: (i, 0)),
        ],
        out_shape=[
            jax.ShapeDtypeStruct((3, ATT_HEADS, t, ATT_HEAD_DIM), BF16),
            jax.ShapeDtypeStruct((t, R_COLS), F32),
            jax.ShapeDtypeStruct((t, DT_COLS), F32),
        ],
        scratch_shapes=[pltpu.VMEM((tm, D_MODEL), BF16)],
        compiler_params=_compiler_params(("parallel", "arbitrary")),
        name="in_projection",
    )(x, norm_w, w_qkv, w_g, w_a, w_dt, qk_w)


def _kv_rows_kernel(x_ref, nw_ref, w_ref, qkw_ref, o_ref, h_scr, *, n_k):
    j = pl.program_id(1)

    @pl.when(j == 0)
    def _():
        h_scr[...] = _normed_rows(x_ref, nw_ref)

    acc = jnp.dot(h_scr[...], w_ref[...], preferred_element_type=F32)
    for hh in range(PROJ_TN // ATT_HEAD_DIM):
        sl = slice(hh * ATT_HEAD_DIM, (hh + 1) * ATT_HEAD_DIM)
        o_ref[:, sl] = _head_norm(acc[:, sl], j < n_k, qkw_ref[:, sl])


def _kv_rows(x, norm_w, w_qkv, qk_w, *, first_row, n_rows, tm):
    tn = PROJ_TN
    tiles_per_proj = D_ATT // tn
    row0 = first_row // tm
    kern = functools.partial(_kv_rows_kernel, n_k=tiles_per_proj)
    return pl.pallas_call(
        kern,
        grid=(n_rows // tm, 2 * tiles_per_proj),
        in_specs=[
            pl.BlockSpec((tm, D_MODEL), lambda i, j: (row0 + i, 0)),
            pl.BlockSpec((1, D_MODEL), lambda i, j: (0, 0)),
            pl.BlockSpec((D_MODEL, tn), lambda i, j: (0, tiles_per_proj + j)),
            pl.BlockSpec((1, tn), lambda i, j: (0, jnp.minimum(tiles_per_proj + j, 2 * tiles_per_proj - 1))),
        ],
        out_specs=pl.BlockSpec((tm, tn), lambda i, j: (i, j)),
        out_shape=jax.ShapeDtypeStruct((n_rows, 2 * D_ATT), F32),
        scratch_shapes=[pltpu.VMEM((tm, D_MODEL), BF16)],
        compiler_params=_compiler_params(("parallel", "arbitrary")),
        name="kv_rows",
    )(x, norm_w, w_qkv, qk_w)


def _ssd_kernel(x_ref, b_ref, c_ref, z_ref, dt_ref, hist_ref, s0_ref, convp_ref, hp_ref, nw_ref,
                y_ref, sout_ref, conv_scr, state_scr, y_scr, xw_scr, *, q_rows, n_chunks):
    c = pl.program_id(2)
    qp = SSD_ROWS
    pad_rows = qp - q_rows

    @pl.when(c == 0)
    def _():
        conv_scr[0:SUBLANES, :] = hist_ref[0, 0]
        state_scr[...] = s0_ref[0]
        if pad_rows:
            conv_scr[SUBLANES + q_rows:, :] = jnp.zeros((pad_rows, GROUP_CONV_COLS), F32)

    conv_scr[SUBLANES:SUBLANES + q_rows, 0:GROUP_COLS] = x_ref[...]
    conv_scr[SUBLANES:SUBLANES + q_rows, GROUP_COLS:GROUP_COLS + D_STATE] = b_ref[...]
    conv_scr[SUBLANES:SUBLANES + q_rows, GROUP_COLS + D_STATE:] = c_ref[...]

    convp = convp_ref[0]
    conv = convp[CONV_WIDTH:CONV_WIDTH + 1, :]
    for tap in range(CONV_WIDTH):
        start = SUBLANES - (CONV_WIDTH - 1) + tap
        conv = conv + convp[tap:tap + 1, :] * conv_scr[start:start + qp, :]
    act = _silu(conv)
    conv_scr[0:SUBLANES, :] = conv_scr[q_rows:q_rows + SUBLANES, :]

    xact = act[:, 0:GROUP_COLS]
    bact = act[:, GROUP_COLS:GROUP_COLS + D_STATE].astype(BF16)
    cact = act[:, GROUP_COLS + D_STATE:].astype(BF16)

    hp = hp_ref[0]
    dt_raw = dt_ref[...]
    if pad_rows:
        dt_raw = jnp.concatenate([dt_raw, jnp.zeros((pad_rows, LANES), F32)], axis=0)
    v = dt_raw + hp[0:1, :]
    dt = jnp.maximum(v, 0.0) + jnp.log1p(jnp.exp(-jnp.abs(v)))
    if pad_rows:
        row_id = lax.broadcasted_iota(jnp.int32, (qp, LANES), 0)
        dt = jnp.where(row_id < q_rows, dt, 0.0)
    a = dt * (-jnp.exp(hp[1:2, :]))

    ii = lax.broadcasted_iota(jnp.int32, (qp, qp), 0)
    jj = lax.broadcasted_iota(jnp.int32, (qp, qp), 1)
    causal = ii >= jj
    tril = causal.astype(F32)
    acum = jnp.dot(tril, a, precision=lax.Precision.HIGHEST, preferred_element_type=F32)
    acum_t = acum.T
    dt_t = dt.T
    last = acum[qp - 1:qp, :]
    e_acum = jnp.exp(acum)
    w_all = jnp.exp(last - acum) * dt
    e_last = jnp.exp(last)

    cb = lax.dot_general(cact, bact, (((1,), (1,)), ((), ())), preferred_element_type=F32)
    state = state_scr[...]
    y_off = lax.dot_general(cact, state.astype(BF16), (((1,), (1,)), ((), ())),
                            preferred_element_type=F32)

    for r in range(HEADS_PER_GROUP):
        sl = slice(r * SSM_HEADDIM, (r + 1) * SSM_HEADDIM)
        col = acum[:, r:r + 1]
        row = acum_t[r:r + 1, :]
        decay = jnp.exp(jnp.where(causal, col - row, -jnp.inf))
        m = cb * decay * dt_t[r:r + 1, :]
        xh = xact[:, sl]
        yd = jnp.dot(m.astype(BF16), xh.astype(BF16), preferred_element_type=F32)
        y_scr[:, sl] = yd + y_off[:, sl] * e_acum[:, r:r + 1] + hp[2:3, r:r + 1] * xh
        xw_scr[:, sl] = xh * w_all[:, r:r + 1]
        state_scr[sl, :] = state[sl, :] * e_last[:, r:r + 1]

    upd = lax.dot_general(xw_scr[...].astype(BF16), bact, (((0,), (0,)), ((), ())),
                          preferred_element_type=F32)
    state_scr[...] = state_scr[...] + upd

    y = y_scr[...]
    if pad_rows:
        y = y[0:q_rows, :]
    yg = y * _silu(z_ref[...])
    rn = lax.rsqrt(jnp.mean(yg * yg, axis=-1, keepdims=True) + EPS)
    y_ref[...] = (yg * rn * nw_ref[0]).astype(y_ref.dtype)

    @pl.when(c == n_chunks - 1)
    def _():
        sout_ref[0] = state_scr[...]


def _ssd_branch(rest, dt, hist, s0, convp, hp, ssm_nw, *, n_streams, q_rows, n_chunks):
    t = rest.shape[0]
    grid = (n_streams, SSM_GROUPS, n_chunks)

    def rows(b, g, c):
        return b * n_chunks + c

    kern = functools.partial(_ssd_kernel, q_rows=q_rows, n_chunks=n_chunks)
    return pl.pallas_call(
        kern,
        grid=grid,
        in_specs=[
            pl.BlockSpec((q_rows, GROUP_COLS), lambda b, g, c: (rows(b, g, c), R_X // GROUP_COLS + g)),
            pl.BlockSpec((q_rows, D_STATE), lambda b, g, c: (rows(b, g, c), R_B // D_STATE + g)),
            pl.BlockSpec((q_rows, D_STATE), lambda b, g, c: (rows(b, g, c), R_C // D_STATE + g)),
            pl.BlockSpec((q_rows, GROUP_COLS), lambda b, g, c: (rows(b, g, c), R_Z // GROUP_COLS + g)),
            pl.BlockSpec((q_rows, LANES), lambda b, g, c: (rows(b, g, c), g)),
            pl.BlockSpec((1, 1, SUBLANES, GROUP_CONV_COLS), lambda b, g, c: (b, g, 0, 0)),
            pl.BlockSpec((1, GROUP_COLS, D_STATE), lambda b, g, c: (b, g, 0)),
            pl.BlockSpec((1, SUBLANES, GROUP_CONV_COLS), lambda b, g, c: (g, 0, 0)),
            pl.BlockSpec((1, SUBLANES, LANES), lambda b, g, c: (g, 0, 0)),
            pl.BlockSpec((1, 1, GROUP_COLS), lambda b, g, c: (g, 0, 0)),
        ],
        out_specs=[
            pl.BlockSpec((q_rows, GROUP_COLS), lambda b, g, c: (rows(b, g, c), g)),
            pl.BlockSpec((1, GROUP_COLS, D_STATE), lambda b, g, c: (b, g, 0)),
        ],
        out_shape=[
            jax.ShapeDtypeStruct((t, D_SSM), BF16),
            jax.ShapeDtypeStruct((n_streams, D_SSM, D_STATE), F32),
        ],
        scratch_shapes=[
            pltpu.VMEM((SUBLANES + SSD_ROWS, GROUP_CONV_COLS), F32),
            pltpu.VMEM((GROUP_COLS, D_STATE), F32),
            pltpu.VMEM((SSD_ROWS, GROUP_COLS), F32),
            pltpu.VMEM((SSD_ROWS, GROUP_COLS), F32),
        ],
        compiler_params=_compiler_params(("parallel", "parallel", "arbitrary")),
        name="ssd_branch",
    )(rest, rest, rest, rest, dt, hist, s0, convp, hp, ssm_nw)


def _softmax_stage(s_scr, p_scr, bias_ref, h, q_rows):
    scale = ATT_HEAD_DIM ** -0.5
    slab = min(SOFTMAX_SLAB, q_rows)
    for r0 in range(0, q_rows, slab):
        s = s_scr[h, r0:r0 + slab, :] * scale + bias_ref[h, r0:r0 + slab, :]
        m = jnp.max(s, axis=-1, keepdims=True)
        e = jnp.exp(s - m)
        p = e * (1.0 / jnp.sum(e, axis=-1, keepdims=True))
        p_scr[h, r0:r0 + slab, :] = p.astype(BF16)


def _head_pipeline(qk_stage, softmax_stage, pv_stage):
    for step in range(ATT_HEADS + 2):
        if step < ATT_HEADS:
            qk_stage(step)
        if 1 <= step <= ATT_HEADS:
            softmax_stage(step - 1)
        if step >= 2:
            pv_stage(step - 2)


def _prompt_attn_kernel(q_ref, g_ref, k0, k1, k2, k3, k4, v0, v1, v2, v3, v4, bias_ref, o_ref,
                        s_scr, p_scr, *, n_hist_blocks):
    t = pl.program_id(0)
    k_refs = (k0, k1, k2, k3, k4)
    v_refs = (v0, v1, v2, v3, v4)
    n_blocks = n_hist_blocks + 1

    def qk_stage(h):
        qh = q_ref[0, h]
        for blk in range(n_blocks):
            s = lax.dot_general(qh, k_refs[blk][0, h], (((1,), (1,)), ((), ())), preferred_element_type=F32)
            s_scr[h, :, blk * ATT_Q_ROWS:(blk + 1) * ATT_Q_ROWS] = jnp.where(t + blk >= n_hist_blocks, s, MASK_VALUE)

    def pv_stage(h):
        sl = slice(h * ATT_HEAD_DIM, (h + 1) * ATT_HEAD_DIM)
        o = jnp.dot(p_scr[h, :, 0:ATT_Q_ROWS], v_refs[0][0, h], preferred_element_type=F32)
        for blk in range(1, n_blocks):
            o = o + jnp.dot(p_scr[h, :, blk * ATT_Q_ROWS:(blk + 1) * ATT_Q_ROWS], v_refs[blk][0, h],
                            preferred_element_type=F32)
        o_ref[:, sl] = (o * _silu(g_ref[:, sl])).astype(o_ref.dtype)

    _head_pipeline(qk_stage, functools.partial(_softmax_stage, s_scr, p_scr, bias_ref, q_rows=ATT_Q_ROWS), pv_stage)


def _prompt_attention(qkv, rest, bias):
    t = rest.shape[0]
    n_hist_blocks = BAND_ROWS // ATT_Q_ROWS
    n_blocks = n_hist_blocks + 1
    n_tiles = t // ATT_Q_ROWS
    span = BAND_ROWS + ATT_Q_ROWS
    head_block = (1, ATT_HEADS, ATT_Q_ROWS, ATT_HEAD_DIM)

    def kv_spec(which, blk):
        return pl.BlockSpec(head_block, lambda i: (which, 0, jnp.maximum(i - n_hist_blocks + blk, 0), 0))

    in_specs = [pl.BlockSpec(head_block, lambda i: (QKV_Q, 0, i, 0)),
                pl.BlockSpec((ATT_Q_ROWS, D_ATT), lambda i: (i, R_G // D_ATT))]
    in_specs += [kv_spec(QKV_K, blk) for blk in range(n_blocks)]
    in_specs += [kv_spec(QKV_V, blk) for blk in range(n_blocks)]
    in_specs += [pl.BlockSpec((ATT_HEADS, ATT_Q_ROWS, span), lambda i: (0, 0, 0))]
    kern = functools.partial(_prompt_attn_kernel, n_hist_blocks=n_hist_blocks)
    return pl.pallas_call(
        kern,
        grid=(n_tiles,),
        in_specs=in_specs,
        out_specs=pl.BlockSpec((ATT_Q_ROWS, D_ATT), lambda i: (i, 0)),
        out_shape=jax.ShapeDtypeStruct((t, D_ATT), BF16),
        scratch_shapes=[pltpu.VMEM((ATT_HEADS, ATT_Q_ROWS, span), F32),
                        pltpu.VMEM((ATT_HEADS, ATT_Q_ROWS, span), BF16)],
        compiler_params=_compiler_params(("parallel",)),
        name="prompt_attention",
    )(qkv, rest, *([qkv] * (2 * n_blocks)), bias)


def _sample_attn_kernel(q_ref, kn_ref, vn_ref, g_ref, kc_ref, vc_ref, bias_ref, o_ref, s_scr, p_scr, *, q_rows):
    new_pad = LANES - q_rows
    zpad = jnp.zeros((new_pad, ATT_HEAD_DIM), BF16)

    def qk_stage(h):
        sl = slice(h * ATT_HEAD_DIM, (h + 1) * ATT_HEAD_DIM)
        qh = q_ref[0, h]
        s_scr[h, :, 0:BAND_ROWS] = lax.dot_general(qh, kc_ref[0, :, sl], (((1,), (1,)), ((), ())),
                                                   preferred_element_type=F32)
        k_new = jnp.concatenate([kn_ref[0, h], zpad], axis=0)
        s_scr[h, :, BAND_ROWS:] = lax.dot_general(qh, k_new, (((1,), (1,)), ((), ())), preferred_element_type=F32)

    def pv_stage(h):
        sl = slice(h * ATT_HEAD_DIM, (h + 1) * ATT_HEAD_DIM)
        v_new = jnp.concatenate([vn_ref[0, h], zpad], axis=0)
        o = jnp.dot(p_scr[h, :, 0:BAND_ROWS], vc_ref[0, :, sl], preferred_element_type=F32)
        o = o + jnp.dot(p_scr[h, :, BAND_ROWS:], v_new, preferred_element_type=F32)
        o_ref[:, sl] = (o * _silu(g_ref[:, sl])).astype(o_ref.dtype)

    _head_pipeline(qk_stage, functools.partial(_softmax_stage, s_scr, p_scr, bias_ref, q_rows=q_rows), pv_stage)


def _sample_attention(qkv, rest, cache_k, cache_v, bias, *, n_streams, q_rows):
    t = rest.shape[0]
    span_pad = BAND_ROWS + LANES
    kern = functools.partial(_sample_attn_kernel, q_rows=q_rows)

    def new_spec(which):
        return pl.BlockSpec((1, ATT_HEADS, q_rows, ATT_HEAD_DIM), lambda b: (which, 0, b, 0))

    return pl.pallas_call(
        kern,
        grid=(n_streams,),
        in_specs=[
            new_spec(QKV_Q), new_spec(QKV_K), new_spec(QKV_V),
            pl.BlockSpec((q_rows, D_ATT), lambda b: (b, R_G // D_ATT)),
            pl.BlockSpec((1, BAND_ROWS, D_ATT), lambda b: (b, 0, 0)),
            pl.BlockSpec((1, BAND_ROWS, D_ATT), lambda b: (b, 0, 0)),
            pl.BlockSpec((ATT_HEADS, q_rows, span_pad), lambda b: (0, 0, 0)),
        ],
        out_specs=pl.BlockSpec((q_rows, D_ATT), lambda b: (b, 0)),
        out_shape=jax.ShapeDtypeStruct((t, D_ATT), BF16),
        scratch_shapes=[pltpu.VMEM((ATT_HEADS, q_rows, span_pad), F32),
                        pltpu.VMEM((ATT_HEADS, q_rows, span_pad), BF16)],
        compiler_params=_compiler_params(("parallel",)),
        name="sample_attention",
    )(qkv, qkv, qkv, rest, cache_k, cache_v, bias)


def _outproj_kernel(ys_ref, ya_ref, w1_ref, w2_ref, x_ref, o_ref):
    acc = jnp.dot(ys_ref[...], w1_ref[...], preferred_element_type=F32)
    acc = acc + jnp.dot(ya_ref[...], w2_ref[...], preferred_element_type=F32)
    o_ref[...] = x_ref[...] + acc


def _out_projection(y_ssm, y_att, w_out, x, *, tm, tn):
    t = x.shape[0]
    return pl.pallas_call(
        _outproj_kernel,
        grid=(t // tm, D_MODEL // tn),
        in_specs=[
            pl.BlockSpec((tm, D_SSM), lambda i, j: (i, 0)),
            pl.BlockSpec((tm, D_ATT), lambda i, j: (i, 0)),
            pl.BlockSpec((D_SSM, tn), lambda i, j: (0, j)),
            pl.BlockSpec((D_ATT, tn), lambda i, j: (D_SSM // D_ATT, j)),
            pl.BlockSpec((tm, tn), lambda i, j: (i, j)),
        ],
        out_specs=pl.BlockSpec((tm, tn), lambda i, j: (i, j)),
        out_shape=jax.ShapeDtypeStruct((t, D_MODEL), F32),
        compiler_params=_compiler_params(("parallel", "arbitrary")),
        name="out_projection",
    )(y_ssm, y_att, w_out, w_out, x)


def _group_conv_columns(v):
    lead = v.shape[:-1]
    xs = v[..., :D_SSM].reshape(*lead, SSM_GROUPS, GROUP_COLS)
    bs = v[..., D_SSM:D_SSM + SSM_GROUPS * D_STATE].reshape(*lead, SSM_GROUPS, D_STATE)
    cs = v[..., D_SSM + SSM_GROUPS * D_STATE:].reshape(*lead, SSM_GROUPS, D_STATE)
    return jnp.concatenate([xs, bs, cs], axis=-1)


def _pad_to(v, size, axis):
    pad = [(0, 0)] * v.ndim
    pad[axis] = (0, size - v.shape[axis])
    return jnp.pad(v, pad)


def _prepare_params(norm_w, w_in, conv_w, conv_b, dt_bias, a_log, d_skip, ssm_norm_w, q_norm_w, k_norm_w, w_out):
    w_qkv = w_in[:, OFF_Q:OFF_G].astype(BF16)
    w_g = w_in[:, OFF_G:].astype(BF16)
    w_a = w_in[:, OFF_Z:OFF_DT].astype(BF16)
    w_dt = w_in[:, OFF_DT:OFF_Q].reshape(D_MODEL, SSM_GROUPS, HEADS_PER_GROUP)
    w_dt = _pad_to(w_dt, LANES, 2).reshape(D_MODEL, DT_COLS).astype(BF16)
    qk_w = jnp.concatenate([jnp.tile(q_norm_w, ATT_HEADS), jnp.tile(k_norm_w, ATT_HEADS)]).reshape(1, 2 * D_ATT)
    taps = jnp.moveaxis(_group_conv_columns(conv_w), 1, 0)
    bias = _group_conv_columns(conv_b)[:, None, :]
    convp = _pad_to(jnp.concatenate([taps, bias], axis=1), SUBLANES, 1)

    def per_group(v):
        return _pad_to(v.reshape(SSM_GROUPS, 1, HEADS_PER_GROUP), LANES, 2)

    hp = _pad_to(jnp.concatenate([per_group(dt_bias), per_group(a_log), per_group(d_skip)], axis=1), SUBLANES, 1)
    ssm_nw = ssm_norm_w.reshape(SSM_GROUPS, 1, GROUP_COLS)
    return dict(norm_w=norm_w.reshape(1, D_MODEL), w_qkv=w_qkv, w_g=w_g, w_a=w_a, w_dt=w_dt, qk_w=qk_w,
                convp=convp, hp=hp, ssm_nw=ssm_nw, w_out=w_out.astype(BF16))


def _conv_history(conv_rows):
    grouped = jnp.moveaxis(_group_conv_columns(conv_rows), 2, 1)
    return jnp.pad(grouped, ((0, 0), (0, 0), (SUBLANES - (CONV_WIDTH - 1), 0), (0, 0)))


def _rel_bias_table(rel_bias, q_rows, hist_rows, span_pad, band_chunk):
    n_heads = rel_bias.shape[0]
    period = span_pad + q_rows
    k = np.arange(period)
    rel = np.clip(hist_rows + q_rows - 1 - k, -REL_CLIP, REL_CLIP) + REL_CLIP
    onehot = jnp.asarray(np.eye(N_REL, dtype=np.float32)[:, rel])
    diag_row = jnp.dot(rel_bias.astype(F32), onehot, precision=lax.Precision.HIGHEST)
    flat = jnp.tile(diag_row, (1, q_rows))[:, :q_rows * (period - 1)]
    table = flat.reshape(n_heads, q_rows, period - 1)[:, :, q_rows - 1:q_rows - 1 + span_pad]
    i_idx = np.arange(q_rows)[:, None]
    j_idx = np.arange(span_pad)[None, :]
    valid = j_idx < hist_rows + q_rows
    if band_chunk is not None:
        start = (i_idx // band_chunk) * band_chunk
        valid = valid & (j_idx >= start) & (j_idx < start + hist_rows + band_chunk)
    return jnp.where(jnp.asarray(np.broadcast_to(valid, (q_rows, span_pad)))[None], table, MASK_VALUE)


def _layer(x, hist, s0, cache_k, cache_v, prm, rel_bias, *, n_streams, seq, prompt):
    t = n_streams * seq
    x2 = x.reshape(t, D_MODEL)
    qkv, rest, dt = _in_projection(x2, prm["norm_w"], prm["w_qkv"], prm["w_g"], prm["w_a"], prm["w_dt"],
                                   prm["qk_w"], tm=512)
    if prompt:
        q_rows, n_chunks = SSD_ROWS, seq // SSD_ROWS
        kv_first, kv_rows = t - BAND_ROWS, BAND_ROWS
    else:
        q_rows, n_chunks = seq, 1
        kv_first, kv_rows = 0, t
    kv_new = _kv_rows(x2, prm["norm_w"], prm["w_qkv"], prm["qk_w"], first_row=kv_first, n_rows=kv_rows, tm=512)
    y_ssm, s_new = _ssd_branch(rest, dt, hist, s0, prm["convp"], prm["hp"], prm["ssm_nw"],
                               n_streams=n_streams, q_rows=q_rows, n_chunks=n_chunks)
    if prompt:
        bias = _rel_bias_table(rel_bias, ATT_Q_ROWS, BAND_ROWS, BAND_ROWS + ATT_Q_ROWS, CHUNK)
        y_att = _prompt_attention(qkv, rest, bias)
    else:
        bias = _rel_bias_table(rel_bias, seq, BAND_ROWS, BAND_ROWS + LANES, None)
        y_att = _sample_attention(qkv, rest, cache_k, cache_v, bias, n_streams=n_streams, q_rows=seq)
    y = _out_projection(y_ssm, y_att, prm["w_out"], x2, tm=512, tn=1024)
    new_conv = rest.reshape(n_streams, seq, R_COLS)[:, seq - (CONV_WIDTH - 1):, R_X:]
    kv_streams = kv_rows // n_streams
    kh = kv_new[:, :D_ATT].reshape(n_streams, kv_streams, ATT_HEADS, ATT_HEAD_DIM)
    vh = kv_new[:, D_ATT:].reshape(n_streams, kv_streams, ATT_HEADS, ATT_HEAD_DIM)
    new_ssm = s_new.reshape(n_streams, SSM_HEADS, SSM_HEADDIM, D_STATE)
    return y.reshape(n_streams, seq, D_MODEL), new_conv, new_ssm, kh, vh


def kernel(x_prompt, x_sample, state_conv, state_ssm, cache_k, cache_v, norm_w, w_in, conv_w, conv_b, dt_bias, a_log, d_skip, ssm_norm_w, q_norm_w, k_norm_w, rel_bias, w_out):
    bp, lp, _ = x_prompt.shape
    bs, ls, _ = x_sample.shape
    assert bp == 1 and lp % 512 == 0 and lp >= BAND_ROWS
    assert ls % (2 * SUBLANES) == 0 and ls <= LANES and (bs * ls) % 512 == 0 and cache_k.shape[2] == BAND_ROWS
    assert norm_w.shape[0] == 1
    prm = _prepare_params(norm_w[0], w_in[0], conv_w[0], conv_b[0], dt_bias[0], a_log[0], d_skip[0],
                          ssm_norm_w[0], q_norm_w[0], k_norm_w[0], w_out[0])
    rb = rel_bias[0]

    zero_hist = jnp.zeros((bp, SSM_GROUPS, SUBLANES, GROUP_CONV_COLS), F32)
    zero_state = jnp.zeros((bp, D_SSM, D_STATE), F32)
    yp, c1, s1, k1, v1 = _layer(x_prompt, zero_hist, zero_state, None, None, prm, rb,
                                n_streams=bp, seq=lp, prompt=True)

    hist = _conv_history(state_conv[0])
    s0 = state_ssm[0].reshape(bs, D_SSM, D_STATE)
    ck = cache_k[0].reshape(bs, BAND_ROWS, D_ATT).astype(BF16)
    cv = cache_v[0].reshape(bs, BAND_ROWS, D_ATT).astype(BF16)
    ys, c2, s2, k2, v2 = _layer(x_sample, hist, s0, ck, cv, prm, rb, n_streams=bs, seq=ls, prompt=False)

    return (yp, ys, c1[None], s1[None], k1[None], v1[None], c2[None], s2[None], k2[None], v2[None])
```

```python
import functools

import jax
import jax.numpy as jnp
import numpy as np
from jax import lax
from jax.experimental import pallas as pl
from jax.experimental.pallas import tpu as pltpu

F32 = jnp.float32
BF16 = jnp.bfloat16

D_MODEL = 2048
CHUNK = 64
LEFT_CHUNKS = 8
BAND_ROWS = LEFT_CHUNKS * CHUNK
D_SSM = 2048
D_ATT = 2048
SSM_HEADDIM = 64
SSM_HEADS = 32
SSM_GROUPS = 4
HEADS_PER_GROUP = SSM_HEADS // SSM_GROUPS
GROUP_COLS = HEADS_PER_GROUP * SSM_HEADDIM
D_STATE = 128
BC_COLS = 2 * SSM_GROUPS * D_STATE
CONV_WIDTH = 4
CONV_DIM = D_SSM + BC_COLS
ATT_HEAD_DIM = 128
ATT_HEADS = 16
REL_CLIP = 128
N_REL = 2 * REL_CLIP + 1
EPS = 1e-6
OFF_Z = 0
OFF_XBC = OFF_Z + D_SSM
OFF_DT = OFF_XBC + CONV_DIM
OFF_Q = OFF_DT + SSM_HEADS
OFF_K = OFF_Q + D_ATT
OFF_V = OFF_K + D_ATT
OFF_G = OFF_V + D_ATT

LANES = 128
SUBLANES = 8
VMEM_LIMIT_BYTES = 56 * 1024 * 1024

R_G = 0
R_Z = R_G + D_ATT
R_X = R_Z + D_SSM
R_BC = R_X + D_SSM
R_COLS = R_BC + BC_COLS
DT_COLS = LANES
QKV_Q, QKV_K, QKV_V = 0, 1, 2
N_SPLIT = 3

PROJ_TN = 1024
SSD_ROWS = 128
ATT_Q_ROWS = 128
SOFTMAX_SLAB = 32
MASK_VALUE = -1e30


def _silu(v):
    return v * (1.0 / (1.0 + jnp.exp(-v)))


def _compiler_params(semantics):
    return pltpu.CompilerParams(dimension_semantics=semantics, vmem_limit_bytes=VMEM_LIMIT_BYTES)


def _normed_rows(x_ref, nw_ref):
    x = x_ref[...]
    ms = jnp.mean(x * x, axis=-1, keepdims=True)
    return (x * lax.rsqrt(ms + EPS) * nw_ref[...]).astype(BF16)


def _head_norm(a, use_norm, w):
    r = lax.rsqrt(jnp.mean(a * a, axis=-1, keepdims=True) + EPS)
    return a * jnp.where(use_norm, r, 1.0) * jnp.where(use_norm, w, 1.0)


def _inproj_kernel(x_ref, nw_ref, wqkvg_ref, wa_ref, wdt_ref, qkw_ref,
                   qkv_ref, rest_ref, dt_ref, h_scr, *, n_qk, n_qkv, n_g):
    j = pl.program_id(1)
    heads_per_tile = PROJ_TN // ATT_HEAD_DIM

    @pl.when(j == 0)
    def _():
        h = _normed_rows(x_ref, nw_ref)
        h_scr[...] = h
        dt_ref[...] = jnp.dot(h, wdt_ref[...], preferred_element_type=F32)

    @pl.when(j < n_qkv)
    def _():
        acc = jnp.dot(h_scr[...], wqkvg_ref[...], preferred_element_type=F32)
        for hh in range(heads_per_tile):
            sl = slice(hh * ATT_HEAD_DIM, (hh + 1) * ATT_HEAD_DIM)
            qkv_ref[0, hh] = _head_norm(acc[:, sl], j < n_qk, qkw_ref[:, sl]).astype(BF16)

    @pl.when(jnp.logical_and(j >= n_qkv, j < n_qkv + n_g))
    def _():
        rest_ref[...] = jnp.dot(h_scr[...], wqkvg_ref[...], preferred_element_type=F32)

    @pl.when(j >= n_qkv + n_g)
    def _():
        rest_ref[...] = jnp.dot(h_scr[...], wa_ref[...], preferred_element_type=F32)


def _in_projection(x, norm_w, w_qkvg, w_a, w_dt, qk_w, *, tm):
    t = x.shape[0]
    tn = PROJ_TN
    tiles_per_proj = D_ATT // tn
    n_qk, n_qkv, n_g = 2 * tiles_per_proj, 3 * tiles_per_proj, tiles_per_proj
    n_a = (D_SSM + CONV_DIM) // tn
    heads_per_tile = tn // ATT_HEAD_DIM
    grid = (t // tm, n_qkv + n_g + n_a)
    kern = functools.partial(_inproj_kernel, n_qk=n_qk, n_qkv=n_qkv, n_g=n_g)

    def qkv_index(i, j):
        jj = jnp.minimum(j, n_qkv - 1)
        return (jj // tiles_per_proj, jj % tiles_per_proj, i, 0)

    return pl.pallas_call(
        kern,
        grid=grid,
        in_specs=[
            pl.BlockSpec((tm, D_MODEL), lambda i, j: (i, 0)),
            pl.BlockSpec((1, D_MODEL), lambda i, j: (0, 0)),
            pl.BlockSpec((D_MODEL, tn), lambda i, j: (0, jnp.minimum(j, n_qkv + n_g - 1))),
            pl.BlockSpec((D_MODEL, tn), lambda i, j: (0, jnp.clip(j - n_qkv - n_g, 0, n_a - 1))),
            pl.BlockSpec((D_MODEL, DT_COLS), lambda i, j: (0, 0)),
            pl.BlockSpec((1, tn), lambda i, j: (0, jnp.minimum(j, n_qk - 1))),
        ],
        out_specs=[
            pl.BlockSpec((1, heads_per_tile, tm, ATT_HEAD_DIM), qkv_index),
            pl.BlockSpec((tm, tn), lambda i, j: (i, jnp.maximum(j - n_qkv, 0))),
            pl.BlockSpec((tm, DT_COLS), lambda i, j: (i, 0)),
        ],
        out_shape=[
            jax.ShapeDtypeStruct((3, ATT_HEADS, t, ATT_HEAD_DIM), BF16),
            jax.ShapeDtypeStruct((t, R_COLS), F32),
            jax.ShapeDtypeStruct((t, DT_COLS), F32),
        ],
        scratch_shapes=[pltpu.VMEM((tm, D_MODEL), BF16)],
        compiler_params=_compiler_params(("parallel", "arbitrary")),
        name="in_projection",
    )(x, norm_w, w_qkvg, w_a, w_dt, qk_w)


def _kv_rows_kernel(x_ref, nw_ref, w_ref, qkw_ref, o_ref, h_scr, *, n_k):
    j = pl.program_id(1)

    @pl.when(j == 0)
    def _():
        h_scr[...] = _normed_rows(x_ref, nw_ref)

    acc = jnp.dot(h_scr[...], w_ref[...], preferred_element_type=F32)
    for hh in range(PROJ_TN // ATT_HEAD_DIM):
        sl = slice(hh * ATT_HEAD_DIM, (hh + 1) * ATT_HEAD_DIM)
        o_ref[:, sl] = _head_norm(acc[:, sl], j < n_k, qkw_ref[:, sl])


def _kv_rows(x, norm_w, w_qkvg, qk_w, *, first_row, n_rows, tm):
    tn = PROJ_TN
    tiles_per_proj = D_ATT // tn
    row0 = first_row // tm
    kern = functools.partial(_kv_rows_kernel, n_k=tiles_per_proj)
    return pl.pallas_call(
        kern,
        grid=(n_rows // tm, 2 * tiles_per_proj),
        in_specs=[
            pl.BlockSpec((tm, D_MODEL), lambda i, j: (row0 + i, 0)),
            pl.BlockSpec((1, D_MODEL), lambda i, j: (0, 0)),
            pl.BlockSpec((D_MODEL, tn), lambda i, j: (0, tiles_per_proj + j)),
            pl.BlockSpec((1, tn), lambda i, j: (0, jnp.minimum(tiles_per_proj + j, 2 * tiles_per_proj - 1))),
        ],
        out_specs=pl.BlockSpec((tm, tn), lambda i, j: (i, j)),
        out_shape=jax.ShapeDtypeStruct((n_rows, 2 * D_ATT), F32),
        scratch_shapes=[pltpu.VMEM((tm, D_MODEL), BF16)],
        compiler_params=_compiler_params(("parallel", "arbitrary")),
        name="kv_rows",
    )(x, norm_w, w_qkvg, qk_w)


def _transpose_rows_to_lanes(v):
    q = v.shape[0]
    if q < LANES:
        v = jnp.concatenate([v, jnp.zeros((LANES - q, LANES), v.dtype)], axis=0)
    return v.T[:, 0:q]


def _split_bf16(v):
    pieces = []
    rem = v
    for _ in range(N_SPLIT):
        piece = rem.astype(BF16)
        pieces.append(piece)
        rem = rem - piece.astype(F32)
    return jnp.concatenate(pieces, axis=1)


def _ssd_kernel(x_ref, bc_ref, z_ref, dt_ref, hist_ref, s0_ref, convp_ref, hp_ref, dexp_ref, nw_ref, expand_ref,
                y_ref, sout_ref, conv_scr, act_scr, st_scr, yd_scr, exp_scr, *, q_rows, n_chunks):
    c = pl.program_id(1)
    q = q_rows

    @pl.when(c == 0)
    def _():
        conv_scr[0:SUBLANES, :] = hist_ref[0]
        for g in range(SSM_GROUPS):
            st_scr[g] = s0_ref[0, g * GROUP_COLS:(g + 1) * GROUP_COLS, :].T

    conv_scr[SUBLANES:SUBLANES + q, 0:D_SSM] = x_ref[...]
    conv_scr[SUBLANES:SUBLANES + q, D_SSM:] = bc_ref[...]

    for c0 in range(0, CONV_DIM, GROUP_COLS):
        cols = slice(c0, c0 + GROUP_COLS)
        xp = conv_scr[:, cols]
        conv = convp_ref[CONV_WIDTH:CONV_WIDTH + 1, cols] + convp_ref[CONV_WIDTH - 1:CONV_WIDTH, cols] * xp[SUBLANES:]
        for shift in range(1, CONV_WIDTH):
            tap = CONV_WIDTH - 1 - shift
            conv = conv + convp_ref[tap:tap + 1, cols] * pltpu.roll(xp, shift, axis=0)[SUBLANES:]
        act_scr[:, cols] = _silu(conv)
    conv_scr[0:SUBLANES, :] = conv_scr[q:q + SUBLANES, :]

    v = dt_ref[...] + hp_ref[0:1, :]
    dt = jnp.maximum(v, 0.0) + jnp.log1p(jnp.exp(-jnp.abs(v)))
    a = dt * (-jnp.exp(hp_ref[1:2, :]))
    ii = lax.broadcasted_iota(jnp.int32, (q, q), 0)
    jj = lax.broadcasted_iota(jnp.int32, (q, q), 1)
    causal = ii >= jj
    acum = jnp.dot(causal.astype(F32), a, precision=lax.Precision.HIGHEST, preferred_element_type=F32)
    acum_t = _transpose_rows_to_lanes(acum)
    dt_t = _transpose_rows_to_lanes(dt)
    last = acum[q - 1:q, :]
    factors = jnp.concatenate(
        [jnp.exp(acum),
         jnp.exp(last - acum) * dt,
         jnp.broadcast_to(jnp.exp(last), (SUBLANES, LANES))], axis=0)
    exp_scr[...] = jnp.dot(_split_bf16(factors), expand_ref[...], preferred_element_type=F32)

    for g in range(SSM_GROUPS):
        cols = slice(g * GROUP_COLS, (g + 1) * GROUP_COLS)
        xact = act_scr[:, cols]
        bact = act_scr[:, D_SSM + g * D_STATE:D_SSM + (g + 1) * D_STATE].astype(BF16)
        cact = act_scr[:, D_SSM + (SSM_GROUPS + g) * D_STATE:D_SSM + (SSM_GROUPS + g + 1) * D_STATE].astype(BF16)
        cb = lax.dot_general(cact, bact, (((1,), (1,)), ((), ())), preferred_element_type=F32)
        st = st_scr[g]
        y_off = jnp.dot(cact, st.astype(BF16), preferred_element_type=F32)
        for r in range(HEADS_PER_GROUP):
            h = g * HEADS_PER_GROUP + r
            decay = jnp.exp(jnp.where(causal, acum[:, h:h + 1] - acum_t[h:h + 1, :], -jnp.inf))
            m = cb * decay * dt_t[h:h + 1, :]
            xh = xact[:, r * SSM_HEADDIM:(r + 1) * SSM_HEADDIM]
            yd_scr[:, h * SSM_HEADDIM:(h + 1) * SSM_HEADDIM] = jnp.dot(
                m.astype(BF16), xh.astype(BF16), preferred_element_type=F32)
        y = yd_scr[:, cols] + y_off * exp_scr[0:q, cols] + dexp_ref[:, cols] * xact
        xw = (xact * exp_scr[q:2 * q, cols]).astype(BF16)
        upd = lax.dot_general(bact, xw, (((0,), (0,)), ((), ())), preferred_element_type=F32)
        st_scr[g] = st * exp_scr[2 * q:2 * q + 1, cols] + upd

        yg = y * _silu(z_ref[:, cols])
        rn = lax.rsqrt(jnp.mean(yg * yg, axis=-1, keepdims=True) + EPS)
        y_ref[:, cols] = (yg * rn * nw_ref[:, cols]).astype(y_ref.dtype)

    @pl.when(c == n_chunks - 1)
    def _():
        for g in range(SSM_GROUPS):
            sout_ref[0, g * GROUP_COLS:(g + 1) * GROUP_COLS, :] = st_scr[g].T


def _ssd_branch(rest, dt, hist, s0, convp, hp, dexp, ssm_nw, expand, *, n_streams, q_rows, n_chunks):
    t = rest.shape[0]

    def rows(b, c):
        return b * n_chunks + c

    def const(shape):
        return pl.BlockSpec(shape, lambda b, c: (0,) * len(shape))

    kern = functools.partial(_ssd_kernel, q_rows=q_rows, n_chunks=n_chunks)
    return pl.pallas_call(
        kern,
        grid=(n_streams, n_chunks),
        in_specs=[
            pl.BlockSpec((q_rows, D_SSM), lambda b, c: (rows(b, c), R_X // D_SSM)),
            pl.BlockSpec((q_rows, BC_COLS), lambda b, c: (rows(b, c), R_BC // BC_COLS)),
            pl.BlockSpec((q_rows, D_SSM), lambda b, c: (rows(b, c), R_Z // D_SSM)),
            pl.BlockSpec((q_rows, DT_COLS), lambda b, c: (rows(b, c), 0)),
            pl.BlockSpec((1, SUBLANES, CONV_DIM), lambda b, c: (b, 0, 0)),
            pl.BlockSpec((1, D_SSM, D_STATE), lambda b, c: (b, 0, 0)),
            const((SUBLANES, CONV_DIM)),
            const((SUBLANES, LANES)),
            const((1, D_SSM)),
            const((1, D_SSM)),
            const((N_SPLIT * LANES, D_SSM)),
        ],
        out_specs=[
            pl.BlockSpec((q_rows, D_SSM), lambda b, c: (rows(b, c), 0)),
            pl.BlockSpec((1, D_SSM, D_STATE), lambda b, c: (b, 0, 0)),
        ],
        out_shape=[
            jax.ShapeDtypeStruct((t, D_SSM), BF16),
            jax.ShapeDtypeStruct((n_streams, D_SSM, D_STATE), F32),
        ],
        scratch_shapes=[
            pltpu.VMEM((SUBLANES + q_rows, CONV_DIM), F32),
            pltpu.VMEM((q_rows, CONV_DIM), F32),
            pltpu.VMEM((SSM_GROUPS, D_STATE, GROUP_COLS), F32),
            pltpu.VMEM((q_rows, D_SSM), F32),
            pltpu.VMEM((2 * q_rows + SUBLANES, D_SSM), F32),
        ],
        compiler_params=_compiler_params(("parallel", "arbitrary")),
        name="ssd_branch",
    )(rest, rest, rest, dt, hist, s0, convp, hp, dexp, ssm_nw, expand)


def _softmax_stage(s_scr, p_scr, bias_ref, h, q_rows):
    scale = ATT_HEAD_DIM ** -0.5
    slab = min(SOFTMAX_SLAB, q_rows)
    for r0 in range(0, q_rows, slab):
        s = s_scr[h, r0:r0 + slab, :] * scale + bias_ref[h, r0:r0 + slab, :]
        m = jnp.max(s, axis=-1, keepdims=True)
        e = jnp.exp(s - m)
        p = e * (1.0 / jnp.sum(e, axis=-1, keepdims=True))
        p_scr[h, r0:r0 + slab, :] = p.astype(BF16)


def _head_pipeline(qk_stage, softmax_stage, pv_stage):
    for step in range(ATT_HEADS + 2):
        if step < ATT_HEADS:
            qk_stage(step)
        if 1 <= step <= ATT_HEADS:
            softmax_stage(step - 1)
        if step >= 2:
            pv_stage(step - 2)


def _prompt_attn_kernel(q_ref, g_ref, k0, k1, k2, k3, k4, v0, v1, v2, v3, v4, bias_ref, o_ref,
                        s_scr, p_scr, *, n_hist_blocks):
    t = pl.program_id(0)
    k_refs = (k0, k1, k2, k3, k4)
    v_refs = (v0, v1, v2, v3, v4)
    n_blocks = n_hist_blocks + 1

    def qk_stage(h):
        qh = q_ref[0, h]
        for blk in range(n_blocks):
            s = lax.dot_general(qh, k_refs[blk][0, h], (((1,), (1,)), ((), ())), preferred_element_type=F32)
            s_scr[h, :, blk * ATT_Q_ROWS:(blk + 1) * ATT_Q_ROWS] = jnp.where(t + blk >= n_hist_blocks, s, MASK_VALUE)

    def pv_stage(h):
        sl = slice(h * ATT_HEAD_DIM, (h + 1) * ATT_HEAD_DIM)
        o = jnp.dot(p_scr[h, :, 0:ATT_Q_ROWS], v_refs[0][0, h], preferred_element_type=F32)
        for blk in range(1, n_blocks):
            o = o + jnp.dot(p_scr[h, :, blk * ATT_Q_ROWS:(blk + 1) * ATT_Q_ROWS], v_refs[blk][0, h],
                            preferred_element_type=F32)
        o_ref[:, sl] = (o * _silu(g_ref[:, sl])).astype(o_ref.dtype)

    _head_pipeline(qk_stage, functools.partial(_softmax_stage, s_scr, p_scr, bias_ref, q_rows=ATT_Q_ROWS), pv_stage)


def _prompt_attention(qkv, rest, bias):
    t = rest.shape[0]
    n_hist_blocks = BAND_ROWS // ATT_Q_ROWS
    n_blocks = n_hist_blocks + 1
    n_tiles = t // ATT_Q_ROWS
    span = BAND_ROWS + ATT_Q_ROWS
    head_block = (1, ATT_HEADS, ATT_Q_ROWS, ATT_HEAD_DIM)

    def kv_spec(which, blk):
        return pl.BlockSpec(head_block, lambda i: (which, 0, jnp.maximum(i - n_hist_blocks + blk, 0), 0))

    in_specs = [pl.BlockSpec(head_block, lambda i: (QKV_Q, 0, i, 0)),
                pl.BlockSpec((ATT_Q_ROWS, D_ATT), lambda i: (i, R_G // D_ATT))]
    in_specs += [kv_spec(QKV_K, blk) for blk in range(n_blocks)]
    in_specs += [kv_spec(QKV_V, blk) for blk in range(n_blocks)]
    in_specs += [pl.BlockSpec((ATT_HEADS, ATT_Q_ROWS, span), lambda i: (0, 0, 0))]
    kern = functools.partial(_prompt_attn_kernel, n_hist_blocks=n_hist_blocks)
    return pl.pallas_call(
        kern,
        grid=(n_tiles,),
        in_specs=in_specs,
        out_specs=pl.BlockSpec((ATT_Q_ROWS, D_ATT), lambda i: (i, 0)),
        out_shape=jax.ShapeDtypeStruct((t, D_ATT), BF16),
        scratch_shapes=[pltpu.VMEM((ATT_HEADS, ATT_Q_ROWS, span), F32),
                        pltpu.VMEM((ATT_HEADS, ATT_Q_ROWS, span), BF16)],
        compiler_params=_compiler_params(("parallel",)),
        name="prompt_attention",
    )(qkv, rest, *([qkv] * (2 * n_blocks)), bias)


def _sample_attn_kernel(q_ref, kn_ref, vn_ref, g_ref, kc_ref, vc_ref, bias_ref, o_ref, s_scr, p_scr, *, q_rows):
    new_pad = LANES - q_rows
    zpad = jnp.zeros((new_pad, ATT_HEAD_DIM), BF16)

    def cached(ref, h):
        return ref[0, pl.ds(h, BAND_ROWS, stride=ATT_HEADS), :].astype(BF16)

    def qk_stage(h):
        qh = q_ref[0, h]
        s_scr[h, :, 0:BAND_ROWS] = lax.dot_general(qh, cached(kc_ref, h), (((1,), (1,)), ((), ())),
                                                   preferred_element_type=F32)
        k_new = jnp.concatenate([kn_ref[0, h], zpad], axis=0)
        s_scr[h, :, BAND_ROWS:] = lax.dot_general(qh, k_new, (((1,), (1,)), ((), ())), preferred_element_type=F32)

    def pv_stage(h):
        sl = slice(h * ATT_HEAD_DIM, (h + 1) * ATT_HEAD_DIM)
        v_new = jnp.concatenate([vn_ref[0, h], zpad], axis=0)
        o = jnp.dot(p_scr[h, :, 0:BAND_ROWS], cached(vc_ref, h), preferred_element_type=F32)
        o = o + jnp.dot(p_scr[h, :, BAND_ROWS:], v_new, preferred_element_type=F32)
        o_ref[:, sl] = (o * _silu(g_ref[:, sl])).astype(o_ref.dtype)

    _head_pipeline(qk_stage, functools.partial(_softmax_stage, s_scr, p_scr, bias_ref, q_rows=q_rows), pv_stage)


def _sample_attention(qkv, rest, cache_k, cache_v, bias, *, n_streams, q_rows):
    t = rest.shape[0]
    span_pad = BAND_ROWS + LANES
    kern = functools.partial(_sample_attn_kernel, q_rows=q_rows)

    def new_spec(which):
        return pl.BlockSpec((1, ATT_HEADS, q_rows, ATT_HEAD_DIM), lambda b: (which, 0, b, 0))

    cache_spec = pl.BlockSpec((1, BAND_ROWS * ATT_HEADS, ATT_HEAD_DIM), lambda b: (b, 0, 0))
    return pl.pallas_call(
        kern,
        grid=(n_streams,),
        in_specs=[
            new_spec(QKV_Q), new_spec(QKV_K), new_spec(QKV_V),
            pl.BlockSpec((q_rows, D_ATT), lambda b: (b, R_G // D_ATT)),
            cache_spec, cache_spec,
            pl.BlockSpec((ATT_HEADS, q_rows, span_pad), lambda b: (0, 0, 0)),
        ],
        out_specs=pl.BlockSpec((q_rows, D_ATT), lambda b: (b, 0)),
        out_shape=jax.ShapeDtypeStruct((t, D_ATT), BF16),
        scratch_shapes=[pltpu.VMEM((ATT_HEADS, q_rows, span_pad), F32),
                        pltpu.VMEM((ATT_HEADS, q_rows, span_pad), BF16)],
        compiler_params=_compiler_params(("parallel",)),
        name="sample_attention",
    )(qkv, qkv, qkv, rest, cache_k, cache_v, bias)


def _outproj_kernel(ys_ref, ya_ref, w1_ref, w2_ref, x_ref, o_ref):
    acc = jnp.dot(ys_ref[...], w1_ref[...], preferred_element_type=F32)
    acc = acc + jnp.dot(ya_ref[...], w2_ref[...], preferred_element_type=F32)
    o_ref[...] = x_ref[...] + acc


def _out_projection(y_ssm, y_att, w_out, x, *, tm, tn):
    t = x.shape[0]
    return pl.pallas_call(
        _outproj_kernel,
        grid=(t // tm, D_MODEL // tn),
        in_specs=[
            pl.BlockSpec((tm, D_SSM), lambda i, j: (i, 0)),
            pl.BlockSpec((tm, D_ATT), lambda i, j: (i, 0)),
            pl.BlockSpec((D_SSM, tn), lambda i, j: (0, j)),
            pl.BlockSpec((D_ATT, tn), lambda i, j: (D_SSM // D_ATT, j)),
            pl.BlockSpec((tm, tn), lambda i, j: (i, j)),
        ],
        out_specs=pl.BlockSpec((tm, tn), lambda i, j: (i, j)),
        out_shape=jax.ShapeDtypeStruct((t, D_MODEL), F32),
        compiler_params=_compiler_params(("parallel", "arbitrary")),
        name="out_projection",
    )(y_ssm, y_att, w_out, w_out, x)


def _pad_to(v, size, axis):
    pad = [(0, 0)] * v.ndim
    pad[axis] = (0, size - v.shape[axis])
    return jnp.pad(v, pad)


def _prepare_params(norm_w, w_in, conv_w, conv_b, dt_bias, a_log, d_skip, ssm_norm_w, q_norm_w, k_norm_w, w_out):
    w_qkvg = w_in[:, OFF_Q:].astype(BF16)
    w_a = w_in[:, OFF_Z:OFF_DT].astype(BF16)
    w_dt = _pad_to(w_in[:, OFF_DT:OFF_Q], DT_COLS, 1).astype(BF16)
    qk_w = jnp.concatenate([jnp.tile(q_norm_w, ATT_HEADS), jnp.tile(k_norm_w, ATT_HEADS)]).reshape(1, 2 * D_ATT)
    convp = _pad_to(jnp.concatenate([conv_w, conv_b[None]], axis=0), SUBLANES, 0)
    hp = _pad_to(_pad_to(jnp.stack([dt_bias, a_log]), LANES, 1), SUBLANES, 0)
    dexp = jnp.repeat(d_skip, SSM_HEADDIM).reshape(1, D_SSM)
    head_of_col = np.arange(D_SSM) // SSM_HEADDIM
    expand = (np.arange(LANES)[:, None] == head_of_col[None, :]).astype(np.float32)
    expand = jnp.asarray(np.tile(expand, (N_SPLIT, 1)), dtype=BF16)
    return dict(norm_w=norm_w.reshape(1, D_MODEL), w_qkvg=w_qkvg, w_a=w_a, w_dt=w_dt, qk_w=qk_w,
                convp=convp, hp=hp, dexp=dexp, ssm_nw=ssm_norm_w.reshape(1, D_SSM), expand=expand,
                w_out=w_out.astype(BF16))


def _rel_bias_table(rel_bias, q_rows, hist_rows, span_pad, band_chunk):
    n_heads = rel_bias.shape[0]
    period = span_pad + q_rows
    k = np.arange(period)
    rel = np.clip(hist_rows + q_rows - 1 - k, -REL_CLIP, REL_CLIP) + REL_CLIP
    onehot = jnp.asarray(np.eye(N_REL, dtype=np.float32)[:, rel])
    diag_row = jnp.dot(rel_bias.astype(F32), onehot, precision=lax.Precision.HIGHEST)
    flat = jnp.tile(diag_row, (1, q_rows))[:, :q_rows * (period - 1)]
    table = flat.reshape(n_heads, q_rows, period - 1)[:, :, q_rows - 1:q_rows - 1 + span_pad]
    i_idx = np.arange(q_rows)[:, None]
    j_idx = np.arange(span_pad)[None, :]
    valid = j_idx < hist_rows + q_rows
    if band_chunk is not None:
        start = (i_idx // band_chunk) * band_chunk
        valid = valid & (j_idx >= start) & (j_idx < start + hist_rows + band_chunk)
    return jnp.where(jnp.asarray(np.broadcast_to(valid, (q_rows, span_pad)))[None], table, MASK_VALUE)


def _layer(x, hist, s0, cache_k, cache_v, prm, rel_bias, *, n_streams, seq, prompt, proj_tm):
    t = n_streams * seq
    x2 = x.reshape(t, D_MODEL)
    qkv, rest, dt = _in_projection(x2, prm["norm_w"], prm["w_qkvg"], prm["w_a"], prm["w_dt"], prm["qk_w"],
                                   tm=proj_tm)
    if prompt:
        q_rows, n_chunks = SSD_ROWS, seq // SSD_ROWS
        kv_first, kv_rows = t - BAND_ROWS, BAND_ROWS
    else:
        q_rows, n_chunks = seq, 1
        kv_first, kv_rows = 0, t
    kv_new = _kv_rows(x2, prm["norm_w"], prm["w_qkvg"], prm["qk_w"], first_row=kv_first, n_rows=kv_rows, tm=512)
    y_ssm, s_new = _ssd_branch(rest, dt, hist, s0, prm["convp"], prm["hp"], prm["dexp"], prm["ssm_nw"],
                               prm["expand"], n_streams=n_streams, q_rows=q_rows, n_chunks=n_chunks)
    if prompt:
        bias = _rel_bias_table(rel_bias, ATT_Q_ROWS, BAND_ROWS, BAND_ROWS + ATT_Q_ROWS, CHUNK)
        y_att = _prompt_attention(qkv, rest, bias)
    else:
        bias = _rel_bias_table(rel_bias, seq, BAND_ROWS, BAND_ROWS + LANES, None)
        y_att = _sample_attention(qkv, rest, cache_k, cache_v, bias, n_streams=n_streams, q_rows=seq)
    y = _out_projection(y_ssm, y_att, prm["w_out"], x2, tm=512, tn=1024)
    new_conv = rest.reshape(n_streams, seq, R_COLS)[:, seq - (CONV_WIDTH - 1):, R_X:]
    kv_streams = kv_rows // n_streams
    kh = kv_new[:, :D_ATT].reshape(n_streams, kv_streams, ATT_HEADS, ATT_HEAD_DIM)
    vh = kv_new[:, D_ATT:].reshape(n_streams, kv_streams, ATT_HEADS, ATT_HEAD_DIM)
    new_ssm = s_new.reshape(n_streams, SSM_HEADS, SSM_HEADDIM, D_STATE)
    return y.reshape(n_streams, seq, D_MODEL), new_conv, new_ssm, kh, vh


def kernel(x_prompt, x_sample, state_conv, state_ssm, cache_k, cache_v, norm_w, w_in, conv_w, conv_b, dt_bias, a_log, d_skip, ssm_norm_w, q_norm_w, k_norm_w, rel_bias, w_out):
    bp, lp, _ = x_prompt.shape
    bs, ls, _ = x_sample.shape
    assert bp == 1 and lp % 1024 == 0 and lp >= BAND_ROWS
    assert ls % (2 * SUBLANES) == 0 and ls <= LANES and (bs * ls) % 512 == 0 and cache_k.shape[2] == BAND_ROWS
    assert norm_w.shape[0] == 1
    prm = _prepare_params(norm_w[0], w_in[0], conv_w[0], conv_b[0], dt_bias[0], a_log[0], d_skip[0],
                          ssm_norm_w[0], q_norm_w[0], k_norm_w[0], w_out[0])
    rb = rel_bias[0]

    zero_hist = jnp.zeros((bp, SUBLANES, CONV_DIM), F32)
    zero_state = jnp.zeros((bp, D_SSM, D_STATE), F32)
    yp, c1, s1, k1, v1 = _layer(x_prompt, zero_hist, zero_state, None, None, prm, rb,
                                n_streams=bp, seq=lp, prompt=True, proj_tm=1024)

    hist = jnp.pad(state_conv[0], ((0, 0), (SUBLANES - (CONV_WIDTH - 1), 0), (0, 0)))
    s0 = state_ssm[0].reshape(bs, D_SSM, D_STATE)
    ck = cache_k[0].reshape(bs, BAND_ROWS * ATT_HEADS, ATT_HEAD_DIM)
    cv = cache_v[0].reshape(bs, BAND_ROWS * ATT_HEADS, ATT_HEAD_DIM)
    ys, c2, s2, k2, v2 = _layer(x_sample, hist, s0, ck, cv, prm, rb, n_streams=bs, seq=ls, prompt=False,
                                proj_tm=512)

    return (yp, ys, c1[None], s1[None], k1[None], v1[None], c2[None], s2[None], k2[None], v2[None])
```

```python
import functools

import jax
import jax.numpy as jnp
import numpy as np
from jax import lax
from jax.experimental import pallas as pl
from jax.experimental.pallas import tpu as pltpu

F32 = jnp.float32
BF16 = jnp.bfloat16

D_MODEL = 2048
CHUNK = 64
LEFT_CHUNKS = 8
BAND_ROWS = LEFT_CHUNKS * CHUNK
D_SSM = 2048
D_ATT = 2048
SSM_HEADDIM = 64
SSM_HEADS = 32
SSM_GROUPS = 4
HEADS_PER_GROUP = SSM_HEADS // SSM_GROUPS
GROUP_COLS = HEADS_PER_GROUP * SSM_HEADDIM
D_STATE = 128
BC_COLS = 2 * SSM_GROUPS * D_STATE
CONV_WIDTH = 4
CONV_DIM = D_SSM + BC_COLS
ATT_HEAD_DIM = 128
ATT_HEADS = 16
REL_CLIP = 128
N_REL = 2 * REL_CLIP + 1
EPS = 1e-6
OFF_Z = 0
OFF_XBC = OFF_Z + D_SSM
OFF_DT = OFF_XBC + CONV_DIM
OFF_Q = OFF_DT + SSM_HEADS
OFF_K = OFF_Q + D_ATT
OFF_V = OFF_K + D_ATT
OFF_G = OFF_V + D_ATT

LANES = 128
SUBLANES = 8
VMEM_LIMIT_BYTES = 56 * 1024 * 1024

R_G = 0
R_Z = R_G + D_ATT
R_X = R_Z + D_SSM
R_BC = R_X + D_SSM
R_COLS = R_BC + BC_COLS
DT_COLS = LANES
QKV_Q, QKV_K, QKV_V = 0, 1, 2
N_SPLIT = 3

PROJ_TN = 1024
SSD_ROWS = 128
ATT_Q_ROWS = 128
SOFTMAX_SLAB = 32
MASK_VALUE = -1e30
ATT_SCALE = ATT_HEAD_DIM ** -0.5
LOG2E = 1.4426950408889634


def _silu(v):
    return v * (1.0 / (1.0 + jnp.exp(-v)))


def _compiler_params(semantics):
    return pltpu.CompilerParams(dimension_semantics=semantics, vmem_limit_bytes=VMEM_LIMIT_BYTES)


def _normed_rows(x_ref, nw_ref):
    x = x_ref[...]
    ms = jnp.mean(x * x, axis=-1, keepdims=True)
    return (x * lax.rsqrt(ms + EPS) * nw_ref[...]).astype(BF16)


def _head_norm(a, use_norm, w):
    r = lax.rsqrt(jnp.mean(a * a, axis=-1, keepdims=True) + EPS)
    return a * jnp.where(use_norm, r, 1.0) * jnp.where(use_norm, w, 1.0)


def _inproj_kernel(x_ref, nw_ref, wqkvg_ref, wa_ref, wdt_ref, qkw_ref,
                   qkv_ref, rest_ref, dt_ref, h_scr, *, n_qk, n_qkv, n_g):
    j = pl.program_id(1)
    heads_per_tile = PROJ_TN // ATT_HEAD_DIM

    @pl.when(j == 0)
    def _():
        h = _normed_rows(x_ref, nw_ref)
        h_scr[...] = h
        dt_ref[...] = jnp.dot(h, wdt_ref[...], preferred_element_type=F32)

    @pl.when(j < n_qkv)
    def _():
        acc = jnp.dot(h_scr[...], wqkvg_ref[...], preferred_element_type=F32)
        for hh in range(heads_per_tile):
            sl = slice(hh * ATT_HEAD_DIM, (hh + 1) * ATT_HEAD_DIM)
            qkv_ref[0, hh] = _head_norm(acc[:, sl], j < n_qk, qkw_ref[:, sl]).astype(BF16)

    @pl.when(jnp.logical_and(j >= n_qkv, j < n_qkv + n_g))
    def _():
        rest_ref[...] = jnp.dot(h_scr[...], wqkvg_ref[...], preferred_element_type=F32)

    @pl.when(j >= n_qkv + n_g)
    def _():
        rest_ref[...] = jnp.dot(h_scr[...], wa_ref[...], preferred_element_type=F32)


def _in_projection(x, norm_w, w_qkvg, w_a, w_dt, qk_w, *, tm):
    t = x.shape[0]
    tn = PROJ_TN
    tiles_per_proj = D_ATT // tn
    n_qk, n_qkv, n_g = 2 * tiles_per_proj, 3 * tiles_per_proj, tiles_per_proj
    n_a = (D_SSM + CONV_DIM) // tn
    heads_per_tile = tn // ATT_HEAD_DIM
    grid = (t // tm, n_qkv + n_g + n_a)
    kern = functools.partial(_inproj_kernel, n_qk=n_qk, n_qkv=n_qkv, n_g=n_g)

    def qkv_index(i, j):
        jj = jnp.minimum(j, n_qkv - 1)
        return (jj // tiles_per_proj, jj % tiles_per_proj, i, 0)

    return pl.pallas_call(
        kern,
        grid=grid,
        in_specs=[
            pl.BlockSpec((tm, D_MODEL), lambda i, j: (i, 0)),
            pl.BlockSpec((1, D_MODEL), lambda i, j: (0, 0)),
            pl.BlockSpec((D_MODEL, tn), lambda i, j: (0, jnp.minimum(j, n_qkv + n_g - 1))),
            pl.BlockSpec((D_MODEL, tn), lambda i, j: (0, jnp.clip(j - n_qkv - n_g, 0, n_a - 1))),
            pl.BlockSpec((D_MODEL, DT_COLS), lambda i, j: (0, 0)),
            pl.BlockSpec((1, tn), lambda i, j: (0, jnp.minimum(j, n_qk - 1))),
        ],
        out_specs=[
            pl.BlockSpec((1, heads_per_tile, tm, ATT_HEAD_DIM), qkv_index),
            pl.BlockSpec((tm, tn), lambda i, j: (i, jnp.maximum(j - n_qkv, 0))),
            pl.BlockSpec((tm, DT_COLS), lambda i, j: (i, 0)),
        ],
        out_shape=[
            jax.ShapeDtypeStruct((3, ATT_HEADS, t, ATT_HEAD_DIM), BF16),
            jax.ShapeDtypeStruct((t, R_COLS), F32),
            jax.ShapeDtypeStruct((t, DT_COLS), F32),
        ],
        scratch_shapes=[pltpu.VMEM((tm, D_MODEL), BF16)],
        compiler_params=_compiler_params(("parallel", "arbitrary")),
        name="in_projection",
    )(x, norm_w, w_qkvg, w_a, w_dt, qk_w)


def _kv_rows_kernel(x_ref, nw_ref, w_ref, qkw_ref, o_ref, h_scr, *, n_k):
    j = pl.program_id(1)

    @pl.when(j == 0)
    def _():
        h_scr[...] = _normed_rows(x_ref, nw_ref)

    acc = jnp.dot(h_scr[...], w_ref[...], preferred_element_type=F32)
    for hh in range(PROJ_TN // ATT_HEAD_DIM):
        sl = slice(hh * ATT_HEAD_DIM, (hh + 1) * ATT_HEAD_DIM)
        o_ref[:, sl] = _head_norm(acc[:, sl], j < n_k, qkw_ref[:, sl])


def _kv_rows(x, norm_w, w_qkvg, qk_w, *, first_row, n_rows, tm):
    tn = PROJ_TN
    tiles_per_proj = D_ATT // tn
    row0 = first_row // tm
    kern = functools.partial(_kv_rows_kernel, n_k=tiles_per_proj)
    return pl.pallas_call(
        kern,
        grid=(n_rows // tm, 2 * tiles_per_proj),
        in_specs=[
            pl.BlockSpec((tm, D_MODEL), lambda i, j: (row0 + i, 0)),
            pl.BlockSpec((1, D_MODEL), lambda i, j: (0, 0)),
            pl.BlockSpec((D_MODEL, tn), lambda i, j: (0, tiles_per_proj + j)),
            pl.BlockSpec((1, tn), lambda i, j: (0, jnp.minimum(tiles_per_proj + j, 2 * tiles_per_proj - 1))),
        ],
        out_specs=pl.BlockSpec((tm, tn), lambda i, j: (i, j)),
        out_shape=jax.ShapeDtypeStruct((n_rows, 2 * D_ATT), F32),
        scratch_shapes=[pltpu.VMEM((tm, D_MODEL), BF16)],
        compiler_params=_compiler_params(("parallel", "arbitrary")),
        name="kv_rows",
    )(x, norm_w, w_qkvg, qk_w)


def _transpose_rows_to_lanes(v):
    q = v.shape[0]
    if q < LANES:
        v = jnp.concatenate([v, jnp.zeros((LANES - q, LANES), v.dtype)], axis=0)
    return v.T[:, 0:q]


def _split_bf16(v):
    pieces = []
    rem = v
    for _ in range(N_SPLIT):
        piece = rem.astype(BF16)
        pieces.append(piece)
        rem = rem - piece.astype(F32)
    return jnp.concatenate(pieces, axis=1)


def _ssd_kernel(x_ref, bc_ref, z_ref, dt_ref, hist_ref, s0_ref, convp_ref, hp_ref, dexp_ref, nw_ref, expand_ref,
                y_ref, sout_ref, conv_scr, act_scr, st_scr, yd_scr, exp_scr, *, q_rows, n_chunks):
    c = pl.program_id(1)
    q = q_rows

    @pl.when(c == 0)
    def _():
        conv_scr[0:SUBLANES, :] = hist_ref[0]
        for g in range(SSM_GROUPS):
            st_scr[g] = s0_ref[0, g * GROUP_COLS:(g + 1) * GROUP_COLS, :].T

    conv_scr[SUBLANES:SUBLANES + q, 0:D_SSM] = x_ref[...]
    conv_scr[SUBLANES:SUBLANES + q, D_SSM:] = bc_ref[...]

    for c0 in range(0, CONV_DIM, GROUP_COLS):
        cols = slice(c0, c0 + GROUP_COLS)
        xp = conv_scr[:, cols]
        conv = convp_ref[CONV_WIDTH:CONV_WIDTH + 1, cols] + convp_ref[CONV_WIDTH - 1:CONV_WIDTH, cols] * xp[SUBLANES:]
        for shift in range(1, CONV_WIDTH):
            tap = CONV_WIDTH - 1 - shift
            conv = conv + convp_ref[tap:tap + 1, cols] * pltpu.roll(xp, shift, axis=0)[SUBLANES:]
        act_scr[:, cols] = _silu(conv)
    conv_scr[0:SUBLANES, :] = conv_scr[q:q + SUBLANES, :]

    v = dt_ref[...] + hp_ref[0:1, :]
    dt = jnp.maximum(v, 0.0) + jnp.log1p(jnp.exp(-jnp.abs(v)))
    a = dt * (-jnp.exp(hp_ref[1:2, :]))
    ii = lax.broadcasted_iota(jnp.int32, (q, q), 0)
    jj = lax.broadcasted_iota(jnp.int32, (q, q), 1)
    causal = ii >= jj
    acum = jnp.dot(causal.astype(F32), a, precision=lax.Precision.HIGHEST, preferred_element_type=F32)
    acum2 = acum * LOG2E
    row_t = _transpose_rows_to_lanes(acum2 - jnp.log(dt) * LOG2E)
    last = acum[q - 1:q, :]
    factors = jnp.concatenate(
        [jnp.exp(acum),
         jnp.exp(last - acum) * dt,
         jnp.broadcast_to(jnp.exp(last), (SUBLANES, LANES))], axis=0)
    exp_scr[...] = jnp.dot(_split_bf16(factors), expand_ref[...], preferred_element_type=F32)

    for g in range(SSM_GROUPS):
        cols = slice(g * GROUP_COLS, (g + 1) * GROUP_COLS)
        xact = act_scr[:, cols]
        bact = act_scr[:, D_SSM + g * D_STATE:D_SSM + (g + 1) * D_STATE].astype(BF16)
        cact = act_scr[:, D_SSM + (SSM_GROUPS + g) * D_STATE:D_SSM + (SSM_GROUPS + g + 1) * D_STATE].astype(BF16)
        cb = lax.dot_general(cact, bact, (((1,), (1,)), ((), ())), preferred_element_type=F32)
        st = st_scr[g]
        y_off = jnp.dot(cact, st.astype(BF16), preferred_element_type=F32)
        for r in range(HEADS_PER_GROUP):
            h = g * HEADS_PER_GROUP + r
            m = cb * jnp.exp2(jnp.where(causal, acum2[:, h:h + 1] - row_t[h:h + 1, :], -jnp.inf))
            xh = xact[:, r * SSM_HEADDIM:(r + 1) * SSM_HEADDIM]
            yd_scr[:, h * SSM_HEADDIM:(h + 1) * SSM_HEADDIM] = jnp.dot(
                m.astype(BF16), xh.astype(BF16), preferred_element_type=F32)
        y = yd_scr[:, cols] + y_off * exp_scr[0:q, cols] + dexp_ref[:, cols] * xact
        xw = (xact * exp_scr[q:2 * q, cols]).astype(BF16)
        upd = lax.dot_general(bact, xw, (((0,), (0,)), ((), ())), preferred_element_type=F32)
        st_scr[g] = st * exp_scr[2 * q:2 * q + 1, cols] + upd

        yg = y * _silu(z_ref[:, cols])
        rn = lax.rsqrt(jnp.mean(yg * yg, axis=-1, keepdims=True) + EPS)
        y_ref[:, cols] = (yg * rn * nw_ref[:, cols]).astype(y_ref.dtype)

    @pl.when(c == n_chunks - 1)
    def _():
        for g in range(SSM_GROUPS):
            sout_ref[0, g * GROUP_COLS:(g + 1) * GROUP_COLS, :] = st_scr[g].T


def _ssd_branch(rest, dt, hist, s0, convp, hp, dexp, ssm_nw, expand, *, n_streams, q_rows, n_chunks):
    t = rest.shape[0]

    def rows(b, c):
        return b * n_chunks + c

    def const(shape):
        return pl.BlockSpec(shape, lambda b, c: (0,) * len(shape))

    kern = functools.partial(_ssd_kernel, q_rows=q_rows, n_chunks=n_chunks)
    return pl.pallas_call(
        kern,
        grid=(n_streams, n_chunks),
        in_specs=[
            pl.BlockSpec((q_rows, D_SSM), lambda b, c: (rows(b, c), R_X // D_SSM)),
            pl.BlockSpec((q_rows, BC_COLS), lambda b, c: (rows(b, c), R_BC // BC_COLS)),
            pl.BlockSpec((q_rows, D_SSM), lambda b, c: (rows(b, c), R_Z // D_SSM)),
            pl.BlockSpec((q_rows, DT_COLS), lambda b, c: (rows(b, c), 0)),
            pl.BlockSpec((1, SUBLANES, CONV_DIM), lambda b, c: (b, 0, 0)),
            pl.BlockSpec((1, D_SSM, D_STATE), lambda b, c: (b, 0, 0)),
            const((SUBLANES, CONV_DIM)),
            const((SUBLANES, LANES)),
            const((1, D_SSM)),
            const((1, D_SSM)),
            const((N_SPLIT * LANES, D_SSM)),
        ],
        out_specs=[
            pl.BlockSpec((q_rows, D_SSM), lambda b, c: (rows(b, c), 0)),
            pl.BlockSpec((1, D_SSM, D_STATE), lambda b, c: (b, 0, 0)),
        ],
        out_shape=[
            jax.ShapeDtypeStruct((t, D_SSM), BF16),
            jax.ShapeDtypeStruct((n_streams, D_SSM, D_STATE), F32),
        ],
        scratch_shapes=[
            pltpu.VMEM((SUBLANES + q_rows, CONV_DIM), F32),
            pltpu.VMEM((q_rows, CONV_DIM), F32),
            pltpu.VMEM((SSM_GROUPS, D_STATE, GROUP_COLS), F32),
            pltpu.VMEM((q_rows, D_SSM), F32),
            pltpu.VMEM((2 * q_rows + SUBLANES, D_SSM), F32),
        ],
        compiler_params=_compiler_params(("parallel", "arbitrary")),
        name="ssd_branch",
    )(rest, rest, rest, dt, hist, s0, convp, hp, dexp, ssm_nw, expand)


def _softmax_stage(s_scr, p_scr, inv_scr, bias_ref, h, q_rows):
    exp2_scale = ATT_SCALE * LOG2E
    slab = min(SOFTMAX_SLAB, q_rows)
    for r0 in range(0, q_rows, slab):
        u = s_scr[h, r0:r0 + slab, :] + bias_ref[h, r0:r0 + slab, :]
        m = jnp.max(u, axis=-1, keepdims=True)
        e = jnp.exp2((u - m) * exp2_scale)
        p_scr[h, r0:r0 + slab, :] = e.astype(BF16)
        inv = 1.0 / jnp.sum(e, axis=-1, keepdims=True)
        inv_scr[h, r0:r0 + slab, :] = jnp.broadcast_to(inv, (slab, ATT_HEAD_DIM))


def _head_pipeline(qk_stage, softmax_stage, pv_stage):
    for step in range(ATT_HEADS + 2):
        if step < ATT_HEADS:
            qk_stage(step)
        if 1 <= step <= ATT_HEADS:
            softmax_stage(step - 1)
        if step >= 2:
            pv_stage(step - 2)


def _prompt_attn_kernel(*refs, n_hist_blocks):
    n_blocks = n_hist_blocks + 1
    q_ref, g_ref = refs[0], refs[1]
    k_refs = refs[2:2 + n_blocks]
    v_refs = refs[2 + n_blocks:2 + 2 * n_blocks]
    bias_ref, o_ref, s_scr, p_scr, inv_scr = refs[2 + 2 * n_blocks:]
    t = pl.program_id(0)

    def qk_stage(h):
        qh = q_ref[0, h]
        for blk in range(n_blocks):
            s = lax.dot_general(qh, k_refs[blk][0, h], (((1,), (1,)), ((), ())), preferred_element_type=F32)
            s_scr[h, :, blk * ATT_Q_ROWS:(blk + 1) * ATT_Q_ROWS] = jnp.where(t + blk >= n_hist_blocks, s, MASK_VALUE)

    def pv_stage(h):
        sl = slice(h * ATT_HEAD_DIM, (h + 1) * ATT_HEAD_DIM)
        o = jnp.dot(p_scr[h, :, 0:ATT_Q_ROWS], v_refs[0][0, h], preferred_element_type=F32)
        for blk in range(1, n_blocks):
            o = o + jnp.dot(p_scr[h, :, blk * ATT_Q_ROWS:(blk + 1) * ATT_Q_ROWS], v_refs[blk][0, h],
                            preferred_element_type=F32)
        o_ref[:, sl] = (o * inv_scr[h] * _silu(g_ref[:, sl])).astype(o_ref.dtype)

    _head_pipeline(qk_stage,
                   functools.partial(_softmax_stage, s_scr, p_scr, inv_scr, bias_ref, q_rows=ATT_Q_ROWS), pv_stage)


def _prompt_attention(qkv, rest, bias):
    t = rest.shape[0]
    n_hist_blocks = BAND_ROWS // ATT_Q_ROWS
    n_blocks = n_hist_blocks + 1
    n_tiles = t // ATT_Q_ROWS
    span = BAND_ROWS + ATT_Q_ROWS
    head_block = (1, ATT_HEADS, ATT_Q_ROWS, ATT_HEAD_DIM)

    def kv_spec(which, blk):
        return pl.BlockSpec(head_block, lambda i: (which, 0, jnp.maximum(i - n_hist_blocks + blk, 0), 0))

    in_specs = [pl.BlockSpec(head_block, lambda i: (QKV_Q, 0, i, 0)),
                pl.BlockSpec((ATT_Q_ROWS, D_ATT), lambda i: (i, R_G // D_ATT))]
    in_specs += [kv_spec(QKV_K, blk) for blk in range(n_blocks)]
    in_specs += [kv_spec(QKV_V, blk) for blk in range(n_blocks)]
    in_specs += [pl.BlockSpec((ATT_HEADS, ATT_Q_ROWS, span), lambda i: (0, 0, 0))]
    kern = functools.partial(_prompt_attn_kernel, n_hist_blocks=n_hist_blocks)
    return pl.pallas_call(
        kern,
        grid=(n_tiles,),
        in_specs=in_specs,
        out_specs=pl.BlockSpec((ATT_Q_ROWS, D_ATT), lambda i: (i, 0)),
        out_shape=jax.ShapeDtypeStruct((t, D_ATT), BF16),
        scratch_shapes=[pltpu.VMEM((ATT_HEADS, ATT_Q_ROWS, span), F32),
                        pltpu.VMEM((ATT_HEADS, ATT_Q_ROWS, span), BF16),
                        pltpu.VMEM((ATT_HEADS, ATT_Q_ROWS, ATT_HEAD_DIM), F32)],
        compiler_params=_compiler_params(("parallel",)),
        name="prompt_attention",
    )(qkv, rest, *([qkv] * (2 * n_blocks)), bias)


def _sample_attn_kernel(q_ref, kn_ref, vn_ref, g_ref, kc_lo, kc_hi, vc_lo, vc_hi, bias_ref, o_ref,
                        s_scr, p_scr, inv_scr, *, q_rows):
    new_pad = LANES - q_rows
    zpad = jnp.zeros((new_pad, ATT_HEAD_DIM), BF16)

    def cached(lo_ref, hi_ref, h):
        ref = (lo_ref if h < SUBLANES else hi_ref).reshape(BAND_ROWS * SUBLANES, ATT_HEAD_DIM)
        return ref[pl.ds(h % SUBLANES, BAND_ROWS, stride=SUBLANES), :].astype(BF16)

    def qk_stage(h):
        qh = q_ref[0, h]
        s_scr[h, :, 0:BAND_ROWS] = lax.dot_general(qh, cached(kc_lo, kc_hi, h), (((1,), (1,)), ((), ())),
                                                   preferred_element_type=F32)
        k_new = jnp.concatenate([kn_ref[0, h], zpad], axis=0)
        s_scr[h, :, BAND_ROWS:] = lax.dot_general(qh, k_new, (((1,), (1,)), ((), ())), preferred_element_type=F32)

    def pv_stage(h):
        sl = slice(h * ATT_HEAD_DIM, (h + 1) * ATT_HEAD_DIM)
        v_new = jnp.concatenate([vn_ref[0, h], zpad], axis=0)
        o = jnp.dot(p_scr[h, :, 0:BAND_ROWS], cached(vc_lo, vc_hi, h), preferred_element_type=F32)
        o = o + jnp.dot(p_scr[h, :, BAND_ROWS:], v_new, preferred_element_type=F32)
        o_ref[:, sl] = (o * inv_scr[h] * _silu(g_ref[:, sl])).astype(o_ref.dtype)

    _head_pipeline(qk_stage, functools.partial(_softmax_stage, s_scr, p_scr, inv_scr, bias_ref, q_rows=q_rows),
                   pv_stage)


def _sample_attention(qkv, rest, cache_k, cache_v, bias, *, n_streams, q_rows):
    t = rest.shape[0]
    span_pad = BAND_ROWS + LANES
    kern = functools.partial(_sample_attn_kernel, q_rows=q_rows)

    def new_spec(which):
        return pl.BlockSpec((1, ATT_HEADS, q_rows, ATT_HEAD_DIM), lambda b: (which, 0, b, 0))

    def cache_spec(half):
        return pl.BlockSpec((None, BAND_ROWS, None, SUBLANES, ATT_HEAD_DIM), lambda b: (b, 0, half, 0, 0))

    return pl.pallas_call(
        kern,
        grid=(n_streams,),
        in_specs=[
            new_spec(QKV_Q), new_spec(QKV_K), new_spec(QKV_V),
            pl.BlockSpec((q_rows, D_ATT), lambda b: (b, R_G // D_ATT)),
            cache_spec(0), cache_spec(1), cache_spec(0), cache_spec(1),
            pl.BlockSpec((ATT_HEADS, q_rows, span_pad), lambda b: (0, 0, 0)),
        ],
        out_specs=pl.BlockSpec((q_rows, D_ATT), lambda b: (b, 0)),
        out_shape=jax.ShapeDtypeStruct((t, D_ATT), BF16),
        scratch_shapes=[pltpu.VMEM((ATT_HEADS, q_rows, span_pad), F32),
                        pltpu.VMEM((ATT_HEADS, q_rows, span_pad), BF16),
                        pltpu.VMEM((ATT_HEADS, q_rows, ATT_HEAD_DIM), F32)],
        compiler_params=_compiler_params(("parallel",)),
        name="sample_attention",
    )(qkv, qkv, qkv, rest, cache_k, cache_k, cache_v, cache_v, bias)


def _outproj_kernel(ys_ref, ya_ref, w1_ref, w2_ref, x_ref, o_ref):
    acc = jnp.dot(ys_ref[...], w1_ref[...], preferred_element_type=F32)
    acc = acc + jnp.dot(ya_ref[...], w2_ref[...], preferred_element_type=F32)
    o_ref[...] = x_ref[...] + acc


def _out_projection(y_ssm, y_att, w_out, x, *, tm, tn):
    t = x.shape[0]
    return pl.pallas_call(
        _outproj_kernel,
        grid=(t // tm, D_MODEL // tn),
        in_specs=[
            pl.BlockSpec((tm, D_SSM), lambda i, j: (i, 0)),
            pl.BlockSpec((tm, D_ATT), lambda i, j: (i, 0)),
            pl.BlockSpec((D_SSM, tn), lambda i, j: (0, j)),
            pl.BlockSpec((D_ATT, tn), lambda i, j: (D_SSM // D_ATT, j)),
            pl.BlockSpec((tm, tn), lambda i, j: (i, j)),
        ],
        out_specs=pl.BlockSpec((tm, tn), lambda i, j: (i, j)),
        out_shape=jax.ShapeDtypeStruct((t, D_MODEL), F32),
        compiler_params=_compiler_params(("parallel", "arbitrary")),
        name="out_projection",
    )(y_ssm, y_att, w_out, w_out, x)


def _pad_to(v, size, axis):
    pad = [(0, 0)] * v.ndim
    pad[axis] = (0, size - v.shape[axis])
    return jnp.pad(v, pad)


def _prepare_params(norm_w, w_in, conv_w, conv_b, dt_bias, a_log, d_skip, ssm_norm_w, q_norm_w, k_norm_w, w_out):
    w_bf = w_in.astype(BF16)
    w_qkvg = w_bf[:, OFF_Q:]
    w_a = w_bf
    w_dt = _pad_to(w_bf[:, OFF_DT:OFF_Q], DT_COLS, 1)
    qk_w = jnp.concatenate([jnp.tile(q_norm_w, ATT_HEADS), jnp.tile(k_norm_w, ATT_HEADS)]).reshape(1, 2 * D_ATT)
    convp = _pad_to(jnp.concatenate([conv_w, conv_b[None]], axis=0), SUBLANES, 0)
    hp = _pad_to(_pad_to(jnp.stack([dt_bias, a_log]), LANES, 1), SUBLANES, 0)
    dexp = jnp.repeat(d_skip, SSM_HEADDIM).reshape(1, D_SSM)
    head_of_col = np.arange(D_SSM) // SSM_HEADDIM
    expand = (np.arange(LANES)[:, None] == head_of_col[None, :]).astype(np.float32)
    expand = jnp.asarray(np.tile(expand, (N_SPLIT, 1)), dtype=BF16)
    return dict(norm_w=norm_w.reshape(1, D_MODEL), w_qkvg=w_qkvg, w_a=w_a, w_dt=w_dt, qk_w=qk_w,
                convp=convp, hp=hp, dexp=dexp, ssm_nw=ssm_norm_w.reshape(1, D_SSM), expand=expand,
                w_out=w_out.astype(BF16))


def _rel_bias_table(rel_bias, q_rows, hist_rows, span_pad, band_chunk):
    n_heads = rel_bias.shape[0]
    period = span_pad + q_rows
    k = np.arange(period)
    rel = np.clip(hist_rows + q_rows - 1 - k, -REL_CLIP, REL_CLIP) + REL_CLIP
    onehot = jnp.asarray(np.eye(N_REL, dtype=np.float32)[:, rel])
    diag_row = jnp.dot(rel_bias.astype(F32), onehot, precision=lax.Precision.HIGHEST)
    flat = jnp.tile(diag_row, (1, q_rows))[:, :q_rows * (period - 1)]
    table = flat.reshape(n_heads, q_rows, period - 1)[:, :, q_rows - 1:q_rows - 1 + span_pad]
    i_idx = np.arange(q_rows)[:, None]
    j_idx = np.arange(span_pad)[None, :]
    valid = j_idx < hist_rows + q_rows
    if band_chunk is not None:
        start = (i_idx // band_chunk) * band_chunk
        valid = valid & (j_idx >= start) & (j_idx < start + hist_rows + band_chunk)
    return jnp.where(jnp.asarray(np.broadcast_to(valid, (q_rows, span_pad)))[None], table / ATT_SCALE, MASK_VALUE)


def _layer(x, hist, s0, cache_k, cache_v, prm, rel_bias, *, n_streams, seq, prompt, proj_tm):
    t = n_streams * seq
    x2 = x.reshape(t, D_MODEL)
    qkv, rest, dt = _in_projection(x2, prm["norm_w"], prm["w_qkvg"], prm["w_a"], prm["w_dt"], prm["qk_w"],
                                   tm=proj_tm)
    if prompt:
        q_rows, n_chunks = SSD_ROWS, seq // SSD_ROWS
        kv_first, kv_rows = t - BAND_ROWS, BAND_ROWS
    else:
        q_rows, n_chunks = seq, 1
        kv_first, kv_rows = 0, t
    kv_new = _kv_rows(x2, prm["norm_w"], prm["w_qkvg"], prm["qk_w"], first_row=kv_first, n_rows=kv_rows, tm=512)
    y_ssm, s_new = _ssd_branch(rest, dt, hist, s0, prm["convp"], prm["hp"], prm["dexp"], prm["ssm_nw"],
                               prm["expand"], n_streams=n_streams, q_rows=q_rows, n_chunks=n_chunks)
    if prompt:
        bias = _rel_bias_table(rel_bias, ATT_Q_ROWS, BAND_ROWS, BAND_ROWS + ATT_Q_ROWS, CHUNK)
        y_att = _prompt_attention(qkv, rest, bias)
    else:
        bias = _rel_bias_table(rel_bias, seq, BAND_ROWS, BAND_ROWS + LANES, None)
        y_att = _sample_attention(qkv, rest, cache_k, cache_v, bias, n_streams=n_streams, q_rows=seq)
    y = _out_projection(y_ssm, y_att, prm["w_out"], x2, tm=1024, tn=1024)
    new_conv = rest.reshape(n_streams, seq, R_COLS)[:, seq - (CONV_WIDTH - 1):, R_X:]
    kv_streams = kv_rows // n_streams
    kh = kv_new[:, :D_ATT].reshape(n_streams, kv_streams, ATT_HEADS, ATT_HEAD_DIM)
    vh = kv_new[:, D_ATT:].reshape(n_streams, kv_streams, ATT_HEADS, ATT_HEAD_DIM)
    new_ssm = s_new.reshape(n_streams, SSM_HEADS, SSM_HEADDIM, D_STATE)
    return y.reshape(n_streams, seq, D_MODEL), new_conv, new_ssm, kh, vh


def kernel(x_prompt, x_sample, state_conv, state_ssm, cache_k, cache_v, norm_w, w_in, conv_w, conv_b, dt_bias, a_log, d_skip, ssm_norm_w, q_norm_w, k_norm_w, rel_bias, w_out):
    bp, lp, _ = x_prompt.shape
    bs, ls, _ = x_sample.shape
    assert bp == 1 and lp % 1024 == 0 and lp >= BAND_ROWS
    assert ls % (2 * SUBLANES) == 0 and ls <= LANES and (bs * ls) % 1024 == 0 and cache_k.shape[2] == BAND_ROWS
    assert norm_w.shape[0] == 1
    prm = _prepare_params(norm_w[0], w_in[0], conv_w[0], conv_b[0], dt_bias[0], a_log[0], d_skip[0],
                          ssm_norm_w[0], q_norm_w[0], k_norm_w[0], w_out[0])
    rb = rel_bias[0]

    zero_hist = jnp.zeros((bp, SUBLANES, CONV_DIM), F32)
    zero_state = jnp.zeros((bp, D_SSM, D_STATE), F32)
    yp, c1, s1, k1, v1 = _layer(x_prompt, zero_hist, zero_state, None, None, prm, rb,
                                n_streams=bp, seq=lp, prompt=True, proj_tm=1024)

    hist = jnp.pad(state_conv[0], ((0, 0), (SUBLANES - (CONV_WIDTH - 1), 0), (0, 0)))
    s0 = state_ssm[0].reshape(bs, D_SSM, D_STATE)
    half_heads = (bs, BAND_ROWS, ATT_HEADS // SUBLANES, SUBLANES, ATT_HEAD_DIM)
    ck = cache_k[0].reshape(half_heads)
    cv = cache_v[0].reshape(half_heads)
    ys, c2, s2, k2, v2 = _layer(x_sample, hist, s0, ck, cv, prm, rb, n_streams=bs, seq=ls, prompt=False,
                                proj_tm=512)

    return (yp, ys, c1[None], s1[None], k1[None], v1[None], c2[None], s2[None], k2[None], v2[None])
```

```python
import functools

import jax
import jax.numpy as jnp
import numpy as np
from jax import lax
from jax.experimental import pallas as pl
from jax.experimental.pallas import tpu as pltpu

F32 = jnp.float32
BF16 = jnp.bfloat16

D_MODEL = 2048
CHUNK = 64
LEFT_CHUNKS = 8
BAND_ROWS = LEFT_CHUNKS * CHUNK
D_SSM = 2048
D_ATT = 2048
SSM_HEADDIM = 64
SSM_HEADS = 32
SSM_GROUPS = 4
HEADS_PER_GROUP = SSM_HEADS // SSM_GROUPS
GROUP_COLS = HEADS_PER_GROUP * SSM_HEADDIM
D_STATE = 128
BC_COLS = 2 * SSM_GROUPS * D_STATE
CONV_WIDTH = 4
CONV_DIM = D_SSM + BC_COLS
ATT_HEAD_DIM = 128
ATT_HEADS = 16
REL_CLIP = 128
N_REL = 2 * REL_CLIP + 1
EPS = 1e-6
OFF_Z = 0
OFF_XBC = OFF_Z + D_SSM
OFF_DT = OFF_XBC + CONV_DIM
OFF_Q = OFF_DT + SSM_HEADS
OFF_K = OFF_Q + D_ATT
OFF_V = OFF_K + D_ATT
OFF_G = OFF_V + D_ATT

LANES = 128
SUBLANES = 8
VMEM_LIMIT_BYTES = 56 * 1024 * 1024

R_G = 0
R_Z = R_G + D_ATT
R_X = R_Z + D_SSM
R_BC = R_X + D_SSM
R_COLS = R_BC + BC_COLS
DT_COLS = LANES
QKV_Q, QKV_K, QKV_V = 0, 1, 2
N_SPLIT = 3

PROJ_TN = 1024
SSD_ROWS = 128
ATT_Q_ROWS = 128
SOFTMAX_SLAB = 32
MASK_VALUE = -1e30
ATT_SCALE = ATT_HEAD_DIM ** -0.5
LOG2E = 1.4426950408889634


def _silu(v):
    return v * (1.0 / (1.0 + jnp.exp(-v)))


def _compiler_params(semantics):
    return pltpu.CompilerParams(dimension_semantics=semantics, vmem_limit_bytes=VMEM_LIMIT_BYTES)


def _normed_rows(x_ref, nw_ref):
    x = x_ref[...]
    ms = jnp.mean(x * x, axis=-1, keepdims=True)
    return (x * lax.rsqrt(ms + EPS) * nw_ref[...]).astype(BF16)


def _head_norm(a, use_norm, w):
    r = lax.rsqrt(jnp.mean(a * a, axis=-1, keepdims=True) + EPS)
    return a * jnp.where(use_norm, r, 1.0) * jnp.where(use_norm, w, 1.0)


def _inproj_kernel(*refs, n_qk, n_qkv, n_g, emit_kv):
    x_ref, nw_ref, wqkvg_ref, wa_ref, wdt_ref, qkw_ref, qkv_ref, rest_ref, dt_ref = refs[:9]
    kv_ref = refs[9] if emit_kv else None
    h_scr = refs[-1]
    j = pl.program_id(1)
    heads_per_tile = PROJ_TN // ATT_HEAD_DIM

    @pl.when(j == 0)
    def _():
        h = _normed_rows(x_ref, nw_ref)
        h_scr[...] = h
        dt_ref[...] = jnp.dot(h, wdt_ref[...], preferred_element_type=F32)

    @pl.when(j < n_qkv)
    def _():
        acc = jnp.dot(h_scr[...], wqkvg_ref[...], preferred_element_type=F32)
        for hh in range(heads_per_tile):
            sl = slice(hh * ATT_HEAD_DIM, (hh + 1) * ATT_HEAD_DIM)
            normed = _head_norm(acc[:, sl], j < n_qk, qkw_ref[:, sl])
            qkv_ref[0, hh] = normed.astype(BF16)
            if emit_kv:
                kv_ref[:, sl] = normed

    @pl.when(jnp.logical_and(j >= n_qkv, j < n_qkv + n_g))
    def _():
        rest_ref[...] = jnp.dot(h_scr[...], wqkvg_ref[...], preferred_element_type=F32)

    @pl.when(j >= n_qkv + n_g)
    def _():
        rest_ref[...] = jnp.dot(h_scr[...], wa_ref[...], preferred_element_type=F32)


def _in_projection(x, norm_w, w_qkvg, w_a, w_dt, qk_w, *, tm, emit_kv):
    t = x.shape[0]
    tn = PROJ_TN
    tiles_per_proj = D_ATT // tn
    n_qk, n_qkv, n_g = 2 * tiles_per_proj, 3 * tiles_per_proj, tiles_per_proj
    n_a = (D_SSM + CONV_DIM) // tn
    heads_per_tile = tn // ATT_HEAD_DIM
    grid = (t // tm, n_qkv + n_g + n_a)
    kern = functools.partial(_inproj_kernel, n_qk=n_qk, n_qkv=n_qkv, n_g=n_g, emit_kv=emit_kv)

    def qkv_index(i, j):
        jj = jnp.minimum(j, n_qkv - 1)
        return (jj // tiles_per_proj, jj % tiles_per_proj, i, 0)

    out_specs = [
        pl.BlockSpec((1, heads_per_tile, tm, ATT_HEAD_DIM), qkv_index),
        pl.BlockSpec((tm, tn), lambda i, j: (i, jnp.maximum(j - n_qkv, 0))),
        pl.BlockSpec((tm, DT_COLS), lambda i, j: (i, 0)),
    ]
    out_shape = [
        jax.ShapeDtypeStruct((3, ATT_HEADS, t, ATT_HEAD_DIM), BF16),
        jax.ShapeDtypeStruct((t, R_COLS), F32),
        jax.ShapeDtypeStruct((t, DT_COLS), F32),
    ]
    if emit_kv:
        n_kv = n_qkv - tiles_per_proj
        out_specs.append(pl.BlockSpec((tm, tn), lambda i, j: (i, jnp.clip(j - tiles_per_proj, 0, n_kv - 1))))
        out_shape.append(jax.ShapeDtypeStruct((t, 2 * D_ATT), F32))
    return pl.pallas_call(
        kern,
        grid=grid,
        in_specs=[
            pl.BlockSpec((tm, D_MODEL), lambda i, j: (i, 0)),
            pl.BlockSpec((1, D_MODEL), lambda i, j: (0, 0)),
            pl.BlockSpec((D_MODEL, tn), lambda i, j: (0, jnp.minimum(j, n_qkv + n_g - 1))),
            pl.BlockSpec((D_MODEL, tn), lambda i, j: (0, jnp.clip(j - n_qkv - n_g, 0, n_a - 1))),
            pl.BlockSpec((D_MODEL, DT_COLS), lambda i, j: (0, 0)),
            pl.BlockSpec((1, tn), lambda i, j: (0, jnp.minimum(j, n_qk - 1))),
        ],
        out_specs=out_specs,
        out_shape=out_shape,
        scratch_shapes=[pltpu.VMEM((tm, D_MODEL), BF16)],
        compiler_params=_compiler_params(("parallel", "arbitrary")),
        name="in_projection",
    )(x, norm_w, w_qkvg, w_a, w_dt, qk_w)


def _kv_rows_kernel(x_ref, nw_ref, w_ref, qkw_ref, o_ref, h_scr, *, n_k):
    j = pl.program_id(1)

    @pl.when(j == 0)
    def _():
        h_scr[...] = _normed_rows(x_ref, nw_ref)

    acc = jnp.dot(h_scr[...], w_ref[...], preferred_element_type=F32)
    for hh in range(PROJ_TN // ATT_HEAD_DIM):
        sl = slice(hh * ATT_HEAD_DIM, (hh + 1) * ATT_HEAD_DIM)
        o_ref[:, sl] = _head_norm(acc[:, sl], j < n_k, qkw_ref[:, sl])


def _kv_rows(x, norm_w, w_qkvg, qk_w, *, first_row, n_rows, tm):
    tn = PROJ_TN
    tiles_per_proj = D_ATT // tn
    row0 = first_row // tm
    kern = functools.partial(_kv_rows_kernel, n_k=tiles_per_proj)
    return pl.pallas_call(
        kern,
        grid=(n_rows // tm, 2 * tiles_per_proj),
        in_specs=[
            pl.BlockSpec((tm, D_MODEL), lambda i, j: (row0 + i, 0)),
            pl.BlockSpec((1, D_MODEL), lambda i, j: (0, 0)),
            pl.BlockSpec((D_MODEL, tn), lambda i, j: (0, tiles_per_proj + j)),
            pl.BlockSpec((1, tn), lambda i, j: (0, jnp.minimum(tiles_per_proj + j, 2 * tiles_per_proj - 1))),
        ],
        out_specs=pl.BlockSpec((tm, tn), lambda i, j: (i, j)),
        out_shape=jax.ShapeDtypeStruct((n_rows, 2 * D_ATT), F32),
        scratch_shapes=[pltpu.VMEM((tm, D_MODEL), BF16)],
        compiler_params=_compiler_params(("parallel", "arbitrary")),
        name="kv_rows",
    )(x, norm_w, w_qkvg, qk_w)


def _transpose_rows_to_lanes(v):
    q = v.shape[0]
    if q < LANES:
        v = jnp.concatenate([v, jnp.zeros((LANES - q, LANES), v.dtype)], axis=0)
    return v.T[:, 0:q]


def _split_bf16(v):
    pieces = []
    rem = v
    for _ in range(N_SPLIT):
        piece = rem.astype(BF16)
        pieces.append(piece)
        rem = rem - piece.astype(F32)
    return jnp.concatenate(pieces, axis=1)


def _ssd_kernel(x_ref, bc_ref, z_ref, dt_ref, hist_ref, s0_ref, convp_ref, hp_ref, dexp_ref, nw_ref, expand_ref,
                y_ref, sout_ref, conv_scr, act_scr, st_scr, yd_scr, exp_scr, *, q_rows, n_chunks):
    c = pl.program_id(1)
    q = q_rows

    @pl.when(c == 0)
    def _():
        conv_scr[0:SUBLANES, :] = hist_ref[0]
        for g in range(SSM_GROUPS):
            st_scr[g] = s0_ref[0, g * GROUP_COLS:(g + 1) * GROUP_COLS, :].T

    conv_scr[SUBLANES:SUBLANES + q, 0:D_SSM] = x_ref[...]
    conv_scr[SUBLANES:SUBLANES + q, D_SSM:] = bc_ref[...]

    for c0 in range(0, CONV_DIM, GROUP_COLS):
        cols = slice(c0, c0 + GROUP_COLS)
        xp = conv_scr[:, cols]
        conv = convp_ref[CONV_WIDTH:CONV_WIDTH + 1, cols] + convp_ref[CONV_WIDTH - 1:CONV_WIDTH, cols] * xp[SUBLANES:]
        for shift in range(1, CONV_WIDTH):
            tap = CONV_WIDTH - 1 - shift
            conv = conv + convp_ref[tap:tap + 1, cols] * pltpu.roll(xp, shift, axis=0)[SUBLANES:]
        act_scr[:, cols] = _silu(conv)
    conv_scr[0:SUBLANES, :] = conv_scr[q:q + SUBLANES, :]

    v = dt_ref[...] + hp_ref[0:1, :]
    dt = jnp.maximum(v, 0.0) + jnp.log1p(jnp.exp(-jnp.abs(v)))
    a = dt * (-jnp.exp(hp_ref[1:2, :]))
    ii = lax.broadcasted_iota(jnp.int32, (q, q), 0)
    jj = lax.broadcasted_iota(jnp.int32, (q, q), 1)
    causal = ii >= jj
    acum = jnp.dot(causal.astype(F32), a, precision=lax.Precision.HIGHEST, preferred_element_type=F32)
    acum2 = acum * LOG2E
    row_t = _transpose_rows_to_lanes(acum2 - jnp.log(dt) * LOG2E)
    last = acum[q - 1:q, :]
    factors = jnp.concatenate(
        [jnp.exp(acum),
         jnp.exp(last - acum) * dt,
         jnp.broadcast_to(jnp.exp(last), (SUBLANES, LANES))], axis=0)
    exp_scr[...] = jnp.dot(_split_bf16(factors), expand_ref[...], preferred_element_type=F32)

    for g in range(SSM_GROUPS):
        cols = slice(g * GROUP_COLS, (g + 1) * GROUP_COLS)
        xact = act_scr[:, cols]
        bact = act_scr[:, D_SSM + g * D_STATE:D_SSM + (g + 1) * D_STATE].astype(BF16)
        cact = act_scr[:, D_SSM + (SSM_GROUPS + g) * D_STATE:D_SSM + (SSM_GROUPS + g + 1) * D_STATE].astype(BF16)
        cb = lax.dot_general(cact, bact, (((1,), (1,)), ((), ())), preferred_element_type=F32)
        st = st_scr[g]
        y_off = jnp.dot(cact, st.astype(BF16), preferred_element_type=F32)
        for r in range(HEADS_PER_GROUP):
            h = g * HEADS_PER_GROUP + r
            m = cb * jnp.exp2(jnp.where(causal, acum2[:, h:h + 1] - row_t[h:h + 1, :], -jnp.inf))
            xh = xact[:, r * SSM_HEADDIM:(r + 1) * SSM_HEADDIM]
            yd_scr[:, h * SSM_HEADDIM:(h + 1) * SSM_HEADDIM] = jnp.dot(
                m.astype(BF16), xh.astype(BF16), preferred_element_type=F32)
        y = yd_scr[:, cols] + y_off * exp_scr[0:q, cols] + dexp_ref[:, cols] * xact
        xw = (xact * exp_scr[q:2 * q, cols]).astype(BF16)
        upd = lax.dot_general(bact, xw, (((0,), (0,)), ((), ())), preferred_element_type=F32)
        st_scr[g] = st * exp_scr[2 * q:2 * q + 1, cols] + upd

        yg = y * _silu(z_ref[:, cols])
        rn = lax.rsqrt(jnp.mean(yg * yg, axis=-1, keepdims=True) + EPS)
        y_ref[:, cols] = (yg * rn * nw_ref[:, cols]).astype(y_ref.dtype)

    @pl.when(c == n_chunks - 1)
    def _():
        for g in range(SSM_GROUPS):
            sout_ref[0, g * GROUP_COLS:(g + 1) * GROUP_COLS, :] = st_scr[g].T


def _ssd_branch(rest, dt, hist, s0, convp, hp, dexp, ssm_nw, expand, *, n_streams, q_rows, n_chunks):
    t = rest.shape[0]

    def rows(b, c):
        return b * n_chunks + c

    def const(shape):
        return pl.BlockSpec(shape, lambda b, c: (0,) * len(shape))

    kern = functools.partial(_ssd_kernel, q_rows=q_rows, n_chunks=n_chunks)
    return pl.pallas_call(
        kern,
        grid=(n_streams, n_chunks),
        in_specs=[
            pl.BlockSpec((q_rows, D_SSM), lambda b, c: (rows(b, c), R_X // D_SSM)),
            pl.BlockSpec((q_rows, BC_COLS), lambda b, c: (rows(b, c), R_BC // BC_COLS)),
            pl.BlockSpec((q_rows, D_SSM), lambda b, c: (rows(b, c), R_Z // D_SSM)),
            pl.BlockSpec((q_rows, DT_COLS), lambda b, c: (rows(b, c), 0)),
            pl.BlockSpec((1, SUBLANES, CONV_DIM), lambda b, c: (b, 0, 0)),
            pl.BlockSpec((1, D_SSM, D_STATE), lambda b, c: (b, 0, 0)),
            const((SUBLANES, CONV_DIM)),
            const((SUBLANES, LANES)),
            const((1, D_SSM)),
            const((1, D_SSM)),
            const((N_SPLIT * LANES, D_SSM)),
        ],
        out_specs=[
            pl.BlockSpec((q_rows, D_SSM), lambda b, c: (rows(b, c), 0)),
            pl.BlockSpec((1, D_SSM, D_STATE), lambda b, c: (b, 0, 0)),
        ],
        out_shape=[
            jax.ShapeDtypeStruct((t, D_SSM), BF16),
            jax.ShapeDtypeStruct((n_streams, D_SSM, D_STATE), F32),
        ],
        scratch_shapes=[
            pltpu.VMEM((SUBLANES + q_rows, CONV_DIM), F32),
            pltpu.VMEM((q_rows, CONV_DIM), F32),
            pltpu.VMEM((SSM_GROUPS, D_STATE, GROUP_COLS), F32),
            pltpu.VMEM((q_rows, D_SSM), F32),
            pltpu.VMEM((2 * q_rows + SUBLANES, D_SSM), F32),
        ],
        compiler_params=_compiler_params(("parallel", "arbitrary")),
        name="ssd_branch",
    )(rest, rest, rest, dt, hist, s0, convp, hp, dexp, ssm_nw, expand)


def _softmax_stage(s_scr, p_scr, inv_scr, bias_ref, h, q_rows):
    exp2_scale = ATT_SCALE * LOG2E
    slab = min(SOFTMAX_SLAB, q_rows)
    bias_rows = bias_ref.shape[1]
    for r0 in range(0, q_rows, slab):
        b0 = r0 % bias_rows
        u = s_scr[h, r0:r0 + slab, :] + bias_ref[h, b0:b0 + slab, :]
        m = jnp.max(u, axis=-1, keepdims=True)
        e = jnp.exp2((u - m) * exp2_scale)
        p_scr[h, r0:r0 + slab, :] = e.astype(BF16)
        inv = 1.0 / jnp.sum(e, axis=-1, keepdims=True)
        inv_scr[h, r0:r0 + slab, :] = jnp.broadcast_to(inv, (slab, ATT_HEAD_DIM))


def _head_pipeline(qk_stage, softmax_stage, pv_stage):
    for step in range(ATT_HEADS + 2):
        if step < ATT_HEADS:
            qk_stage(step)
        if 1 <= step <= ATT_HEADS:
            softmax_stage(step - 1)
        if step >= 2:
            pv_stage(step - 2)


def _prompt_attn_kernel(*refs, n_hist_blocks):
    n_win = n_hist_blocks + 2
    tq = ATT_Q_ROWS
    q_ref, g_ref = refs[0], refs[1]
    k_refs = refs[2:2 + n_win]
    v_refs = refs[2 + n_win:2 + 2 * n_win]
    bias_ref, o_ref, s_scr, p_scr, inv_scr = refs[2 + 2 * n_win:]
    first_block = 2 * pl.program_id(0) - n_hist_blocks

    def block_rows(blk):
        lo = 0 if blk <= n_hist_blocks else tq
        hi = 2 * tq if blk >= 1 else tq
        return lo, hi

    def qk_stage(h):
        for blk in range(n_win):
            lo, hi = block_rows(blk)
            s = lax.dot_general(q_ref[0, h, lo:hi, :], k_refs[blk][0, h], (((1,), (1,)), ((), ())),
                                preferred_element_type=F32)
            if blk < n_hist_blocks:
                s = jnp.where(first_block + blk >= 0, s, MASK_VALUE)
            if lo == 0:
                s_scr[h, 0:tq, blk * tq:(blk + 1) * tq] = s[0:tq]
            if hi == 2 * tq:
                s_scr[h, tq:2 * tq, (blk - 1) * tq:blk * tq] = s[tq - lo:2 * tq - lo]

    def pv_stage(h):
        sl = slice(h * ATT_HEAD_DIM, (h + 1) * ATT_HEAD_DIM)
        o_first = jnp.zeros((tq, ATT_HEAD_DIM), F32)
        o_second = jnp.zeros((tq, ATT_HEAD_DIM), F32)
        for blk in range(n_win):
            lo, hi = block_rows(blk)
            parts = []
            if lo == 0:
                parts.append(p_scr[h, 0:tq, blk * tq:(blk + 1) * tq])
            if hi == 2 * tq:
                parts.append(p_scr[h, tq:2 * tq, (blk - 1) * tq:blk * tq])
            p = parts[0] if len(parts) == 1 else jnp.concatenate(parts, axis=0)
            o = jnp.dot(p, v_refs[blk][0, h], preferred_element_type=F32)
            if lo == 0:
                o_first = o_first + o[0:tq]
            if hi == 2 * tq:
                o_second = o_second + o[tq - lo:2 * tq - lo]
        o = jnp.concatenate([o_first, o_second], axis=0)
        o_ref[:, sl] = (o * inv_scr[h] * _silu(g_ref[:, sl])).astype(o_ref.dtype)

    _head_pipeline(qk_stage,
                   functools.partial(_softmax_stage, s_scr, p_scr, inv_scr, bias_ref, q_rows=2 * tq), pv_stage)


def _prompt_attention(qkv, rest, bias):
    t = rest.shape[0]
    tq = ATT_Q_ROWS
    n_hist_blocks = BAND_ROWS // tq
    n_win = n_hist_blocks + 2
    span = BAND_ROWS + tq
    kv_block = (1, ATT_HEADS, tq, ATT_HEAD_DIM)

    def kv_spec(which, blk):
        return pl.BlockSpec(kv_block, lambda i: (which, 0, jnp.maximum(2 * i - n_hist_blocks + blk, 0), 0))

    in_specs = [pl.BlockSpec((1, ATT_HEADS, 2 * tq, ATT_HEAD_DIM), lambda i: (QKV_Q, 0, i, 0)),
                pl.BlockSpec((2 * tq, D_ATT), lambda i: (i, R_G // D_ATT))]
    in_specs += [kv_spec(QKV_K, blk) for blk in range(n_win)]
    in_specs += [kv_spec(QKV_V, blk) for blk in range(n_win)]
    in_specs += [pl.BlockSpec((ATT_HEADS, tq, span), lambda i: (0, 0, 0))]
    kern = functools.partial(_prompt_attn_kernel, n_hist_blocks=n_hist_blocks)
    return pl.pallas_call(
        kern,
        grid=(t // (2 * tq),),
        in_specs=in_specs,
        out_specs=pl.BlockSpec((2 * tq, D_ATT), lambda i: (i, 0)),
        out_shape=jax.ShapeDtypeStruct((t, D_ATT), BF16),
        scratch_shapes=[pltpu.VMEM((ATT_HEADS, 2 * tq, span), F32),
                        pltpu.VMEM((ATT_HEADS, 2 * tq, span), BF16),
                        pltpu.VMEM((ATT_HEADS, 2 * tq, ATT_HEAD_DIM), F32)],
        compiler_params=_compiler_params(("parallel",)),
        name="prompt_attention",
    )(qkv, rest, *([qkv] * (2 * n_win)), bias)


def _sample_attn_kernel(q_ref, kn_ref, vn_ref, g_ref, kc_lo, kc_hi, vc_lo, vc_hi, bias_ref, o_ref,
                        s_scr, p_scr, inv_scr, *, q_rows):
    new_pad = LANES - q_rows
    zpad = jnp.zeros((new_pad, ATT_HEAD_DIM), BF16)

    def cached(lo_ref, hi_ref, h):
        ref = (lo_ref if h < SUBLANES else hi_ref).reshape(BAND_ROWS * SUBLANES, ATT_HEAD_DIM)
        return ref[pl.ds(h % SUBLANES, BAND_ROWS, stride=SUBLANES), :].astype(BF16)

    def qk_stage(h):
        qh = q_ref[0, h]
        s_scr[h, :, 0:BAND_ROWS] = lax.dot_general(qh, cached(kc_lo, kc_hi, h), (((1,), (1,)), ((), ())),
                                                   preferred_element_type=F32)
        k_new = jnp.concatenate([kn_ref[0, h], zpad], axis=0)
        s_scr[h, :, BAND_ROWS:] = lax.dot_general(qh, k_new, (((1,), (1,)), ((), ())), preferred_element_type=F32)

    def pv_stage(h):
        sl = slice(h * ATT_HEAD_DIM, (h + 1) * ATT_HEAD_DIM)
        v_new = jnp.concatenate([vn_ref[0, h], zpad], axis=0)
        o = jnp.dot(p_scr[h, :, 0:BAND_ROWS], cached(vc_lo, vc_hi, h), preferred_element_type=F32)
        o = o + jnp.dot(p_scr[h, :, BAND_ROWS:], v_new, preferred_element_type=F32)
        o_ref[:, sl] = (o * inv_scr[h] * _silu(g_ref[:, sl])).astype(o_ref.dtype)

    _head_pipeline(qk_stage, functools.partial(_softmax_stage, s_scr, p_scr, inv_scr, bias_ref, q_rows=q_rows),
                   pv_stage)


def _sample_attention(qkv, rest, cache_k, cache_v, bias, *, n_streams, q_rows):
    t = rest.shape[0]
    span_pad = BAND_ROWS + LANES
    kern = functools.partial(_sample_attn_kernel, q_rows=q_rows)

    def new_spec(which):
        return pl.BlockSpec((1, ATT_HEADS, q_rows, ATT_HEAD_DIM), lambda b: (which, 0, b, 0))

    def cache_spec(half):
        return pl.BlockSpec((None, BAND_ROWS, None, SUBLANES, ATT_HEAD_DIM), lambda b: (b, 0, half, 0, 0))

    return pl.pallas_call(
        kern,
        grid=(n_streams,),
        in_specs=[
            new_spec(QKV_Q), new_spec(QKV_K), new_spec(QKV_V),
            pl.BlockSpec((q_rows, D_ATT), lambda b: (b, R_G // D_ATT)),
            cache_spec(0), cache_spec(1), cache_spec(0), cache_spec(1),
            pl.BlockSpec((ATT_HEADS, q_rows, span_pad), lambda b: (0, 0, 0)),
        ],
        out_specs=pl.BlockSpec((q_rows, D_ATT), lambda b: (b, 0)),
        out_shape=jax.ShapeDtypeStruct((t, D_ATT), BF16),
        scratch_shapes=[pltpu.VMEM((ATT_HEADS, q_rows, span_pad), F32),
                        pltpu.VMEM((ATT_HEADS, q_rows, span_pad), BF16),
                        pltpu.VMEM((ATT_HEADS, q_rows, ATT_HEAD_DIM), F32)],
        compiler_params=_compiler_params(("parallel",)),
        name="sample_attention",
    )(qkv, qkv, qkv, rest, cache_k, cache_k, cache_v, cache_v, bias)


def _outproj_kernel(ys_ref, ya_ref, w1_ref, w2_ref, x_ref, o_ref):
    acc = jnp.dot(ys_ref[...], w1_ref[...], preferred_element_type=F32)
    acc = acc + jnp.dot(ya_ref[...], w2_ref[...], preferred_element_type=F32)
    o_ref[...] = x_ref[...] + acc


def _out_projection(y_ssm, y_att, w_out, x, *, tm, tn):
    t = x.shape[0]
    return pl.pallas_call(
        _outproj_kernel,
        grid=(t // tm, D_MODEL // tn),
        in_specs=[
            pl.BlockSpec((tm, D_SSM), lambda i, j: (i, 0)),
            pl.BlockSpec((tm, D_ATT), lambda i, j: (i, 0)),
            pl.BlockSpec((D_SSM, tn), lambda i, j: (0, j)),
            pl.BlockSpec((D_ATT, tn), lambda i, j: (D_SSM // D_ATT, j)),
            pl.BlockSpec((tm, tn), lambda i, j: (i, j)),
        ],
        out_specs=pl.BlockSpec((tm, tn), lambda i, j: (i, j)),
        out_shape=jax.ShapeDtypeStruct((t, D_MODEL), F32),
        compiler_params=_compiler_params(("parallel", "arbitrary")),
        name="out_projection",
    )(y_ssm, y_att, w_out, w_out, x)


def _pad_to(v, size, axis):
    pad = [(0, 0)] * v.ndim
    pad[axis] = (0, size - v.shape[axis])
    return jnp.pad(v, pad)


def _prepare_params(norm_w, w_in, conv_w, conv_b, dt_bias, a_log, d_skip, ssm_norm_w, q_norm_w, k_norm_w, w_out):
    w_bf = w_in.astype(BF16)
    w_qkvg = w_bf[:, OFF_Q:]
    w_a = w_bf
    w_dt = _pad_to(w_bf[:, OFF_DT:OFF_Q], DT_COLS, 1)
    qk_w = jnp.concatenate([jnp.tile(q_norm_w, ATT_HEADS), jnp.tile(k_norm_w, ATT_HEADS)]).reshape(1, 2 * D_ATT)
    convp = _pad_to(jnp.concatenate([conv_w, conv_b[None]], axis=0), SUBLANES, 0)
    hp = _pad_to(_pad_to(jnp.stack([dt_bias, a_log]), LANES, 1), SUBLANES, 0)
    dexp = jnp.repeat(d_skip, SSM_HEADDIM).reshape(1, D_SSM)
    head_of_col = np.arange(D_SSM) // SSM_HEADDIM
    expand = (np.arange(LANES)[:, None] == head_of_col[None, :]).astype(np.float32)
    expand = jnp.asarray(np.tile(expand, (N_SPLIT, 1)), dtype=BF16)
    return dict(norm_w=norm_w.reshape(1, D_MODEL), w_qkvg=w_qkvg, w_a=w_a, w_dt=w_dt, qk_w=qk_w,
                convp=convp, hp=hp, dexp=dexp, ssm_nw=ssm_norm_w.reshape(1, D_SSM), expand=expand,
                w_out=w_out.astype(BF16))


def _rel_bias_table(rel_bias, q_rows, hist_rows, span_pad, band_chunk):
    n_heads = rel_bias.shape[0]
    period = span_pad + q_rows
    k = np.arange(period)
    rel = np.clip(hist_rows + q_rows - 1 - k, -REL_CLIP, REL_CLIP) + REL_CLIP
    onehot = jnp.asarray(np.eye(N_REL, dtype=np.float32)[:, rel])
    diag_row = jnp.dot(rel_bias.astype(F32), onehot, precision=lax.Precision.HIGHEST)
    flat = jnp.tile(diag_row, (1, q_rows))[:, :q_rows * (period - 1)]
    table = flat.reshape(n_heads, q_rows, period - 1)[:, :, q_rows - 1:q_rows - 1 + span_pad]
    i_idx = np.arange(q_rows)[:, None]
    j_idx = np.arange(span_pad)[None, :]
    valid = j_idx < hist_rows + q_rows
    if band_chunk is not None:
        start = (i_idx // band_chunk) * band_chunk
        valid = valid & (j_idx >= start) & (j_idx < start + hist_rows + band_chunk)
    return jnp.where(jnp.asarray(np.broadcast_to(valid, (q_rows, span_pad)))[None], table / ATT_SCALE, MASK_VALUE)


def _layer(x, hist, s0, cache_k, cache_v, prm, rel_bias, *, n_streams, seq, prompt, proj_tm):
    t = n_streams * seq
    x2 = x.reshape(t, D_MODEL)
    proj = _in_projection(x2, prm["norm_w"], prm["w_qkvg"], prm["w_a"], prm["w_dt"], prm["qk_w"],
                          tm=proj_tm, emit_kv=not prompt)
    if prompt:
        qkv, rest, dt = proj
        q_rows, n_chunks = SSD_ROWS, seq // SSD_ROWS
        kv_rows = BAND_ROWS
        kv_new = _kv_rows(x2, prm["norm_w"], prm["w_qkvg"], prm["qk_w"], first_row=t - kv_rows, n_rows=kv_rows,
                          tm=512)
    else:
        qkv, rest, dt, kv_new = proj
        q_rows, n_chunks = seq, 1
        kv_rows = t
    y_ssm, s_new = _ssd_branch(rest, dt, hist, s0, prm["convp"], prm["hp"], prm["dexp"], prm["ssm_nw"],
                               prm["expand"], n_streams=n_streams, q_rows=q_rows, n_chunks=n_chunks)
    if prompt:
        bias = _rel_bias_table(rel_bias, ATT_Q_ROWS, BAND_ROWS, BAND_ROWS + ATT_Q_ROWS, CHUNK)
        y_att = _prompt_attention(qkv, rest, bias)
    else:
        bias = _rel_bias_table(rel_bias, seq, BAND_ROWS, BAND_ROWS + LANES, None)
        y_att = _sample_attention(qkv, rest, cache_k, cache_v, bias, n_streams=n_streams, q_rows=seq)
    y = _out_projection(y_ssm, y_att, prm["w_out"], x2, tm=1024, tn=1024)
    new_conv = rest.reshape(n_streams, seq, R_COLS)[:, seq - (CONV_WIDTH - 1):, R_X:]
    kv_streams = kv_rows // n_streams
    kh = kv_new[:, :D_ATT].reshape(n_streams, kv_streams, ATT_HEADS, ATT_HEAD_DIM)
    vh = kv_new[:, D_ATT:].reshape(n_streams, kv_streams, ATT_HEADS, ATT_HEAD_DIM)
    new_ssm = s_new.reshape(n_streams, SSM_HEADS, SSM_HEADDIM, D_STATE)
    return y.reshape(n_streams, seq, D_MODEL), new_conv, new_ssm, kh, vh


def kernel(x_prompt, x_sample, state_conv, state_ssm, cache_k, cache_v, norm_w, w_in, conv_w, conv_b, dt_bias, a_log, d_skip, ssm_norm_w, q_norm_w, k_norm_w, rel_bias, w_out):
    bp, lp, _ = x_prompt.shape
    bs, ls, _ = x_sample.shape
    assert bp == 1 and lp % 1024 == 0 and lp >= BAND_ROWS
    assert ls % (2 * SUBLANES) == 0 and ls <= LANES and (bs * ls) % 1024 == 0 and cache_k.shape[2] == BAND_ROWS
    assert norm_w.shape[0] == 1
    prm = _prepare_params(norm_w[0], w_in[0], conv_w[0], conv_b[0], dt_bias[0], a_log[0], d_skip[0],
                          ssm_norm_w[0], q_norm_w[0], k_norm_w[0], w_out[0])
    rb = rel_bias[0]

    zero_hist = jnp.zeros((bp, SUBLANES, CONV_DIM), F32)
    zero_state = jnp.zeros((bp, D_SSM, D_STATE), F32)
    yp, c1, s1, k1, v1 = _layer(x_prompt, zero_hist, zero_state, None, None, prm, rb,
                                n_streams=bp, seq=lp, prompt=True, proj_tm=1024)

    hist = jnp.pad(state_conv[0], ((0, 0), (SUBLANES - (CONV_WIDTH - 1), 0), (0, 0)))
    s0 = state_ssm[0].reshape(bs, D_SSM, D_STATE)
    half_heads = (bs, BAND_ROWS, ATT_HEADS // SUBLANES, SUBLANES, ATT_HEAD_DIM)
    ck = cache_k[0].reshape(half_heads)
    cv = cache_v[0].reshape(half_heads)
    ys, c2, s2, k2, v2 = _layer(x_sample, hist, s0, ck, cv, prm, rb, n_streams=bs, seq=ls, prompt=False,
                                proj_tm=512)

    return (yp, ys, c1[None], s1[None], k1[None], v1[None], c2[None], s2[None], k2[None], v2[None])
```

```python
import functools

import jax
import jax.numpy as jnp
import numpy as np
from jax import lax
from jax.experimental import pallas as pl
from jax.experimental.pallas import tpu as pltpu

F32 = jnp.float32
BF16 = jnp.bfloat16

D_MODEL = 2048
CHUNK = 64
LEFT_CHUNKS = 8
BAND_ROWS = LEFT_CHUNKS * CHUNK
D_SSM = 2048
D_ATT = 2048
SSM_HEADDIM = 64
SSM_HEADS = 32
SSM_GROUPS = 4
HEADS_PER_GROUP = SSM_HEADS // SSM_GROUPS
GROUP_COLS = HEADS_PER_GROUP * SSM_HEADDIM
D_STATE = 128
BC_COLS = 2 * SSM_GROUPS * D_STATE
CONV_WIDTH = 4
CONV_DIM = D_SSM + BC_COLS
ATT_HEAD_DIM = 128
ATT_HEADS = 16
REL_CLIP = 128
N_REL = 2 * REL_CLIP + 1
EPS = 1e-6
OFF_Z = 0
OFF_XBC = OFF_Z + D_SSM
OFF_DT = OFF_XBC + CONV_DIM
OFF_Q = OFF_DT + SSM_HEADS
OFF_K = OFF_Q + D_ATT
OFF_V = OFF_K + D_ATT
OFF_G = OFF_V + D_ATT

LANES = 128
SUBLANES = 8
VMEM_LIMIT_BYTES = 56 * 1024 * 1024

R_G = 0
R_Z = R_G + D_ATT
R_X = R_Z + D_SSM
R_BC = R_X + D_SSM
R_COLS = R_BC + BC_COLS
DT_COLS = LANES
QKV_K, QKV_V, QKV_Q = 0, 1, 2
N_SPLIT = 3

PROJ_TN = 1024
SSD_ROWS = 128
ATT_Q_ROWS = 128
SOFTMAX_SLAB = 32
MASK_VALUE = -1e30
ATT_SCALE = ATT_HEAD_DIM ** -0.5
LOG2E = 1.4426950408889634


def _silu(v):
    return v * (1.0 / (1.0 + jnp.exp(-v)))


def _compiler_params(semantics):
    return pltpu.CompilerParams(dimension_semantics=semantics, vmem_limit_bytes=VMEM_LIMIT_BYTES)


def _normed_rows(x_ref, nw_ref):
    x = x_ref[...]
    ms = jnp.mean(x * x, axis=-1, keepdims=True)
    return (x * lax.rsqrt(ms + EPS) * nw_ref[...]).astype(BF16)


def _head_norm(a, use_norm, w):
    r = lax.rsqrt(jnp.mean(a * a, axis=-1, keepdims=True) + EPS)
    return a * jnp.where(use_norm, r, 1.0) * jnp.where(use_norm, w, 1.0)


def _inproj_kernel(*refs, n_qk, n_qkv, n_g, emit_kv):
    x_ref, nw_ref, wqkvg_ref, wa_ref, wdt_ref, qkw_ref, qkv_ref, rest_ref, dt_ref = refs[:9]
    kv_ref = refs[9] if emit_kv else None
    h_scr = refs[-1]
    j = pl.program_id(1)
    heads_per_tile = PROJ_TN // ATT_HEAD_DIM

    @pl.when(j == 0)
    def _():
        h = _normed_rows(x_ref, nw_ref)
        h_scr[...] = h
        dt_ref[...] = jnp.dot(h, wdt_ref[...], preferred_element_type=F32)

    @pl.when(j < n_qkv)
    def _():
        acc = jnp.dot(h_scr[...], wqkvg_ref[...], preferred_element_type=F32)
        for hh in range(heads_per_tile):
            sl = slice(hh * ATT_HEAD_DIM, (hh + 1) * ATT_HEAD_DIM)
            normed = _head_norm(acc[:, sl], j < n_qk, qkw_ref[:, sl])
            qkv_ref[0, hh] = normed.astype(BF16)
            if emit_kv:
                kv_ref[:, sl] = normed

    @pl.when(jnp.logical_and(j >= n_qkv, j < n_qkv + n_g))
    def _():
        rest_ref[...] = jnp.dot(h_scr[...], wqkvg_ref[...], preferred_element_type=F32)

    @pl.when(j >= n_qkv + n_g)
    def _():
        rest_ref[...] = jnp.dot(h_scr[...], wa_ref[...], preferred_element_type=F32)


def _in_projection(x, norm_w, w_qkvg, w_a, w_dt, qk_w, *, tm, emit_kv):
    t = x.shape[0]
    tn = PROJ_TN
    tiles_per_proj = D_ATT // tn
    n_qk, n_qkv, n_g = 2 * tiles_per_proj, 3 * tiles_per_proj, tiles_per_proj
    n_a = (D_SSM + CONV_DIM) // tn
    heads_per_tile = tn // ATT_HEAD_DIM
    grid = (t // tm, n_qkv + n_g + n_a)
    kern = functools.partial(_inproj_kernel, n_qk=n_qk, n_qkv=n_qkv, n_g=n_g, emit_kv=emit_kv)

    def qkv_index(i, j):
        jj = jnp.minimum(j, n_qkv - 1)
        plane = (jj // tiles_per_proj + QKV_Q) % 3
        return (plane, jj % tiles_per_proj, i, 0)

    out_specs = [
        pl.BlockSpec((1, heads_per_tile, tm, ATT_HEAD_DIM), qkv_index),
        pl.BlockSpec((tm, tn), lambda i, j: (i, jnp.maximum(j - n_qkv, 0))),
        pl.BlockSpec((tm, DT_COLS), lambda i, j: (i, 0)),
    ]
    out_shape = [
        jax.ShapeDtypeStruct((3, ATT_HEADS, t, ATT_HEAD_DIM), BF16),
        jax.ShapeDtypeStruct((t, R_COLS), F32),
        jax.ShapeDtypeStruct((t, DT_COLS), F32),
    ]
    if emit_kv:
        n_kv = n_qkv - tiles_per_proj
        out_specs.append(pl.BlockSpec((tm, tn), lambda i, j: (i, jnp.clip(j - tiles_per_proj, 0, n_kv - 1))))
        out_shape.append(jax.ShapeDtypeStruct((t, 2 * D_ATT), F32))
    return pl.pallas_call(
        kern,
        grid=grid,
        in_specs=[
            pl.BlockSpec((tm, D_MODEL), lambda i, j: (i, 0)),
            pl.BlockSpec((1, D_MODEL), lambda i, j: (0, 0)),
            pl.BlockSpec((D_MODEL, tn), lambda i, j: (0, jnp.minimum(j, n_qkv + n_g - 1))),
            pl.BlockSpec((D_MODEL, tn), lambda i, j: (0, jnp.clip(j - n_qkv - n_g, 0, n_a - 1))),
            pl.BlockSpec((D_MODEL, DT_COLS), lambda i, j: (0, 0)),
            pl.BlockSpec((1, tn), lambda i, j: (0, jnp.minimum(j, n_qk - 1))),
        ],
        out_specs=out_specs,
        out_shape=out_shape,
        scratch_shapes=[pltpu.VMEM((tm, D_MODEL), BF16)],
        compiler_params=_compiler_params(("parallel", "arbitrary")),
        name="in_projection",
    )(x, norm_w, w_qkvg, w_a, w_dt, qk_w)


def _kv_rows_kernel(x_ref, nw_ref, w_ref, qkw_ref, o_ref, h_scr, *, n_k):
    j = pl.program_id(1)

    @pl.when(j == 0)
    def _():
        h_scr[...] = _normed_rows(x_ref, nw_ref)

    acc = jnp.dot(h_scr[...], w_ref[...], preferred_element_type=F32)
    for hh in range(PROJ_TN // ATT_HEAD_DIM):
        sl = slice(hh * ATT_HEAD_DIM, (hh + 1) * ATT_HEAD_DIM)
        o_ref[:, sl] = _head_norm(acc[:, sl], j < n_k, qkw_ref[:, sl])


def _kv_rows(x, norm_w, w_qkvg, qk_w, *, first_row, n_rows, tm):
    tn = PROJ_TN
    tiles_per_proj = D_ATT // tn
    row0 = first_row // tm
    kern = functools.partial(_kv_rows_kernel, n_k=tiles_per_proj)
    return pl.pallas_call(
        kern,
        grid=(n_rows // tm, 2 * tiles_per_proj),
        in_specs=[
            pl.BlockSpec((tm, D_MODEL), lambda i, j: (row0 + i, 0)),
            pl.BlockSpec((1, D_MODEL), lambda i, j: (0, 0)),
            pl.BlockSpec((D_MODEL, tn), lambda i, j: (0, tiles_per_proj + j)),
            pl.BlockSpec((1, tn), lambda i, j: (0, jnp.minimum(tiles_per_proj + j, 2 * tiles_per_proj - 1))),
        ],
        out_specs=pl.BlockSpec((tm, tn), lambda i, j: (i, j)),
        out_shape=jax.ShapeDtypeStruct((n_rows, 2 * D_ATT), F32),
        scratch_shapes=[pltpu.VMEM((tm, D_MODEL), BF16)],
        compiler_params=_compiler_params(("parallel", "arbitrary")),
        name="kv_rows",
    )(x, norm_w, w_qkvg, qk_w)


def _transpose_rows_to_lanes(v):
    q = v.shape[0]
    if q < LANES:
        v = jnp.concatenate([v, jnp.zeros((LANES - q, LANES), v.dtype)], axis=0)
    return v.T[:, 0:q]


def _split_bf16(v):
    pieces = []
    rem = v
    for _ in range(N_SPLIT):
        piece = rem.astype(BF16)
        pieces.append(piece)
        rem = rem - piece.astype(F32)
    return jnp.concatenate(pieces, axis=1)


def _ssd_kernel(x_ref, bc_ref, z_ref, dt_ref, hist_ref, s0_ref, convp_ref, hp_ref, dexp_ref, nw_ref, expand_ref,
                y_ref, sout_ref, conv_scr, act_scr, st_scr, yd_scr, exp_scr, *, q_rows, n_chunks):
    c = pl.program_id(1)
    q = q_rows

    @pl.when(c == 0)
    def _():
        conv_scr[0:SUBLANES, :] = hist_ref[0]
        for g in range(SSM_GROUPS):
            st_scr[g] = s0_ref[0, g * GROUP_COLS:(g + 1) * GROUP_COLS, :].T

    conv_scr[SUBLANES:SUBLANES + q, 0:D_SSM] = x_ref[...]
    conv_scr[SUBLANES:SUBLANES + q, D_SSM:] = bc_ref[...]

    for c0 in range(0, CONV_DIM, GROUP_COLS):
        cols = slice(c0, c0 + GROUP_COLS)
        xp = conv_scr[:, cols]
        conv = convp_ref[CONV_WIDTH:CONV_WIDTH + 1, cols] + convp_ref[CONV_WIDTH - 1:CONV_WIDTH, cols] * xp[SUBLANES:]
        for shift in range(1, CONV_WIDTH):
            tap = CONV_WIDTH - 1 - shift
            conv = conv + convp_ref[tap:tap + 1, cols] * pltpu.roll(xp, shift, axis=0)[SUBLANES:]
        act_scr[:, cols] = _silu(conv)
    conv_scr[0:SUBLANES, :] = conv_scr[q:q + SUBLANES, :]

    v = dt_ref[...] + hp_ref[0:1, :]
    dt = jnp.maximum(v, 0.0) + jnp.log1p(jnp.exp(-jnp.abs(v)))
    a = dt * (-jnp.exp(hp_ref[1:2, :]))
    ii = lax.broadcasted_iota(jnp.int32, (q, q), 0)
    jj = lax.broadcasted_iota(jnp.int32, (q, q), 1)
    causal = ii >= jj
    acum = jnp.dot(causal.astype(F32), a, precision=lax.Precision.HIGHEST, preferred_element_type=F32)
    acum2 = acum * LOG2E
    row_t = _transpose_rows_to_lanes(acum2 - jnp.log(dt) * LOG2E)
    last = acum[q - 1:q, :]
    factors = jnp.concatenate(
        [jnp.exp(acum),
         jnp.exp(last - acum) * dt,
         jnp.broadcast_to(jnp.exp(last), (SUBLANES, LANES))], axis=0)
    exp_scr[...] = jnp.dot(_split_bf16(factors), expand_ref[...], preferred_element_type=F32)

    for g in range(SSM_GROUPS):
        cols = slice(g * GROUP_COLS, (g + 1) * GROUP_COLS)
        xact = act_scr[:, cols]
        bact = act_scr[:, D_SSM + g * D_STATE:D_SSM + (g + 1) * D_STATE].astype(BF16)
        cact = act_scr[:, D_SSM + (SSM_GROUPS + g) * D_STATE:D_SSM + (SSM_GROUPS + g + 1) * D_STATE].astype(BF16)
        cb = lax.dot_general(cact, bact, (((1,), (1,)), ((), ())), preferred_element_type=F32)
        st = st_scr[g]
        y_off = jnp.dot(cact, st.astype(BF16), preferred_element_type=F32)
        for r in range(HEADS_PER_GROUP):
            h = g * HEADS_PER_GROUP + r
            m = cb * jnp.exp2(jnp.where(causal, acum2[:, h:h + 1] - row_t[h:h + 1, :], -jnp.inf))
            xh = xact[:, r * SSM_HEADDIM:(r + 1) * SSM_HEADDIM]
            yd_scr[:, h * SSM_HEADDIM:(h + 1) * SSM_HEADDIM] = jnp.dot(
                m.astype(BF16), xh.astype(BF16), preferred_element_type=F32)
        y = yd_scr[:, cols] + y_off * exp_scr[0:q, cols] + dexp_ref[:, cols] * xact
        xw = (xact * exp_scr[q:2 * q, cols]).astype(BF16)
        upd = lax.dot_general(bact, xw, (((0,), (0,)), ((), ())), preferred_element_type=F32)
        st_scr[g] = st * exp_scr[2 * q:2 * q + 1, cols] + upd

        yg = y * _silu(z_ref[:, cols])
        rn = lax.rsqrt(jnp.mean(yg * yg, axis=-1, keepdims=True) + EPS)
        y_ref[:, cols] = (yg * rn * nw_ref[:, cols]).astype(y_ref.dtype)

    @pl.when(c == n_chunks - 1)
    def _():
        for g in range(SSM_GROUPS):
            sout_ref[0, g * GROUP_COLS:(g + 1) * GROUP_COLS, :] = st_scr[g].T


def _ssd_branch(rest, dt, hist, s0, convp, hp, dexp, ssm_nw, expand, *, n_streams, q_rows, n_chunks):
    t = rest.shape[0]

    def rows(b, c):
        return b * n_chunks + c

    def const(shape):
        return pl.BlockSpec(shape, lambda b, c: (0,) * len(shape))

    kern = functools.partial(_ssd_kernel, q_rows=q_rows, n_chunks=n_chunks)
    return pl.pallas_call(
        kern,
        grid=(n_streams, n_chunks),
        in_specs=[
            pl.BlockSpec((q_rows, D_SSM), lambda b, c: (rows(b, c), R_X // D_SSM)),
            pl.BlockSpec((q_rows, BC_COLS), lambda b, c: (rows(b, c), R_BC // BC_COLS)),
            pl.BlockSpec((q_rows, D_SSM), lambda b, c: (rows(b, c), R_Z // D_SSM)),
            pl.BlockSpec((q_rows, DT_COLS), lambda b, c: (rows(b, c), 0)),
            pl.BlockSpec((1, SUBLANES, CONV_DIM), lambda b, c: (b, 0, 0)),
            pl.BlockSpec((1, D_SSM, D_STATE), lambda b, c: (b, 0, 0)),
            const((SUBLANES, CONV_DIM)),
            const((SUBLANES, LANES)),
            const((1, D_SSM)),
            const((1, D_SSM)),
            const((N_SPLIT * LANES, D_SSM)),
        ],
        out_specs=[
            pl.BlockSpec((q_rows, D_SSM), lambda b, c: (rows(b, c), 0)),
            pl.BlockSpec((1, D_SSM, D_STATE), lambda b, c: (b, 0, 0)),
        ],
        out_shape=[
            jax.ShapeDtypeStruct((t, D_SSM), BF16),
            jax.ShapeDtypeStruct((n_streams, D_SSM, D_STATE), F32),
        ],
        scratch_shapes=[
            pltpu.VMEM((SUBLANES + q_rows, CONV_DIM), F32),
            pltpu.VMEM((q_rows, CONV_DIM), F32),
            pltpu.VMEM((SSM_GROUPS, D_STATE, GROUP_COLS), F32),
            pltpu.VMEM((q_rows, D_SSM), F32),
            pltpu.VMEM((2 * q_rows + SUBLANES, D_SSM), F32),
        ],
        compiler_params=_compiler_params(("parallel", "arbitrary")),
        name="ssd_branch",
    )(rest, rest, rest, dt, hist, s0, convp, hp, dexp, ssm_nw, expand)


def _softmax_stage(s_scr, p_scr, inv_scr, bias_ref, h, q_rows):
    exp2_scale = ATT_SCALE * LOG2E
    slab = min(SOFTMAX_SLAB, q_rows)
    bias_rows = bias_ref.shape[1]
    for r0 in range(0, q_rows, slab):
        b0 = r0 % bias_rows
        u = s_scr[h, r0:r0 + slab, :] + bias_ref[h, b0:b0 + slab, :]
        m = jnp.max(u, axis=-1, keepdims=True)
        e = jnp.exp2((u - m) * exp2_scale)
        p_scr[h, r0:r0 + slab, :] = e.astype(BF16)
        inv = 1.0 / jnp.sum(e, axis=-1, keepdims=True)
        inv_scr[h, r0:r0 + slab, :] = jnp.broadcast_to(inv, (slab, ATT_HEAD_DIM))


def _head_pipeline(qk_stage, softmax_stage, pv_stage):
    for step in range(ATT_HEADS + 2):
        if step < ATT_HEADS:
            qk_stage(step)
        if 1 <= step <= ATT_HEADS:
            softmax_stage(step - 1)
        if step >= 2:
            pv_stage(step - 2)


def _prompt_attn_kernel(*refs, n_hist_blocks):
    n_win = n_hist_blocks + 2
    tq = ATT_Q_ROWS
    q_ref, g_ref = refs[0], refs[1]
    kv_refs = refs[2:2 + n_win]
    bias_ref, o_ref, s_scr, p_scr, inv_scr = refs[2 + n_win:]
    first_block = 2 * pl.program_id(0) - n_hist_blocks

    def block_rows(blk):
        lo = 0 if blk <= n_hist_blocks else tq
        hi = 2 * tq if blk >= 1 else tq
        return lo, hi

    def qk_stage(h):
        for blk in range(n_win):
            lo, hi = block_rows(blk)
            s = lax.dot_general(q_ref[0, h, lo:hi, :], kv_refs[blk][QKV_K, h], (((1,), (1,)), ((), ())),
                                preferred_element_type=F32)
            if blk < n_hist_blocks:
                s = jnp.where(first_block + blk >= 0, s, MASK_VALUE)
            if lo == 0:
                s_scr[h, 0:tq, blk * tq:(blk + 1) * tq] = s[0:tq]
            if hi == 2 * tq:
                s_scr[h, tq:2 * tq, (blk - 1) * tq:blk * tq] = s[tq - lo:2 * tq - lo]

    def pv_stage(h):
        sl = slice(h * ATT_HEAD_DIM, (h + 1) * ATT_HEAD_DIM)
        o_first = jnp.zeros((tq, ATT_HEAD_DIM), F32)
        o_second = jnp.zeros((tq, ATT_HEAD_DIM), F32)
        for blk in range(n_win):
            lo, hi = block_rows(blk)
            parts = []
            if lo == 0:
                parts.append(p_scr[h, 0:tq, blk * tq:(blk + 1) * tq])
            if hi == 2 * tq:
                parts.append(p_scr[h, tq:2 * tq, (blk - 1) * tq:blk * tq])
            p = parts[0] if len(parts) == 1 else jnp.concatenate(parts, axis=0)
            o = jnp.dot(p, kv_refs[blk][QKV_V, h], preferred_element_type=F32)
            if lo == 0:
                o_first = o_first + o[0:tq]
            if hi == 2 * tq:
                o_second = o_second + o[tq - lo:2 * tq - lo]
        o = jnp.concatenate([o_first, o_second], axis=0)
        o_ref[:, sl] = (o * inv_scr[h] * _silu(g_ref[:, sl])).astype(o_ref.dtype)

    _head_pipeline(qk_stage,
                   functools.partial(_softmax_stage, s_scr, p_scr, inv_scr, bias_ref, q_rows=2 * tq), pv_stage)


def _prompt_attention(qkv, rest, bias):
    t = rest.shape[0]
    tq = ATT_Q_ROWS
    n_hist_blocks = BAND_ROWS // tq
    n_win = n_hist_blocks + 2
    span = BAND_ROWS + tq
    assert (QKV_K, QKV_V) == (0, 1)
    kv_block = (2, ATT_HEADS, tq, ATT_HEAD_DIM)

    def kv_spec(blk):
        return pl.BlockSpec(kv_block, lambda i: (0, 0, jnp.maximum(2 * i - n_hist_blocks + blk, 0), 0))

    in_specs = [pl.BlockSpec((1, ATT_HEADS, 2 * tq, ATT_HEAD_DIM), lambda i: (QKV_Q, 0, i, 0)),
                pl.BlockSpec((2 * tq, D_ATT), lambda i: (i, R_G // D_ATT))]
    in_specs += [kv_spec(blk) for blk in range(n_win)]
    in_specs += [pl.BlockSpec((ATT_HEADS, tq, span), lambda i: (0, 0, 0))]
    kern = functools.partial(_prompt_attn_kernel, n_hist_blocks=n_hist_blocks)
    return pl.pallas_call(
        kern,
        grid=(t // (2 * tq),),
        in_specs=in_specs,
        out_specs=pl.BlockSpec((2 * tq, D_ATT), lambda i: (i, 0)),
        out_shape=jax.ShapeDtypeStruct((t, D_ATT), BF16),
        scratch_shapes=[pltpu.VMEM((ATT_HEADS, 2 * tq, span), F32),
                        pltpu.VMEM((ATT_HEADS, 2 * tq, span), BF16),
                        pltpu.VMEM((ATT_HEADS, 2 * tq, ATT_HEAD_DIM), F32)],
        compiler_params=_compiler_params(("parallel",)),
        name="prompt_attention",
    )(qkv, rest, *([qkv] * n_win), bias)


def _sample_attn_kernel(q_ref, kn_ref, vn_ref, g_ref, kc_lo, kc_hi, vc_lo, vc_hi, bias_ref, o_ref,
                        s_scr, p_scr, inv_scr, *, q_rows):
    new_pad = LANES - q_rows
    zpad = jnp.zeros((new_pad, ATT_HEAD_DIM), BF16)

    def cached(lo_ref, hi_ref, h):
        ref = (lo_ref if h < SUBLANES else hi_ref).reshape(BAND_ROWS * SUBLANES, ATT_HEAD_DIM)
        return ref[pl.ds(h % SUBLANES, BAND_ROWS, stride=SUBLANES), :].astype(BF16)

    def qk_stage(h):
        qh = q_ref[0, h]
        s_scr[h, :, 0:BAND_ROWS] = lax.dot_general(qh, cached(kc_lo, kc_hi, h), (((1,), (1,)), ((), ())),
                                                   preferred_element_type=F32)
        k_new = jnp.concatenate([kn_ref[0, h], zpad], axis=0)
        s_scr[h, :, BAND_ROWS:] = lax.dot_general(qh, k_new, (((1,), (1,)), ((), ())), preferred_element_type=F32)

    def pv_stage(h):
        sl = slice(h * ATT_HEAD_DIM, (h + 1) * ATT_HEAD_DIM)
        v_new = jnp.concatenate([vn_ref[0, h], zpad], axis=0)
        o = jnp.dot(p_scr[h, :, 0:BAND_ROWS], cached(vc_lo, vc_hi, h), preferred_element_type=F32)
        o = o + jnp.dot(p_scr[h, :, BAND_ROWS:], v_new, preferred_element_type=F32)
        o_ref[:, sl] = (o * inv_scr[h] * _silu(g_ref[:, sl])).astype(o_ref.dtype)

    _head_pipeline(qk_stage, functools.partial(_softmax_stage, s_scr, p_scr, inv_scr, bias_ref, q_rows=q_rows),
                   pv_stage)


def _sample_attention(qkv, rest, cache_k, cache_v, bias, *, n_streams, q_rows):
    t = rest.shape[0]
    span_pad = BAND_ROWS + LANES
    kern = functools.partial(_sample_attn_kernel, q_rows=q_rows)

    def new_spec(which):
        return pl.BlockSpec((1, ATT_HEADS, q_rows, ATT_HEAD_DIM), lambda b: (which, 0, b, 0))

    def cache_spec(half):
        return pl.BlockSpec((None, BAND_ROWS, None, SUBLANES, ATT_HEAD_DIM), lambda b: (b, 0, half, 0, 0))

    return pl.pallas_call(
        kern,
        grid=(n_streams,),
        in_specs=[
            new_spec(QKV_Q), new_spec(QKV_K), new_spec(QKV_V),
            pl.BlockSpec((q_rows, D_ATT), lambda b: (b, R_G // D_ATT)),
            cache_spec(0), cache_spec(1), cache_spec(0), cache_spec(1),
            pl.BlockSpec((ATT_HEADS, q_rows, span_pad), lambda b: (0, 0, 0)),
        ],
        out_specs=pl.BlockSpec((q_rows, D_ATT), lambda b: (b, 0)),
        out_shape=jax.ShapeDtypeStruct((t, D_ATT), BF16),
        scratch_shapes=[pltpu.VMEM((ATT_HEADS, q_rows, span_pad), F32),
                        pltpu.VMEM((ATT_HEADS, q_rows, span_pad), BF16),
                        pltpu.VMEM((ATT_HEADS, q_rows, ATT_HEAD_DIM), F32)],
        compiler_params=_compiler_params(("parallel",)),
        name="sample_attention",
    )(qkv, qkv, qkv, rest, cache_k, cache_k, cache_v, cache_v, bias)


def _outproj_kernel(ys_ref, ya_ref, w1_ref, w2_ref, x_ref, o_ref):
    acc = jnp.dot(ys_ref[...], w1_ref[...], preferred_element_type=F32)
    acc = acc + jnp.dot(ya_ref[...], w2_ref[...], preferred_element_type=F32)
    o_ref[...] = x_ref[...] + acc


def _out_projection(y_ssm, y_att, w_out, x, *, tm, tn):
    t = x.shape[0]
    return pl.pallas_call(
        _outproj_kernel,
        grid=(t // tm, D_MODEL // tn),
        in_specs=[
            pl.BlockSpec((tm, D_SSM), lambda i, j: (i, 0)),
            pl.BlockSpec((tm, D_ATT), lambda i, j: (i, 0)),
            pl.BlockSpec((D_SSM, tn), lambda i, j: (0, j)),
            pl.BlockSpec((D_ATT, tn), lambda i, j: (D_SSM // D_ATT, j)),
            pl.BlockSpec((tm, tn), lambda i, j: (i, j)),
        ],
        out_specs=pl.BlockSpec((tm, tn), lambda i, j: (i, j)),
        out_shape=jax.ShapeDtypeStruct((t, D_MODEL), F32),
        compiler_params=_compiler_params(("parallel", "arbitrary")),
        name="out_projection",
    )(y_ssm, y_att, w_out, w_out, x)


def _pad_to(v, size, axis):
    pad = [(0, 0)] * v.ndim
    pad[axis] = (0, size - v.shape[axis])
    return jnp.pad(v, pad)


def _cast_w_in_kernel(main_ref, next_ref, wa_ref, wq_ref, *, n_a, lane_shift):
    j = pl.program_id(0)
    tn = main_ref.shape[1]
    row_chunk = 256

    @pl.when(j < n_a)
    def _():
        wa_ref[...] = main_ref[...].astype(BF16)

    @pl.when(j >= n_a)
    def _():
        for r0 in range(0, main_ref.shape[0], row_chunk):
            rows = slice(r0, r0 + row_chunk)
            both = jnp.concatenate([main_ref[rows, :], next_ref[rows, :]], axis=1)
            wq_ref[rows, :] = pltpu.roll(both, both.shape[1] - lane_shift, axis=1)[:, 0:tn].astype(BF16)


def _cast_w_in(w_in):
    tn = PROJ_TN
    n_a = OFF_DT // tn
    n_q = (4 * D_ATT) // tn
    lane_shift = OFF_Q - OFF_DT
    assert OFF_DT % tn == 0 and 0 < lane_shift < LANES
    kern = functools.partial(_cast_w_in_kernel, n_a=n_a, lane_shift=lane_shift)
    return pl.pallas_call(
        kern,
        grid=(n_a + n_q,),
        in_specs=[
            pl.BlockSpec((D_MODEL, tn), lambda j: (0, j)),
            pl.BlockSpec((D_MODEL, LANES), lambda j: (0, (j + 1) * (tn // LANES))),
        ],
        out_specs=[
            pl.BlockSpec((D_MODEL, tn), lambda j: (0, jnp.minimum(j, n_a - 1))),
            pl.BlockSpec((D_MODEL, tn), lambda j: (0, jnp.maximum(j - n_a, 0))),
        ],
        out_shape=[
            jax.ShapeDtypeStruct((D_MODEL, n_a * tn), BF16),
            jax.ShapeDtypeStruct((D_MODEL, n_q * tn), BF16),
        ],
        compiler_params=_compiler_params(("arbitrary",)),
        name="cast_w_in",
    )(w_in, w_in)


def _prepare_params(norm_w, w_in, conv_w, conv_b, dt_bias, a_log, d_skip, ssm_norm_w, q_norm_w, k_norm_w, w_out):
    w_a, w_qkvg = _cast_w_in(w_in)
    w_dt = _pad_to(w_in[:, OFF_DT:OFF_Q], DT_COLS, 1).astype(BF16)
    qk_w = jnp.concatenate([jnp.tile(q_norm_w, ATT_HEADS), jnp.tile(k_norm_w, ATT_HEADS)]).reshape(1, 2 * D_ATT)
    convp = _pad_to(jnp.concatenate([conv_w, conv_b[None]], axis=0), SUBLANES, 0)
    hp = _pad_to(_pad_to(jnp.stack([dt_bias, a_log]), LANES, 1), SUBLANES, 0)
    dexp = jnp.repeat(d_skip, SSM_HEADDIM).reshape(1, D_SSM)
    head_of_col = np.arange(D_SSM) // SSM_HEADDIM
    expand = (np.arange(LANES)[:, None] == head_of_col[None, :]).astype(np.float32)
    expand = jnp.asarray(np.tile(expand, (N_SPLIT, 1)), dtype=BF16)
    return dict(norm_w=norm_w.reshape(1, D_MODEL), w_qkvg=w_qkvg, w_a=w_a, w_dt=w_dt, qk_w=qk_w,
                convp=convp, hp=hp, dexp=dexp, ssm_nw=ssm_norm_w.reshape(1, D_SSM), expand=expand,
                w_out=w_out.astype(BF16))


def _rel_bias_table(rel_bias, q_rows, hist_rows, span_pad, band_chunk):
    n_heads = rel_bias.shape[0]
    period = span_pad + q_rows
    k = np.arange(period)
    rel = np.clip(hist_rows + q_rows - 1 - k, -REL_CLIP, REL_CLIP) + REL_CLIP
    onehot = jnp.asarray(np.eye(N_REL, dtype=np.float32)[:, rel])
    diag_row = jnp.dot(rel_bias.astype(F32), onehot, precision=lax.Precision.HIGHEST)
    flat = jnp.tile(diag_row, (1, q_rows))[:, :q_rows * (period - 1)]
    table = flat.reshape(n_heads, q_rows, period - 1)[:, :, q_rows - 1:q_rows - 1 + span_pad]
    i_idx = np.arange(q_rows)[:, None]
    j_idx = np.arange(span_pad)[None, :]
    valid = j_idx < hist_rows + q_rows
    if band_chunk is not None:
        start = (i_idx // band_chunk) * band_chunk
        valid = valid & (j_idx >= start) & (j_idx < start + hist_rows + band_chunk)
    return jnp.where(jnp.asarray(np.broadcast_to(valid, (q_rows, span_pad)))[None], table / ATT_SCALE, MASK_VALUE)


def _layer(x, hist, s0, cache_k, cache_v, prm, rel_bias, *, n_streams, seq, prompt, proj_tm):
    t = n_streams * seq
    x2 = x.reshape(t, D_MODEL)
    proj = _in_projection(x2, prm["norm_w"], prm["w_qkvg"], prm["w_a"], prm["w_dt"], prm["qk_w"],
                          tm=proj_tm, emit_kv=not prompt)
    if prompt:
        qkv, rest, dt = proj
        q_rows, n_chunks = SSD_ROWS, seq // SSD_ROWS
        kv_rows = BAND_ROWS
        kv_new = _kv_rows(x2, prm["norm_w"], prm["w_qkvg"], prm["qk_w"], first_row=t - kv_rows, n_rows=kv_rows,
                          tm=512)
    else:
        qkv, rest, dt, kv_new = proj
        q_rows, n_chunks = seq, 1
        kv_rows = t
    y_ssm, s_new = _ssd_branch(rest, dt, hist, s0, prm["convp"], prm["hp"], prm["dexp"], prm["ssm_nw"],
                               prm["expand"], n_streams=n_streams, q_rows=q_rows, n_chunks=n_chunks)
    if prompt:
        bias = _rel_bias_table(rel_bias, ATT_Q_ROWS, BAND_ROWS, BAND_ROWS + ATT_Q_ROWS, CHUNK)
        y_att = _prompt_attention(qkv, rest, bias)
    else:
        bias = _rel_bias_table(rel_bias, seq, BAND_ROWS, BAND_ROWS + LANES, None)
        y_att = _sample_attention(qkv, rest, cache_k, cache_v, bias, n_streams=n_streams, q_rows=seq)
    y = _out_projection(y_ssm, y_att, prm["w_out"], x2, tm=1024, tn=1024)
    new_conv = rest.reshape(n_streams, seq, R_COLS)[:, seq - (CONV_WIDTH - 1):, R_X:]
    kv_streams = kv_rows // n_streams
    kh = kv_new[:, :D_ATT].reshape(n_streams, kv_streams, ATT_HEADS, ATT_HEAD_DIM)
    vh = kv_new[:, D_ATT:].reshape(n_streams, kv_streams, ATT_HEADS, ATT_HEAD_DIM)
    new_ssm = s_new.reshape(n_streams, SSM_HEADS, SSM_HEADDIM, D_STATE)
    return y.reshape(n_streams, seq, D_MODEL), new_conv, new_ssm, kh, vh


def kernel(x_prompt, x_sample, state_conv, state_ssm, cache_k, cache_v, norm_w, w_in, conv_w, conv_b, dt_bias, a_log, d_skip, ssm_norm_w, q_norm_w, k_norm_w, rel_bias, w_out):
    bp, lp, _ = x_prompt.shape
    bs, ls, _ = x_sample.shape
    assert bp == 1 and lp % 1024 == 0 and lp >= BAND_ROWS
    assert ls % (2 * SUBLANES) == 0 and ls <= LANES and (bs * ls) % 1024 == 0 and cache_k.shape[2] == BAND_ROWS
    assert norm_w.shape[0] == 1
    prm = _prepare_params(norm_w[0], w_in[0], conv_w[0], conv_b[0], dt_bias[0], a_log[0], d_skip[0],
                          ssm_norm_w[0], q_norm_w[0], k_norm_w[0], w_out[0])
    rb = rel_bias[0]

    zero_hist = jnp.zeros((bp, SUBLANES, CONV_DIM), F32)
    zero_state = jnp.zeros((bp, D_SSM, D_STATE), F32)
    yp, c1, s1, k1, v1 = _layer(x_prompt, zero_hist, zero_state, None, None, prm, rb,
                                n_streams=bp, seq=lp, prompt=True, proj_tm=1024)

    hist = jnp.pad(state_conv[0], ((0, 0), (SUBLANES - (CONV_WIDTH - 1), 0), (0, 0)))
    s0 = state_ssm[0].reshape(bs, D_SSM, D_STATE)
    half_heads = (bs, BAND_ROWS, ATT_HEADS // SUBLANES, SUBLANES, ATT_HEAD_DIM)
    ck = cache_k[0].reshape(half_heads)
    cv = cache_v[0].reshape(half_heads)
    ys, c2, s2, k2, v2 = _layer(x_sample, hist, s0, ck, cv, prm, rb, n_streams=bs, seq=ls, prompt=False,
                                proj_tm=512)

    return (yp, ys, c1[None], s1[None], k1[None], v1[None], c2[None], s2[None], k2[None], v2[None])
```

```python
import functools

import jax
import jax.numpy as jnp
import numpy as np
from jax import lax
from jax.experimental import pallas as pl
from jax.experimental.pallas import tpu as pltpu

F32 = jnp.float32
BF16 = jnp.bfloat16

D_MODEL = 2048
CHUNK = 64
LEFT_CHUNKS = 8
BAND_ROWS = LEFT_CHUNKS * CHUNK
D_SSM = 2048
D_ATT = 2048
SSM_HEADDIM = 64
SSM_HEADS = 32
SSM_GROUPS = 4
HEADS_PER_GROUP = SSM_HEADS // SSM_GROUPS
GROUP_COLS = HEADS_PER_GROUP * SSM_HEADDIM
D_STATE = 128
BC_COLS = 2 * SSM_GROUPS * D_STATE
CONV_WIDTH = 4
CONV_DIM = D_SSM + BC_COLS
ATT_HEAD_DIM = 128
ATT_HEADS = 16
REL_CLIP = 128
N_REL = 2 * REL_CLIP + 1
EPS = 1e-6
OFF_Z = 0
OFF_XBC = OFF_Z + D_SSM
OFF_DT = OFF_XBC + CONV_DIM
OFF_Q = OFF_DT + SSM_HEADS
OFF_K = OFF_Q + D_ATT
OFF_V = OFF_K + D_ATT
OFF_G = OFF_V + D_ATT

LANES = 128
SUBLANES = 8
VMEM_LIMIT_BYTES = 56 * 1024 * 1024

R_G = 0
R_Z = R_G + D_ATT
R_X = R_Z + D_SSM
R_BC = R_X + D_SSM
R_COLS = R_BC + BC_COLS
DT_COLS = LANES
QKV_K, QKV_V, QKV_Q = 0, 1, 2
N_SPLIT = 3

PROJ_TN = 1024
SSD_ROWS = 128
ATT_Q_ROWS = 128
SOFTMAX_SLAB = 32
MASK_VALUE = -1e30
ATT_SCALE = ATT_HEAD_DIM ** -0.5
LOG2E = 1.4426950408889634


def _silu(v):
    return v * (1.0 / (1.0 + jnp.exp(-v)))


def _compiler_params(semantics):
    return pltpu.CompilerParams(dimension_semantics=semantics, vmem_limit_bytes=VMEM_LIMIT_BYTES)


def _normed_rows(x_ref, nw_ref):
    x = x_ref[...]
    ms = jnp.mean(x * x, axis=-1, keepdims=True)
    return (x * lax.rsqrt(ms + EPS) * nw_ref[...]).astype(BF16)


def _head_norm(a, use_norm, w):
    r = lax.rsqrt(jnp.mean(a * a, axis=-1, keepdims=True) + EPS)
    return a * jnp.where(use_norm, r, 1.0) * jnp.where(use_norm, w, 1.0)


def _inproj_kernel(*refs, n_qk, n_qkv, n_g, emit_kv):
    x_ref, nw_ref, wqkvg_ref, wa_ref, wdt_ref, qkw_ref, qkv_ref, rest_ref, dt_ref = refs[:9]
    kv_ref = refs[9] if emit_kv else None
    h_scr = refs[-1]
    j = pl.program_id(1)
    heads_per_tile = PROJ_TN // ATT_HEAD_DIM

    @pl.when(j == 0)
    def _():
        h = _normed_rows(x_ref, nw_ref)
        h_scr[...] = h
        dt_ref[...] = jnp.dot(h, wdt_ref[...], preferred_element_type=F32)

    @pl.when(j < n_qkv)
    def _():
        acc = jnp.dot(h_scr[...], wqkvg_ref[...], preferred_element_type=F32)
        for hh in range(heads_per_tile):
            sl = slice(hh * ATT_HEAD_DIM, (hh + 1) * ATT_HEAD_DIM)
            normed = _head_norm(acc[:, sl], j < n_qk, qkw_ref[:, sl])
            qkv_ref[0, hh] = normed.astype(BF16)
            if emit_kv:
                kv_ref[:, sl] = normed

    @pl.when(jnp.logical_and(j >= n_qkv, j < n_qkv + n_g))
    def _():
        rest_ref[...] = jnp.dot(h_scr[...], wqkvg_ref[...], preferred_element_type=F32)

    @pl.when(j >= n_qkv + n_g)
    def _():
        rest_ref[...] = jnp.dot(h_scr[...], wa_ref[...], preferred_element_type=F32)


def _in_projection(x, norm_w, w_qkvg, w_a, w_dt, qk_w, *, tm, emit_kv):
    t = x.shape[0]
    tn = PROJ_TN
    tiles_per_proj = D_ATT // tn
    n_qk, n_qkv, n_g = 2 * tiles_per_proj, 3 * tiles_per_proj, tiles_per_proj
    n_a = (D_SSM + CONV_DIM) // tn
    heads_per_tile = tn // ATT_HEAD_DIM
    grid = (t // tm, n_qkv + n_g + n_a)
    kern = functools.partial(_inproj_kernel, n_qk=n_qk, n_qkv=n_qkv, n_g=n_g, emit_kv=emit_kv)

    def qkv_index(i, j):
        jj = jnp.minimum(j, n_qkv - 1)
        plane = (jj // tiles_per_proj + QKV_Q) % 3
        return (plane, jj % tiles_per_proj, i, 0)

    out_specs = [
        pl.BlockSpec((1, heads_per_tile, tm, ATT_HEAD_DIM), qkv_index),
        pl.BlockSpec((tm, tn), lambda i, j: (i, jnp.maximum(j - n_qkv, 0))),
        pl.BlockSpec((tm, DT_COLS), lambda i, j: (i, 0)),
    ]
    out_shape = [
        jax.ShapeDtypeStruct((3, ATT_HEADS, t, ATT_HEAD_DIM), BF16),
        jax.ShapeDtypeStruct((t, R_COLS), F32),
        jax.ShapeDtypeStruct((t, DT_COLS), F32),
    ]
    if emit_kv:
        n_kv = n_qkv - tiles_per_proj
        out_specs.append(pl.BlockSpec((tm, tn), lambda i, j: (i, jnp.clip(j - tiles_per_proj, 0, n_kv - 1))))
        out_shape.append(jax.ShapeDtypeStruct((t, 2 * D_ATT), F32))
    return pl.pallas_call(
        kern,
        grid=grid,
        in_specs=[
            pl.BlockSpec((tm, D_MODEL), lambda i, j: (i, 0)),
            pl.BlockSpec((1, D_MODEL), lambda i, j: (0, 0)),
            pl.BlockSpec((D_MODEL, tn), lambda i, j: (0, jnp.minimum(j, n_qkv + n_g - 1))),
            pl.BlockSpec((D_MODEL, tn), lambda i, j: (0, jnp.clip(j - n_qkv - n_g, 0, n_a - 1))),
            pl.BlockSpec((D_MODEL, DT_COLS), lambda i, j: (0, 0)),
            pl.BlockSpec((1, tn), lambda i, j: (0, jnp.minimum(j, n_qk - 1))),
        ],
        out_specs=out_specs,
        out_shape=out_shape,
        scratch_shapes=[pltpu.VMEM((tm, D_MODEL), BF16)],
        compiler_params=_compiler_params(("parallel", "arbitrary")),
        name="in_projection",
    )(x, norm_w, w_qkvg, w_a, w_dt, qk_w)


def _kv_rows_kernel(x_ref, nw_ref, w_ref, qkw_ref, o_ref, h_scr, *, n_k):
    j = pl.program_id(1)

    @pl.when(j == 0)
    def _():
        h_scr[...] = _normed_rows(x_ref, nw_ref)

    acc = jnp.dot(h_scr[...], w_ref[...], preferred_element_type=F32)
    for hh in range(PROJ_TN // ATT_HEAD_DIM):
        sl = slice(hh * ATT_HEAD_DIM, (hh + 1) * ATT_HEAD_DIM)
        o_ref[:, sl] = _head_norm(acc[:, sl], j < n_k, qkw_ref[:, sl])


def _kv_rows(x, norm_w, w_qkvg, qk_w, *, first_row, n_rows, tm):
    tn = PROJ_TN
    tiles_per_proj = D_ATT // tn
    row0 = first_row // tm
    kern = functools.partial(_kv_rows_kernel, n_k=tiles_per_proj)
    return pl.pallas_call(
        kern,
        grid=(n_rows // tm, 2 * tiles_per_proj),
        in_specs=[
            pl.BlockSpec((tm, D_MODEL), lambda i, j: (row0 + i, 0)),
            pl.BlockSpec((1, D_MODEL), lambda i, j: (0, 0)),
            pl.BlockSpec((D_MODEL, tn), lambda i, j: (0, tiles_per_proj + j)),
            pl.BlockSpec((1, tn), lambda i, j: (0, jnp.minimum(tiles_per_proj + j, 2 * tiles_per_proj - 1))),
        ],
        out_specs=pl.BlockSpec((tm, tn), lambda i, j: (i, j)),
        out_shape=jax.ShapeDtypeStruct((n_rows, 2 * D_ATT), F32),
        scratch_shapes=[pltpu.VMEM((tm, D_MODEL), BF16)],
        compiler_params=_compiler_params(("parallel", "arbitrary")),
        name="kv_rows",
    )(x, norm_w, w_qkvg, qk_w)


def _transpose_rows_to_lanes(v):
    q = v.shape[0]
    if q < LANES:
        v = jnp.concatenate([v, jnp.zeros((LANES - q, LANES), v.dtype)], axis=0)
    return v.T[:, 0:q]


def _split_bf16(v):
    pieces = []
    rem = v
    for _ in range(N_SPLIT):
        piece = rem.astype(BF16)
        pieces.append(piece)
        rem = rem - piece.astype(F32)
    return jnp.concatenate(pieces, axis=1)


def _ssd_kernel(x_ref, bc_ref, z_ref, dt_ref, hist_ref, s0_ref, convp_ref, hp_ref, dexp_ref, nw_ref, expand_ref,
                y_ref, sout_ref, conv_scr, act_scr, st_scr, yd_scr, exp_scr, *, q_rows, n_chunks):
    c = pl.program_id(1)
    q = q_rows

    @pl.when(c == 0)
    def _():
        conv_scr[0:SUBLANES, :] = hist_ref[0]
        for g in range(SSM_GROUPS):
            st_scr[g] = s0_ref[0, g * GROUP_COLS:(g + 1) * GROUP_COLS, :].T

    conv_scr[SUBLANES:SUBLANES + q, 0:D_SSM] = x_ref[...]
    conv_scr[SUBLANES:SUBLANES + q, D_SSM:] = bc_ref[...]

    for c0 in range(0, CONV_DIM, GROUP_COLS):
        cols = slice(c0, c0 + GROUP_COLS)
        xp = conv_scr[:, cols]
        conv = convp_ref[CONV_WIDTH:CONV_WIDTH + 1, cols] + convp_ref[CONV_WIDTH - 1:CONV_WIDTH, cols] * xp[SUBLANES:]
        for shift in range(1, CONV_WIDTH):
            tap = CONV_WIDTH - 1 - shift
            conv = conv + convp_ref[tap:tap + 1, cols] * pltpu.roll(xp, shift, axis=0)[SUBLANES:]
        act_scr[:, cols] = _silu(conv)
    conv_scr[0:SUBLANES, :] = conv_scr[q:q + SUBLANES, :]

    v = dt_ref[...] + hp_ref[0:1, :]
    dt = jnp.maximum(v, 0.0) + jnp.log1p(jnp.exp(-jnp.abs(v)))
    a = dt * (-jnp.exp(hp_ref[1:2, :]))
    ii = lax.broadcasted_iota(jnp.int32, (q, q), 0)
    jj = lax.broadcasted_iota(jnp.int32, (q, q), 1)
    causal = ii >= jj
    acum = jnp.dot(causal.astype(F32), a, precision=lax.Precision.HIGHEST, preferred_element_type=F32)
    acum2 = acum * LOG2E
    row_t = _transpose_rows_to_lanes(acum2 - jnp.log(dt) * LOG2E)
    last = acum[q - 1:q, :]
    factors = jnp.concatenate(
        [jnp.exp(acum),
         jnp.exp(last - acum) * dt,
         jnp.broadcast_to(jnp.exp(last), (SUBLANES, LANES))], axis=0)
    exp_scr[...] = jnp.dot(_split_bf16(factors), expand_ref[...], preferred_element_type=F32)

    for g in range(SSM_GROUPS):
        cols = slice(g * GROUP_COLS, (g + 1) * GROUP_COLS)
        xact = act_scr[:, cols]
        bact = act_scr[:, D_SSM + g * D_STATE:D_SSM + (g + 1) * D_STATE].astype(BF16)
        cact = act_scr[:, D_SSM + (SSM_GROUPS + g) * D_STATE:D_SSM + (SSM_GROUPS + g + 1) * D_STATE].astype(BF16)
        cb = lax.dot_general(cact, bact, (((1,), (1,)), ((), ())), preferred_element_type=F32)
        st = st_scr[g]
        y_off = jnp.dot(cact, st.astype(BF16), preferred_element_type=F32)
        for r in range(HEADS_PER_GROUP):
            h = g * HEADS_PER_GROUP + r
            m = cb * jnp.exp2(jnp.where(causal, acum2[:, h:h + 1] - row_t[h:h + 1, :], -jnp.inf))
            xh = xact[:, r * SSM_HEADDIM:(r + 1) * SSM_HEADDIM]
            yd_scr[:, h * SSM_HEADDIM:(h + 1) * SSM_HEADDIM] = jnp.dot(
                m.astype(BF16), xh.astype(BF16), preferred_element_type=F32)
        y = yd_scr[:, cols] + y_off * exp_scr[0:q, cols] + dexp_ref[:, cols] * xact
        xw = (xact * exp_scr[q:2 * q, cols]).astype(BF16)
        upd = lax.dot_general(bact, xw, (((0,), (0,)), ((), ())), preferred_element_type=F32)
        st_scr[g] = st * exp_scr[2 * q:2 * q + 1, cols] + upd

        yg = y * _silu(z_ref[:, cols])
        rn = lax.rsqrt(jnp.mean(yg * yg, axis=-1, keepdims=True) + EPS)
        y_ref[:, cols] = (yg * rn * nw_ref[:, cols]).astype(y_ref.dtype)

    @pl.when(c == n_chunks - 1)
    def _():
        for g in range(SSM_GROUPS):
            sout_ref[0, g * GROUP_COLS:(g + 1) * GROUP_COLS, :] = st_scr[g].T


def _ssd_branch(rest, dt, hist, s0, convp, hp, dexp, ssm_nw, expand, *, n_streams, q_rows, n_chunks):
    t = rest.shape[0]

    def rows(b, c):
        return b * n_chunks + c

    def const(shape):
        return pl.BlockSpec(shape, lambda b, c: (0,) * len(shape))

    kern = functools.partial(_ssd_kernel, q_rows=q_rows, n_chunks=n_chunks)
    return pl.pallas_call(
        kern,
        grid=(n_streams, n_chunks),
        in_specs=[
            pl.BlockSpec((q_rows, D_SSM), lambda b, c: (rows(b, c), R_X // D_SSM)),
            pl.BlockSpec((q_rows, BC_COLS), lambda b, c: (rows(b, c), R_BC // BC_COLS)),
            pl.BlockSpec((q_rows, D_SSM), lambda b, c: (rows(b, c), R_Z // D_SSM)),
            pl.BlockSpec((q_rows, DT_COLS), lambda b, c: (rows(b, c), 0)),
            pl.BlockSpec((1, SUBLANES, CONV_DIM), lambda b, c: (b, 0, 0)),
            pl.BlockSpec((1, D_SSM, D_STATE), lambda b, c: (b, 0, 0)),
            const((SUBLANES, CONV_DIM)),
            const((SUBLANES, LANES)),
            const((1, D_SSM)),
            const((1, D_SSM)),
            const((N_SPLIT * LANES, D_SSM)),
        ],
        out_specs=[
            pl.BlockSpec((q_rows, D_SSM), lambda b, c: (rows(b, c), 0)),
            pl.BlockSpec((1, D_SSM, D_STATE), lambda b, c: (b, 0, 0)),
        ],
        out_shape=[
            jax.ShapeDtypeStruct((t, D_SSM), BF16),
            jax.ShapeDtypeStruct((n_streams, D_SSM, D_STATE), F32),
        ],
        scratch_shapes=[
            pltpu.VMEM((SUBLANES + q_rows, CONV_DIM), F32),
            pltpu.VMEM((q_rows, CONV_DIM), F32),
            pltpu.VMEM((SSM_GROUPS, D_STATE, GROUP_COLS), F32),
            pltpu.VMEM((q_rows, D_SSM), F32),
            pltpu.VMEM((2 * q_rows + SUBLANES, D_SSM), F32),
        ],
        compiler_params=_compiler_params(("parallel", "arbitrary")),
        name="ssd_branch",
    )(rest, rest, rest, dt, hist, s0, convp, hp, dexp, ssm_nw, expand)


def _softmax_stage(s_scr, p_scr, inv_scr, bias_ref, h, q_rows):
    exp2_scale = ATT_SCALE * LOG2E
    slab = min(SOFTMAX_SLAB, q_rows)
    bias_rows = bias_ref.shape[1]
    for r0 in range(0, q_rows, slab):
        b0 = r0 % bias_rows
        u = s_scr[h, r0:r0 + slab, :] + bias_ref[h, b0:b0 + slab, :]
        m = jnp.max(u, axis=-1, keepdims=True)
        e = jnp.exp2((u - m) * exp2_scale)
        p_scr[h, r0:r0 + slab, :] = e.astype(BF16)
        inv = 1.0 / jnp.sum(e, axis=-1, keepdims=True)
        inv_scr[h, r0:r0 + slab, :] = jnp.broadcast_to(inv, (slab, ATT_HEAD_DIM))


def _head_pipeline(qk_stage, softmax_stage, pv_stage):
    for step in range(ATT_HEADS + 2):
        if step < ATT_HEADS:
            qk_stage(step)
        if 1 <= step <= ATT_HEADS:
            softmax_stage(step - 1)
        if step >= 2:
            pv_stage(step - 2)


def _prompt_attn_kernel(*refs, n_hist_blocks):
    n_win = n_hist_blocks + 2
    tq = ATT_Q_ROWS
    q_ref, g_ref = refs[0], refs[1]
    kv_refs = refs[2:2 + n_win]
    bias_ref, o_ref, s_scr, p_scr, inv_scr = refs[2 + n_win:]
    first_block = 2 * pl.program_id(0) - n_hist_blocks

    def block_rows(blk):
        lo = 0 if blk <= n_hist_blocks else tq
        hi = 2 * tq if blk >= 1 else tq
        return lo, hi

    def qk_stage(h):
        for blk in range(n_win):
            lo, hi = block_rows(blk)
            s = lax.dot_general(q_ref[0, h, lo:hi, :], kv_refs[blk][QKV_K, h], (((1,), (1,)), ((), ())),
                                preferred_element_type=F32)
            if blk < n_hist_blocks:
                s = jnp.where(first_block + blk >= 0, s, MASK_VALUE)
            if lo == 0:
                s_scr[h, 0:tq, blk * tq:(blk + 1) * tq] = s[0:tq]
            if hi == 2 * tq:
                s_scr[h, tq:2 * tq, (blk - 1) * tq:blk * tq] = s[tq - lo:2 * tq - lo]

    def pv_stage(h):
        sl = slice(h * ATT_HEAD_DIM, (h + 1) * ATT_HEAD_DIM)
        o_first = jnp.zeros((tq, ATT_HEAD_DIM), F32)
        o_second = jnp.zeros((tq, ATT_HEAD_DIM), F32)
        for blk in range(n_win):
            lo, hi = block_rows(blk)
            parts = []
            if lo == 0:
                parts.append(p_scr[h, 0:tq, blk * tq:(blk + 1) * tq])
            if hi == 2 * tq:
                parts.append(p_scr[h, tq:2 * tq, (blk - 1) * tq:blk * tq])
            p = parts[0] if len(parts) == 1 else jnp.concatenate(parts, axis=0)
            o = jnp.dot(p, kv_refs[blk][QKV_V, h], preferred_element_type=F32)
            if lo == 0:
                o_first = o_first + o[0:tq]
            if hi == 2 * tq:
                o_second = o_second + o[tq - lo:2 * tq - lo]
        o = jnp.concatenate([o_first, o_second], axis=0)
        o_ref[:, sl] = (o * inv_scr[h] * _silu(g_ref[:, sl])).astype(o_ref.dtype)

    _head_pipeline(qk_stage,
                   functools.partial(_softmax_stage, s_scr, p_scr, inv_scr, bias_ref, q_rows=2 * tq), pv_stage)


def _prompt_attention(qkv, rest, bias):
    t = rest.shape[0]
    tq = ATT_Q_ROWS
    n_hist_blocks = BAND_ROWS // tq
    n_win = n_hist_blocks + 2
    span = BAND_ROWS + tq
    assert (QKV_K, QKV_V) == (0, 1)
    kv_block = (2, ATT_HEADS, tq, ATT_HEAD_DIM)

    def kv_spec(blk):
        return pl.BlockSpec(kv_block, lambda i: (0, 0, jnp.maximum(2 * i - n_hist_blocks + blk, 0), 0))

    in_specs = [pl.BlockSpec((1, ATT_HEADS, 2 * tq, ATT_HEAD_DIM), lambda i: (QKV_Q, 0, i, 0)),
                pl.BlockSpec((2 * tq, D_ATT), lambda i: (i, R_G // D_ATT))]
    in_specs += [kv_spec(blk) for blk in range(n_win)]
    in_specs += [pl.BlockSpec((ATT_HEADS, tq, span), lambda i: (0, 0, 0))]
    kern = functools.partial(_prompt_attn_kernel, n_hist_blocks=n_hist_blocks)
    return pl.pallas_call(
        kern,
        grid=(t // (2 * tq),),
        in_specs=in_specs,
        out_specs=pl.BlockSpec((2 * tq, D_ATT), lambda i: (i, 0)),
        out_shape=jax.ShapeDtypeStruct((t, D_ATT), BF16),
        scratch_shapes=[pltpu.VMEM((ATT_HEADS, 2 * tq, span), F32),
                        pltpu.VMEM((ATT_HEADS, 2 * tq, span), BF16),
                        pltpu.VMEM((ATT_HEADS, 2 * tq, ATT_HEAD_DIM), F32)],
        compiler_params=_compiler_params(("parallel",)),
        name="prompt_attention",
    )(qkv, rest, *([qkv] * n_win), bias)


def _sample_attn_kernel(q_ref, kn_ref, vn_ref, g_ref, kc_lo, kc_hi, vc_lo, vc_hi, bias_ref, o_ref,
                        s_scr, p_scr, inv_scr, *, q_rows):
    new_pad = LANES - q_rows
    zpad = jnp.zeros((new_pad, ATT_HEAD_DIM), BF16)

    def cached(lo_ref, hi_ref, h):
        ref = (lo_ref if h < SUBLANES else hi_ref).reshape(BAND_ROWS * SUBLANES, ATT_HEAD_DIM)
        return ref[pl.ds(h % SUBLANES, BAND_ROWS, stride=SUBLANES), :].astype(BF16)

    def qk_stage(h):
        qh = q_ref[0, h]
        s_scr[h, :, 0:BAND_ROWS] = lax.dot_general(qh, cached(kc_lo, kc_hi, h), (((1,), (1,)), ((), ())),
                                                   preferred_element_type=F32)
        k_new = jnp.concatenate([kn_ref[0, h], zpad], axis=0)
        s_scr[h, :, BAND_ROWS:] = lax.dot_general(qh, k_new, (((1,), (1,)), ((), ())), preferred_element_type=F32)

    def pv_stage(h):
        sl = slice(h * ATT_HEAD_DIM, (h + 1) * ATT_HEAD_DIM)
        v_new = jnp.concatenate([vn_ref[0, h], zpad], axis=0)
        o = jnp.dot(p_scr[h, :, 0:BAND_ROWS], cached(vc_lo, vc_hi, h), preferred_element_type=F32)
        o = o + jnp.dot(p_scr[h, :, BAND_ROWS:], v_new, preferred_element_type=F32)
        o_ref[:, sl] = (o * inv_scr[h] * _silu(g_ref[:, sl])).astype(o_ref.dtype)

    _head_pipeline(qk_stage, functools.partial(_softmax_stage, s_scr, p_scr, inv_scr, bias_ref, q_rows=q_rows),
                   pv_stage)


def _sample_attention(qkv, rest, cache_k, cache_v, bias, *, n_streams, q_rows):
    t = rest.shape[0]
    span_pad = BAND_ROWS + LANES
    kern = functools.partial(_sample_attn_kernel, q_rows=q_rows)

    def new_spec(which):
        return pl.BlockSpec((1, ATT_HEADS, q_rows, ATT_HEAD_DIM), lambda b: (which, 0, b, 0))

    def cache_spec(half):
        return pl.BlockSpec((None, BAND_ROWS, None, SUBLANES, ATT_HEAD_DIM), lambda b: (b, 0, half, 0, 0))

    return pl.pallas_call(
        kern,
        grid=(n_streams,),
        in_specs=[
            new_spec(QKV_Q), new_spec(QKV_K), new_spec(QKV_V),
            pl.BlockSpec((q_rows, D_ATT), lambda b: (b, R_G // D_ATT)),
            cache_spec(0), cache_spec(1), cache_spec(0), cache_spec(1),
            pl.BlockSpec((ATT_HEADS, q_rows, span_pad), lambda b: (0, 0, 0)),
        ],
        out_specs=pl.BlockSpec((q_rows, D_ATT), lambda b: (b, 0)),
        out_shape=jax.ShapeDtypeStruct((t, D_ATT), BF16),
        scratch_shapes=[pltpu.VMEM((ATT_HEADS, q_rows, span_pad), F32),
                        pltpu.VMEM((ATT_HEADS, q_rows, span_pad), BF16),
                        pltpu.VMEM((ATT_HEADS, q_rows, ATT_HEAD_DIM), F32)],
        compiler_params=_compiler_params(("parallel",)),
        name="sample_attention",
    )(qkv, qkv, qkv, rest, cache_k, cache_k, cache_v, cache_v, bias)


def _outproj_kernel(ys_ref, ya_ref, w1_ref, w2_ref, x_ref, o_ref):
    acc = jnp.dot(ys_ref[...], w1_ref[...], preferred_element_type=F32)
    acc = acc + jnp.dot(ya_ref[...], w2_ref[...], preferred_element_type=F32)
    o_ref[...] = x_ref[...] + acc


def _out_projection(y_ssm, y_att, w_out, x, *, tm, tn):
    t = x.shape[0]
    return pl.pallas_call(
        _outproj_kernel,
        grid=(t // tm, D_MODEL // tn),
        in_specs=[
            pl.BlockSpec((tm, D_SSM), lambda i, j: (i, 0)),
            pl.BlockSpec((tm, D_ATT), lambda i, j: (i, 0)),
            pl.BlockSpec((D_SSM, tn), lambda i, j: (0, j)),
            pl.BlockSpec((D_ATT, tn), lambda i, j: (D_SSM // D_ATT, j)),
            pl.BlockSpec((tm, tn), lambda i, j: (i, j)),
        ],
        out_specs=pl.BlockSpec((tm, tn), lambda i, j: (i, j)),
        out_shape=jax.ShapeDtypeStruct((t, D_MODEL), F32),
        compiler_params=_compiler_params(("parallel", "arbitrary")),
        name="out_projection",
    )(y_ssm, y_att, w_out, w_out, x)


def _pad_to(v, size, axis):
    pad = [(0, 0)] * v.ndim
    pad[axis] = (0, size - v.shape[axis])
    return jnp.pad(v, pad)


def _cast_w_in_kernel(main_ref, next_ref, wa_ref, wq_ref, wdt_ref, *, n_a, row_shift):
    j = pl.program_id(0)
    tn = main_ref.shape[0]
    chunk = 256

    @pl.when(j == n_a)
    def _():
        row_id = lax.broadcasted_iota(jnp.int32, (DT_COLS, main_ref.shape[1]), 0)
        wdt_ref[...] = jnp.where(row_id < row_shift, main_ref[0:DT_COLS, :], 0.0).T.astype(BF16)

    def put(rows_of, out_ref):
        for r0 in range(0, tn, chunk):
            out_ref[:, r0:r0 + chunk] = rows_of(r0).T.astype(BF16)

    @pl.when(j < n_a)
    def _():
        put(lambda r0: main_ref[r0:r0 + chunk, :], wa_ref)

    @pl.when(j >= n_a)
    def _():
        def shifted(r0):
            lo = r0 + row_shift
            if lo + chunk <= tn:
                return main_ref[lo:lo + chunk, :]
            return jnp.concatenate([main_ref[lo:tn, :], next_ref[0:lo + chunk - tn, :]], axis=0)

        put(shifted, wq_ref)


def _cast_w_in(w_in_t):
    tn = PROJ_TN
    n_a = OFF_DT // tn
    n_q = (4 * D_ATT) // tn
    row_shift = OFF_Q - OFF_DT
    assert OFF_DT % tn == 0 and row_shift % SUBLANES == 0 and 0 < row_shift < LANES
    kern = functools.partial(_cast_w_in_kernel, n_a=n_a, row_shift=row_shift)
    return pl.pallas_call(
        kern,
        grid=(n_a + n_q,),
        in_specs=[
            pl.BlockSpec((tn, D_MODEL), lambda j: (j, 0)),
            pl.BlockSpec((LANES, D_MODEL), lambda j: ((j + 1) * (tn // LANES), 0)),
        ],
        out_specs=[
            pl.BlockSpec((D_MODEL, tn), lambda j: (0, jnp.minimum(j, n_a - 1))),
            pl.BlockSpec((D_MODEL, tn), lambda j: (0, jnp.maximum(j - n_a, 0))),
            pl.BlockSpec((D_MODEL, DT_COLS), lambda j: (0, 0)),
        ],
        out_shape=[
            jax.ShapeDtypeStruct((D_MODEL, n_a * tn), BF16),
            jax.ShapeDtypeStruct((D_MODEL, n_q * tn), BF16),
            jax.ShapeDtypeStruct((D_MODEL, DT_COLS), BF16),
        ],
        compiler_params=_compiler_params(("arbitrary",)),
        name="cast_w_in",
    )(w_in_t, w_in_t)


def _prepare_params(norm_w, w_in, conv_w, conv_b, dt_bias, a_log, d_skip, ssm_norm_w, q_norm_w, k_norm_w, w_out):
    w_a, w_qkvg, w_dt = _cast_w_in(w_in.T)
    qk_w = jnp.concatenate([jnp.tile(q_norm_w, ATT_HEADS), jnp.tile(k_norm_w, ATT_HEADS)]).reshape(1, 2 * D_ATT)
    convp = _pad_to(jnp.concatenate([conv_w, conv_b[None]], axis=0), SUBLANES, 0)
    hp = _pad_to(_pad_to(jnp.stack([dt_bias, a_log]), LANES, 1), SUBLANES, 0)
    dexp = jnp.repeat(d_skip, SSM_HEADDIM).reshape(1, D_SSM)
    head_of_col = np.arange(D_SSM) // SSM_HEADDIM
    expand = (np.arange(LANES)[:, None] == head_of_col[None, :]).astype(np.float32)
    expand = jnp.asarray(np.tile(expand, (N_SPLIT, 1)), dtype=BF16)
    return dict(norm_w=norm_w.reshape(1, D_MODEL), w_qkvg=w_qkvg, w_a=w_a, w_dt=w_dt, qk_w=qk_w,
                convp=convp, hp=hp, dexp=dexp, ssm_nw=ssm_norm_w.reshape(1, D_SSM), expand=expand,
                w_out=w_out.astype(BF16))


def _rel_bias_table(rel_bias, q_rows, hist_rows, span_pad, band_chunk):
    n_heads = rel_bias.shape[0]
    period = span_pad + q_rows
    k = np.arange(period)
    rel = np.clip(hist_rows + q_rows - 1 - k, -REL_CLIP, REL_CLIP) + REL_CLIP
    onehot = jnp.asarray(np.eye(N_REL, dtype=np.float32)[:, rel])
    diag_row = jnp.dot(rel_bias.astype(F32), onehot, precision=lax.Precision.HIGHEST)
    flat = jnp.tile(diag_row, (1, q_rows))[:, :q_rows * (period - 1)]
    table = flat.reshape(n_heads, q_rows, period - 1)[:, :, q_rows - 1:q_rows - 1 + span_pad]
    i_idx = np.arange(q_rows)[:, None]
    j_idx = np.arange(span_pad)[None, :]
    valid = j_idx < hist_rows + q_rows
    if band_chunk is not None:
        start = (i_idx // band_chunk) * band_chunk
        valid = valid & (j_idx >= start) & (j_idx < start + hist_rows + band_chunk)
    return jnp.where(jnp.asarray(np.broadcast_to(valid, (q_rows, span_pad)))[None], table / ATT_SCALE, MASK_VALUE)


def _layer(x, hist, s0, cache_k, cache_v, prm, rel_bias, *, n_streams, seq, prompt, proj_tm):
    t = n_streams * seq
    x2 = x.reshape(t, D_MODEL)
    proj = _in_projection(x2, prm["norm_w"], prm["w_qkvg"], prm["w_a"], prm["w_dt"], prm["qk_w"],
                          tm=proj_tm, emit_kv=not prompt)
    if prompt:
        qkv, rest, dt = proj
        q_rows, n_chunks = SSD_ROWS, seq // SSD_ROWS
        kv_rows = BAND_ROWS
        kv_new = _kv_rows(x2, prm["norm_w"], prm["w_qkvg"], prm["qk_w"], first_row=t - kv_rows, n_rows=kv_rows,
                          tm=512)
    else:
        qkv, rest, dt, kv_new = proj
        q_rows, n_chunks = seq, 1
        kv_rows = t
    y_ssm, s_new = _ssd_branch(rest, dt, hist, s0, prm["convp"], prm["hp"], prm["dexp"], prm["ssm_nw"],
                               prm["expand"], n_streams=n_streams, q_rows=q_rows, n_chunks=n_chunks)
    if prompt:
        bias = _rel_bias_table(rel_bias, ATT_Q_ROWS, BAND_ROWS, BAND_ROWS + ATT_Q_ROWS, CHUNK)
        y_att = _prompt_attention(qkv, rest, bias)
    else:
        bias = _rel_bias_table(rel_bias, seq, BAND_ROWS, BAND_ROWS + LANES, None)
        y_att = _sample_attention(qkv, rest, cache_k, cache_v, bias, n_streams=n_streams, q_rows=seq)
    y = _out_projection(y_ssm, y_att, prm["w_out"], x2, tm=1024, tn=1024)
    new_conv = rest.reshape(n_streams, seq, R_COLS)[:, seq - (CONV_WIDTH - 1):, R_X:]
    kv_streams = kv_rows // n_streams
    kh = kv_new[:, :D_ATT].reshape(n_streams, kv_streams, ATT_HEADS, ATT_HEAD_DIM)
    vh = kv_new[:, D_ATT:].reshape(n_streams, kv_streams, ATT_HEADS, ATT_HEAD_DIM)
    new_ssm = s_new.reshape(n_streams, SSM_HEADS, SSM_HEADDIM, D_STATE)
    return y.reshape(n_streams, seq, D_MODEL), new_conv, new_ssm, kh, vh


def kernel(x_prompt, x_sample, state_conv, state_ssm, cache_k, cache_v, norm_w, w_in, conv_w, conv_b, dt_bias, a_log, d_skip, ssm_norm_w, q_norm_w, k_norm_w, rel_bias, w_out):
    bp, lp, _ = x_prompt.shape
    bs, ls, _ = x_sample.shape
    assert bp == 1 and lp % 1024 == 0 and lp >= BAND_ROWS
    assert ls % (2 * SUBLANES) == 0 and ls <= LANES and (bs * ls) % 1024 == 0 and cache_k.shape[2] == BAND_ROWS
    assert norm_w.shape[0] == 1
    prm = _prepare_params(norm_w[0], w_in[0], conv_w[0], conv_b[0], dt_bias[0], a_log[0], d_skip[0],
                          ssm_norm_w[0], q_norm_w[0], k_norm_w[0], w_out[0])
    rb = rel_bias[0]

    zero_hist = jnp.zeros((bp, SUBLANES, CONV_DIM), F32)
    zero_state = jnp.zeros((bp, D_SSM, D_STATE), F32)
    yp, c1, s1, k1, v1 = _layer(x_prompt, zero_hist, zero_state, None, None, prm, rb,
                                n_streams=bp, seq=lp, prompt=True, proj_tm=1024)

    hist = jnp.pad(state_conv[0], ((0, 0), (SUBLANES - (CONV_WIDTH - 1), 0), (0, 0)))
    s0 = state_ssm[0].reshape(bs, D_SSM, D_STATE)
    half_heads = (bs, BAND_ROWS, ATT_HEADS // SUBLANES, SUBLANES, ATT_HEAD_DIM)
    ck = cache_k[0].reshape(half_heads)
    cv = cache_v[0].reshape(half_heads)
    ys, c2, s2, k2, v2 = _layer(x_sample, hist, s0, ck, cv, prm, rb, n_streams=bs, seq=ls, prompt=False,
                                proj_tm=512)

    return (yp, ys, c1[None], s1[None], k1[None], v1[None], c2[None], s2[None], k2[None], v2[None])
```

```python
import functools

import jax
import jax.numpy as jnp
import numpy as np
from jax import lax
from jax.experimental import pallas as pl
from jax.experimental.pallas import tpu as pltpu

F32 = jnp.float32
BF16 = jnp.bfloat16

D_MODEL = 2048
CHUNK = 64
LEFT_CHUNKS = 8
BAND_ROWS = LEFT_CHUNKS * CHUNK
D_SSM = 2048
D_ATT = 2048
SSM_HEADDIM = 64
SSM_HEADS = 32
SSM_GROUPS = 4
HEADS_PER_GROUP = SSM_HEADS // SSM_GROUPS
GROUP_COLS = HEADS_PER_GROUP * SSM_HEADDIM
D_STATE = 128
BC_COLS = 2 * SSM_GROUPS * D_STATE
CONV_WIDTH = 4
CONV_DIM = D_SSM + BC_COLS
ATT_HEAD_DIM = 128
ATT_HEADS = 16
REL_CLIP = 128
N_REL = 2 * REL_CLIP + 1
EPS = 1e-6
OFF_Z = 0
OFF_XBC = OFF_Z + D_SSM
OFF_DT = OFF_XBC + CONV_DIM
OFF_Q = OFF_DT + SSM_HEADS
OFF_K = OFF_Q + D_ATT
OFF_V = OFF_K + D_ATT
OFF_G = OFF_V + D_ATT

LANES = 128
SUBLANES = 8
VMEM_LIMIT_BYTES = 56 * 1024 * 1024

R_G = 0
R_Z = R_G + D_ATT
R_X = R_Z + D_SSM
R_BC = R_X + D_SSM
R_COLS = R_BC + BC_COLS
DT_COLS = LANES
QKV_K, QKV_V, QKV_Q = 0, 1, 2
N_SPLIT = 3

PROJ_TN = 1024
SSD_ROWS = 128
ATT_Q_ROWS = 128
SOFTMAX_SLAB = 32
MASK_VALUE = -1e30
ATT_SCALE = ATT_HEAD_DIM ** -0.5
LOG2E = 1.4426950408889634


def _silu(v):
    return v * (1.0 / (1.0 + jnp.exp2(v * (-LOG2E))))


def _compiler_params(semantics):
    return pltpu.CompilerParams(dimension_semantics=semantics, vmem_limit_bytes=VMEM_LIMIT_BYTES)


def _normed_rows(x_ref, nw_ref):
    x = x_ref[...]
    ms = jnp.mean(x * x, axis=-1, keepdims=True)
    return (x * lax.rsqrt(ms + EPS) * nw_ref[...]).astype(BF16)


def _head_norm(a, use_norm, w):
    r = lax.rsqrt(jnp.mean(a * a, axis=-1, keepdims=True) + EPS)
    return a * jnp.where(use_norm, r, 1.0) * jnp.where(use_norm, w, 1.0)


def _inproj_kernel(*refs, n_qk, n_qkv, n_g, emit_kv):
    x_ref, nw_ref, wqkvg_ref, wa_ref, wdt_ref, qkw_ref, qkv_ref, rest_ref, dt_ref = refs[:9]
    kv_ref = refs[9] if emit_kv else None
    h_scr = refs[-1]
    j = pl.program_id(1)
    heads_per_tile = PROJ_TN // ATT_HEAD_DIM

    @pl.when(j == 0)
    def _():
        h = _normed_rows(x_ref, nw_ref)
        h_scr[...] = h
        dt_ref[...] = jnp.dot(h, wdt_ref[...], preferred_element_type=F32)

    def qkv_tile(use_norm):
        acc = jnp.dot(h_scr[...], wqkvg_ref[...], preferred_element_type=F32)
        for hh in range(heads_per_tile):
            sl = slice(hh * ATT_HEAD_DIM, (hh + 1) * ATT_HEAD_DIM)
            head = acc[:, sl]
            if use_norm:
                r = lax.rsqrt(jnp.mean(head * head, axis=-1, keepdims=True) + EPS)
                head = head * r * qkw_ref[:, sl]
            qkv_ref[0, hh] = head.astype(BF16)
            if emit_kv:
                kv_ref[:, sl] = head

    pl.when(j < n_qk)(functools.partial(qkv_tile, True))
    pl.when(jnp.logical_and(j >= n_qk, j < n_qkv))(functools.partial(qkv_tile, False))

    @pl.when(jnp.logical_and(j >= n_qkv, j < n_qkv + n_g))
    def _():
        rest_ref[...] = jnp.dot(h_scr[...], wqkvg_ref[...], preferred_element_type=F32)

    @pl.when(j >= n_qkv + n_g)
    def _():
        rest_ref[...] = jnp.dot(h_scr[...], wa_ref[...], preferred_element_type=F32)


def _in_projection(x, norm_w, w_qkvg, w_a, w_dt, qk_w, *, tm, emit_kv):
    t = x.shape[0]
    tn = PROJ_TN
    tiles_per_proj = D_ATT // tn
    n_qk, n_qkv, n_g = 2 * tiles_per_proj, 3 * tiles_per_proj, tiles_per_proj
    n_a = (D_SSM + CONV_DIM) // tn
    heads_per_tile = tn // ATT_HEAD_DIM
    grid = (t // tm, n_qkv + n_g + n_a)
    kern = functools.partial(_inproj_kernel, n_qk=n_qk, n_qkv=n_qkv, n_g=n_g, emit_kv=emit_kv)

    def qkv_index(i, j):
        jj = jnp.minimum(j, n_qkv - 1)
        plane = (jj // tiles_per_proj + QKV_Q) % 3
        return (plane, jj % tiles_per_proj, i, 0)

    out_specs = [
        pl.BlockSpec((1, heads_per_tile, tm, ATT_HEAD_DIM), qkv_index),
        pl.BlockSpec((tm, tn), lambda i, j: (i, jnp.maximum(j - n_qkv, 0))),
        pl.BlockSpec((tm, DT_COLS), lambda i, j: (i, 0)),
    ]
    out_shape = [
        jax.ShapeDtypeStruct((3, ATT_HEADS, t, ATT_HEAD_DIM), BF16),
        jax.ShapeDtypeStruct((t, R_COLS), F32),
        jax.ShapeDtypeStruct((t, DT_COLS), F32),
    ]
    if emit_kv:
        n_kv = n_qkv - tiles_per_proj
        out_specs.append(pl.BlockSpec((tm, tn), lambda i, j: (i, jnp.clip(j - tiles_per_proj, 0, n_kv - 1))))
        out_shape.append(jax.ShapeDtypeStruct((t, 2 * D_ATT), F32))
    return pl.pallas_call(
        kern,
        grid=grid,
        in_specs=[
            pl.BlockSpec((tm, D_MODEL), lambda i, j: (i, 0)),
            pl.BlockSpec((1, D_MODEL), lambda i, j: (0, 0)),
            pl.BlockSpec((D_MODEL, tn), lambda i, j: (0, jnp.minimum(j, n_qkv + n_g - 1))),
            pl.BlockSpec((D_MODEL, tn), lambda i, j: (0, jnp.clip(j - n_qkv - n_g, 0, n_a - 1))),
            pl.BlockSpec((D_MODEL, DT_COLS), lambda i, j: (0, 0)),
            pl.BlockSpec((1, tn), lambda i, j: (0, jnp.minimum(j, n_qk - 1))),
        ],
        out_specs=out_specs,
        out_shape=out_shape,
        scratch_shapes=[pltpu.VMEM((tm, D_MODEL), BF16)],
        compiler_params=_compiler_params(("parallel", "arbitrary")),
        name="in_projection",
    )(x, norm_w, w_qkvg, w_a, w_dt, qk_w)


def _kv_rows_kernel(x_ref, nw_ref, w_ref, qkw_ref, o_ref, h_scr, *, n_k):
    j = pl.program_id(1)

    @pl.when(j == 0)
    def _():
        h_scr[...] = _normed_rows(x_ref, nw_ref)

    acc = jnp.dot(h_scr[...], w_ref[...], preferred_element_type=F32)
    for hh in range(PROJ_TN // ATT_HEAD_DIM):
        sl = slice(hh * ATT_HEAD_DIM, (hh + 1) * ATT_HEAD_DIM)
        o_ref[:, sl] = _head_norm(acc[:, sl], j < n_k, qkw_ref[:, sl])


def _kv_rows(x, norm_w, w_qkvg, qk_w, *, first_row, n_rows, tm):
    tn = PROJ_TN
    tiles_per_proj = D_ATT // tn
    row0 = first_row // tm
    kern = functools.partial(_kv_rows_kernel, n_k=tiles_per_proj)
    return pl.pallas_call(
        kern,
        grid=(n_rows // tm, 2 * tiles_per_proj),
        in_specs=[
            pl.BlockSpec((tm, D_MODEL), lambda i, j: (row0 + i, 0)),
            pl.BlockSpec((1, D_MODEL), lambda i, j: (0, 0)),
            pl.BlockSpec((D_MODEL, tn), lambda i, j: (0, tiles_per_proj + j)),
            pl.BlockSpec((1, tn), lambda i, j: (0, jnp.minimum(tiles_per_proj + j, 2 * tiles_per_proj - 1))),
        ],
        out_specs=pl.BlockSpec((tm, tn), lambda i, j: (i, j)),
        out_shape=jax.ShapeDtypeStruct((n_rows, 2 * D_ATT), F32),
        scratch_shapes=[pltpu.VMEM((tm, D_MODEL), BF16)],
        compiler_params=_compiler_params(("parallel", "arbitrary")),
        name="kv_rows",
    )(x, norm_w, w_qkvg, qk_w)


def _transpose_rows_to_lanes(v):
    q = v.shape[0]
    if q < LANES:
        v = jnp.concatenate([v, jnp.zeros((LANES - q, LANES), v.dtype)], axis=0)
    return v.T[:, 0:q]


def _split_bf16(v):
    pieces = []
    rem = v
    for _ in range(N_SPLIT):
        piece = rem.astype(BF16)
        pieces.append(piece)
        rem = rem - piece.astype(F32)
    return jnp.concatenate(pieces, axis=1)


def _ssd_kernel(x_ref, bc_ref, z_ref, dt_ref, hist_ref, s0_ref, convp_ref, hp_ref, dexp_ref, nw_ref, expand_ref,
                y_ref, sout_ref, conv_scr, act_scr, st_scr, yd_scr, exp_scr, *, q_rows, n_chunks):
    c = pl.program_id(1)
    q = q_rows

    @pl.when(c == 0)
    def _():
        conv_scr[0:SUBLANES, :] = hist_ref[0]
        for g in range(SSM_GROUPS):
            st_scr[g] = s0_ref[0, g * GROUP_COLS:(g + 1) * GROUP_COLS, :].T

    conv_scr[SUBLANES:SUBLANES + q, 0:D_SSM] = x_ref[...]
    conv_scr[SUBLANES:SUBLANES + q, D_SSM:] = bc_ref[...]

    for c0 in range(0, CONV_DIM, GROUP_COLS):
        cols = slice(c0, c0 + GROUP_COLS)
        xp = conv_scr[:, cols]
        conv = convp_ref[CONV_WIDTH:CONV_WIDTH + 1, cols] + convp_ref[CONV_WIDTH - 1:CONV_WIDTH, cols] * xp[SUBLANES:]
        for shift in range(1, CONV_WIDTH):
            tap = CONV_WIDTH - 1 - shift
            conv = conv + convp_ref[tap:tap + 1, cols] * pltpu.roll(xp, shift, axis=0)[SUBLANES:]
        act_scr[:, cols] = _silu(conv)
    conv_scr[0:SUBLANES, :] = conv_scr[q:q + SUBLANES, :]

    v = dt_ref[...] + hp_ref[0:1, :]
    dt = jnp.maximum(v, 0.0) + jnp.log1p(jnp.exp(-jnp.abs(v)))
    a = dt * (-jnp.exp(hp_ref[1:2, :]))
    ii = lax.broadcasted_iota(jnp.int32, (q, q), 0)
    jj = lax.broadcasted_iota(jnp.int32, (q, q), 1)
    causal = ii >= jj
    acum = jnp.dot(causal.astype(F32), a, precision=lax.Precision.HIGHEST, preferred_element_type=F32)
    acum2 = acum * LOG2E
    row_t = _transpose_rows_to_lanes(acum2 - jnp.log(dt) * LOG2E)
    last = acum[q - 1:q, :]
    factors = jnp.concatenate(
        [jnp.exp(acum),
         jnp.exp(last - acum) * dt,
         jnp.broadcast_to(jnp.exp(last), (SUBLANES, LANES))], axis=0)
    exp_scr[...] = jnp.dot(_split_bf16(factors), expand_ref[...], preferred_element_type=F32)

    for g in range(SSM_GROUPS):
        cols = slice(g * GROUP_COLS, (g + 1) * GROUP_COLS)
        xact = act_scr[:, cols]
        bact = act_scr[:, D_SSM + g * D_STATE:D_SSM + (g + 1) * D_STATE].astype(BF16)
        cact = act_scr[:, D_SSM + (SSM_GROUPS + g) * D_STATE:D_SSM + (SSM_GROUPS + g + 1) * D_STATE].astype(BF16)
        cb = lax.dot_general(cact, bact, (((1,), (1,)), ((), ())), preferred_element_type=F32)
        st = st_scr[g]
        y_off = jnp.dot(cact, st.astype(BF16), preferred_element_type=F32)
        for r in range(HEADS_PER_GROUP):
            h = g * HEADS_PER_GROUP + r
            m = cb * jnp.exp2(jnp.where(causal, acum2[:, h:h + 1] - row_t[h:h + 1, :], -jnp.inf))
            xh = xact[:, r * SSM_HEADDIM:(r + 1) * SSM_HEADDIM]
            yd_scr[:, h * SSM_HEADDIM:(h + 1) * SSM_HEADDIM] = jnp.dot(
                m.astype(BF16), xh.astype(BF16), preferred_element_type=F32)
        y = yd_scr[:, cols] + y_off * exp_scr[0:q, cols] + dexp_ref[:, cols] * xact
        xw = (xact * exp_scr[q:2 * q, cols]).astype(BF16)
        upd = lax.dot_general(bact, xw, (((0,), (0,)), ((), ())), preferred_element_type=F32)
        st_scr[g] = st * exp_scr[2 * q:2 * q + 1, cols] + upd

        yg = y * _silu(z_ref[:, cols])
        rn = lax.rsqrt(jnp.mean(yg * yg, axis=-1, keepdims=True) + EPS)
        y_ref[:, cols] = (yg * rn * nw_ref[:, cols]).astype(y_ref.dtype)

    @pl.when(c == n_chunks - 1)
    def _():
        for g in range(SSM_GROUPS):
            sout_ref[0, g * GROUP_COLS:(g + 1) * GROUP_COLS, :] = st_scr[g].T


def _ssd_branch(rest, dt, hist, s0, convp, hp, dexp, ssm_nw, expand, *, n_streams, q_rows, n_chunks):
    t = rest.shape[0]

    def rows(b, c):
        return b * n_chunks + c

    def const(shape):
        return pl.BlockSpec(shape, lambda b, c: (0,) * len(shape))

    kern = functools.partial(_ssd_kernel, q_rows=q_rows, n_chunks=n_chunks)
    return pl.pallas_call(
        kern,
        grid=(n_streams, n_chunks),
        in_specs=[
            pl.BlockSpec((q_rows, D_SSM), lambda b, c: (rows(b, c), R_X // D_SSM)),
            pl.BlockSpec((q_rows, BC_COLS), lambda b, c: (rows(b, c), R_BC // BC_COLS)),
            pl.BlockSpec((q_rows, D_SSM), lambda b, c: (rows(b, c), R_Z // D_SSM)),
            pl.BlockSpec((q_rows, DT_COLS), lambda b, c: (rows(b, c), 0)),
            pl.BlockSpec((1, SUBLANES, CONV_DIM), lambda b, c: (b, 0, 0)),
            pl.BlockSpec((1, D_SSM, D_STATE), lambda b, c: (b, 0, 0)),
            const((SUBLANES, CONV_DIM)),
            const((SUBLANES, LANES)),
            const((1, D_SSM)),
            const((1, D_SSM)),
            const((N_SPLIT * LANES, D_SSM)),
        ],
        out_specs=[
            pl.BlockSpec((q_rows, D_SSM), lambda b, c: (rows(b, c), 0)),
            pl.BlockSpec((1, D_SSM, D_STATE), lambda b, c: (b, 0, 0)),
        ],
        out_shape=[
            jax.ShapeDtypeStruct((t, D_SSM), BF16),
            jax.ShapeDtypeStruct((n_streams, D_SSM, D_STATE), F32),
        ],
        scratch_shapes=[
            pltpu.VMEM((SUBLANES + q_rows, CONV_DIM), F32),
            pltpu.VMEM((q_rows, CONV_DIM), F32),
            pltpu.VMEM((SSM_GROUPS, D_STATE, GROUP_COLS), F32),
            pltpu.VMEM((q_rows, D_SSM), F32),
            pltpu.VMEM((2 * q_rows + SUBLANES, D_SSM), F32),
        ],
        compiler_params=_compiler_params(("parallel", "arbitrary")),
        name="ssd_branch",
    )(rest, rest, rest, dt, hist, s0, convp, hp, dexp, ssm_nw, expand)


def _softmax_stage(s_scr, p_scr, inv_scr, bias_ref, h, q_rows):
    exp2_scale = ATT_SCALE * LOG2E
    slab = min(SOFTMAX_SLAB, q_rows)
    bias_rows = bias_ref.shape[1]
    for r0 in range(0, q_rows, slab):
        b0 = r0 % bias_rows
        u = s_scr[h, r0:r0 + slab, :] + bias_ref[h, b0:b0 + slab, :]
        m = jnp.max(u, axis=-1, keepdims=True)
        e = jnp.exp2((u - m) * exp2_scale)
        p_scr[h, r0:r0 + slab, :] = e.astype(BF16)
        inv = 1.0 / jnp.sum(e, axis=-1, keepdims=True)
        inv_scr[h, r0:r0 + slab, :] = jnp.broadcast_to(inv, (slab, ATT_HEAD_DIM))


def _head_pipeline(qk_stage, softmax_stage, pv_stage):
    for step in range(ATT_HEADS + 2):
        if step < ATT_HEADS:
            qk_stage(step)
        if 1 <= step <= ATT_HEADS:
            softmax_stage(step - 1)
        if step >= 2:
            pv_stage(step - 2)


def _prompt_attn_kernel(*refs, n_hist_blocks):
    n_win = n_hist_blocks + 2
    tq = ATT_Q_ROWS
    q_ref, g_ref = refs[0], refs[1]
    kv_refs = refs[2:2 + n_win]
    bias_ref, o_ref, s_scr, p_scr, inv_scr = refs[2 + n_win:]
    first_block = 2 * pl.program_id(0) - n_hist_blocks

    def block_rows(blk):
        lo = 0 if blk <= n_hist_blocks else tq
        hi = 2 * tq if blk >= 1 else tq
        return lo, hi

    def qk_stage(h, mask_history):
        for blk in range(n_win):
            lo, hi = block_rows(blk)
            s = lax.dot_general(q_ref[0, h, lo:hi, :], kv_refs[blk][QKV_K, h], (((1,), (1,)), ((), ())),
                                preferred_element_type=F32)
            if mask_history and blk < n_hist_blocks:
                s = jnp.where(first_block + blk >= 0, s, MASK_VALUE)
            if lo == 0:
                s_scr[h, 0:tq, blk * tq:(blk + 1) * tq] = s[0:tq]
            if hi == 2 * tq:
                s_scr[h, tq:2 * tq, (blk - 1) * tq:blk * tq] = s[tq - lo:2 * tq - lo]

    def pv_stage(h):
        sl = slice(h * ATT_HEAD_DIM, (h + 1) * ATT_HEAD_DIM)
        o_first = jnp.zeros((tq, ATT_HEAD_DIM), F32)
        o_second = jnp.zeros((tq, ATT_HEAD_DIM), F32)
        for blk in range(n_win):
            lo, hi = block_rows(blk)
            parts = []
            if lo == 0:
                parts.append(p_scr[h, 0:tq, blk * tq:(blk + 1) * tq])
            if hi == 2 * tq:
                parts.append(p_scr[h, tq:2 * tq, (blk - 1) * tq:blk * tq])
            p = parts[0] if len(parts) == 1 else jnp.concatenate(parts, axis=0)
            o = jnp.dot(p, kv_refs[blk][QKV_V, h], preferred_element_type=F32)
            if lo == 0:
                o_first = o_first + o[0:tq]
            if hi == 2 * tq:
                o_second = o_second + o[tq - lo:2 * tq - lo]
        o = jnp.concatenate([o_first, o_second], axis=0)
        o_ref[:, sl] = (o * inv_scr[h] * _silu(g_ref[:, sl])).astype(o_ref.dtype)

    softmax_stage = functools.partial(_softmax_stage, s_scr, p_scr, inv_scr, bias_ref, q_rows=2 * tq)

    @pl.when(first_block < 0)
    def _():
        _head_pipeline(functools.partial(qk_stage, mask_history=True), softmax_stage, pv_stage)

    @pl.when(first_block >= 0)
    def _():
        _head_pipeline(functools.partial(qk_stage, mask_history=False), softmax_stage, pv_stage)


def _prompt_attention(qkv, rest, bias):
    t = rest.shape[0]
    tq = ATT_Q_ROWS
    n_hist_blocks = BAND_ROWS // tq
    n_win = n_hist_blocks + 2
    span = BAND_ROWS + tq
    assert (QKV_K, QKV_V) == (0, 1)
    kv_block = (2, ATT_HEADS, tq, ATT_HEAD_DIM)

    def kv_spec(blk):
        return pl.BlockSpec(kv_block, lambda i: (0, 0, jnp.maximum(2 * i - n_hist_blocks + blk, 0), 0))

    in_specs = [pl.BlockSpec((1, ATT_HEADS, 2 * tq, ATT_HEAD_DIM), lambda i: (QKV_Q, 0, i, 0)),
                pl.BlockSpec((2 * tq, D_ATT), lambda i: (i, R_G // D_ATT))]
    in_specs += [kv_spec(blk) for blk in range(n_win)]
    in_specs += [pl.BlockSpec((ATT_HEADS, tq, span), lambda i: (0, 0, 0))]
    kern = functools.partial(_prompt_attn_kernel, n_hist_blocks=n_hist_blocks)
    return pl.pallas_call(
        kern,
        grid=(t // (2 * tq),),
        in_specs=in_specs,
        out_specs=pl.BlockSpec((2 * tq, D_ATT), lambda i: (i, 0)),
        out_shape=jax.ShapeDtypeStruct((t, D_ATT), BF16),
        scratch_shapes=[pltpu.VMEM((ATT_HEADS, 2 * tq, span), F32),
                        pltpu.VMEM((ATT_HEADS, 2 * tq, span), BF16),
                        pltpu.VMEM((ATT_HEADS, 2 * tq, ATT_HEAD_DIM), F32)],
        compiler_params=_compiler_params(("parallel",)),
        name="prompt_attention",
    )(qkv, rest, *([qkv] * n_win), bias)


def _sample_attn_kernel(q_ref, kn_ref, vn_ref, g_ref, kc_lo, kc_hi, vc_lo, vc_hi, bias_ref, o_ref,
                        s_scr, p_scr, inv_scr, *, q_rows):
    new_pad = LANES - q_rows
    zpad = jnp.zeros((new_pad, ATT_HEAD_DIM), BF16)

    def cached(lo_ref, hi_ref, h):
        ref = (lo_ref if h < SUBLANES else hi_ref).reshape(BAND_ROWS * SUBLANES, ATT_HEAD_DIM)
        return ref[pl.ds(h % SUBLANES, BAND_ROWS, stride=SUBLANES), :].astype(BF16)

    def qk_stage(h):
        qh = q_ref[0, h]
        s_scr[h, :, 0:BAND_ROWS] = lax.dot_general(qh, cached(kc_lo, kc_hi, h), (((1,), (1,)), ((), ())),
                                                   preferred_element_type=F32)
        k_new = jnp.concatenate([kn_ref[0, h], zpad], axis=0)
        s_scr[h, :, BAND_ROWS:] = lax.dot_general(qh, k_new, (((1,), (1,)), ((), ())), preferred_element_type=F32)

    def pv_stage(h):
        sl = slice(h * ATT_HEAD_DIM, (h + 1) * ATT_HEAD_DIM)
        v_new = jnp.concatenate([vn_ref[0, h], zpad], axis=0)
        o = jnp.dot(p_scr[h, :, 0:BAND_ROWS], cached(vc_lo, vc_hi, h), preferred_element_type=F32)
        o = o + jnp.dot(p_scr[h, :, BAND_ROWS:], v_new, preferred_element_type=F32)
        o_ref[:, sl] = (o * inv_scr[h] * _silu(g_ref[:, sl])).astype(o_ref.dtype)

    _head_pipeline(qk_stage, functools.partial(_softmax_stage, s_scr, p_scr, inv_scr, bias_ref, q_rows=q_rows),
                   pv_stage)


def _sample_attention(qkv, rest, cache_k, cache_v, bias, *, n_streams, q_rows):
    t = rest.shape[0]
    span_pad = BAND_ROWS + LANES
    kern = functools.partial(_sample_attn_kernel, q_rows=q_rows)

    def new_spec(which):
        return pl.BlockSpec((1, ATT_HEADS, q_rows, ATT_HEAD_DIM), lambda b: (which, 0, b, 0))

    def cache_spec(half):
        return pl.BlockSpec((None, BAND_ROWS, None, SUBLANES, ATT_HEAD_DIM), lambda b: (b, 0, half, 0, 0))

    return pl.pallas_call(
        kern,
        grid=(n_streams,),
        in_specs=[
            new_spec(QKV_Q), new_spec(QKV_K), new_spec(QKV_V),
            pl.BlockSpec((q_rows, D_ATT), lambda b: (b, R_G // D_ATT)),
            cache_spec(0), cache_spec(1), cache_spec(0), cache_spec(1),
            pl.BlockSpec((ATT_HEADS, q_rows, span_pad), lambda b: (0, 0, 0)),
        ],
        out_specs=pl.BlockSpec((q_rows, D_ATT), lambda b: (b, 0)),
        out_shape=jax.ShapeDtypeStruct((t, D_ATT), BF16),
        scratch_shapes=[pltpu.VMEM((ATT_HEADS, q_rows, span_pad), F32),
                        pltpu.VMEM((ATT_HEADS, q_rows, span_pad), BF16),
                        pltpu.VMEM((ATT_HEADS, q_rows, ATT_HEAD_DIM), F32)],
        compiler_params=_compiler_params(("parallel",)),
        name="sample_attention",
    )(qkv, qkv, qkv, rest, cache_k, cache_k, cache_v, cache_v, bias)


def _outproj_kernel(ys_ref, ya_ref, w1_ref, w2_ref, x_ref, o_ref):
    acc = jnp.dot(ys_ref[...], w1_ref[...], preferred_element_type=F32)
    acc = acc + jnp.dot(ya_ref[...], w2_ref[...], preferred_element_type=F32)
    o_ref[...] = x_ref[...] + acc


def _out_projection(y_ssm, y_att, w_out, x, *, tm, tn):
    t = x.shape[0]
    return pl.pallas_call(
        _outproj_kernel,
        grid=(t // tm, D_MODEL // tn),
        in_specs=[
            pl.BlockSpec((tm, D_SSM), lambda i, j: (i, 0)),
            pl.BlockSpec((tm, D_ATT), lambda i, j: (i, 0)),
            pl.BlockSpec((D_SSM, tn), lambda i, j: (0, j)),
            pl.BlockSpec((D_ATT, tn), lambda i, j: (D_SSM // D_ATT, j)),
            pl.BlockSpec((tm, tn), lambda i, j: (i, j)),
        ],
        out_specs=pl.BlockSpec((tm, tn), lambda i, j: (i, j)),
        out_shape=jax.ShapeDtypeStruct((t, D_MODEL), F32),
        compiler_params=_compiler_params(("parallel", "arbitrary")),
        name="out_projection",
    )(y_ssm, y_att, w_out, w_out, x)


def _pad_to(v, size, axis):
    pad = [(0, 0)] * v.ndim
    pad[axis] = (0, size - v.shape[axis])
    return jnp.pad(v, pad)


def _cast_w_in_kernel(main_ref, next_ref, wa_ref, wq_ref, wdt_ref, *, n_a, row_shift):
    j = pl.program_id(0)
    tn = main_ref.shape[0]
    chunk = 256

    @pl.when(j == n_a)
    def _():
        row_id = lax.broadcasted_iota(jnp.int32, (DT_COLS, main_ref.shape[1]), 0)
        wdt_ref[...] = jnp.where(row_id < row_shift, main_ref[0:DT_COLS, :], 0.0).T.astype(BF16)

    def put(rows_of, out_ref):
        for r0 in range(0, tn, chunk):
            out_ref[:, r0:r0 + chunk] = rows_of(r0).T.astype(BF16)

    @pl.when(j < n_a)
    def _():
        put(lambda r0: main_ref[r0:r0 + chunk, :], wa_ref)

    @pl.when(j >= n_a)
    def _():
        def shifted(r0):
            lo = r0 + row_shift
            if lo + chunk <= tn:
                return main_ref[lo:lo + chunk, :]
            return jnp.concatenate([main_ref[lo:tn, :], next_ref[0:lo + chunk - tn, :]], axis=0)

        put(shifted, wq_ref)


def _cast_w_in(w_in_t):
    tn = PROJ_TN
    n_a = OFF_DT // tn
    n_q = (4 * D_ATT) // tn
    row_shift = OFF_Q - OFF_DT
    assert OFF_DT % tn == 0 and row_shift % SUBLANES == 0 and 0 < row_shift < LANES
    kern = functools.partial(_cast_w_in_kernel, n_a=n_a, row_shift=row_shift)
    return pl.pallas_call(
        kern,
        grid=(n_a + n_q,),
        in_specs=[
            pl.BlockSpec((tn, D_MODEL), lambda j: (j, 0)),
            pl.BlockSpec((LANES, D_MODEL), lambda j: ((j + 1) * (tn // LANES), 0)),
        ],
        out_specs=[
            pl.BlockSpec((D_MODEL, tn), lambda j: (0, jnp.minimum(j, n_a - 1))),
            pl.BlockSpec((D_MODEL, tn), lambda j: (0, jnp.maximum(j - n_a, 0))),
            pl.BlockSpec((D_MODEL, DT_COLS), lambda j: (0, 0)),
        ],
        out_shape=[
            jax.ShapeDtypeStruct((D_MODEL, n_a * tn), BF16),
            jax.ShapeDtypeStruct((D_MODEL, n_q * tn), BF16),
            jax.ShapeDtypeStruct((D_MODEL, DT_COLS), BF16),
        ],
        compiler_params=_compiler_params(("arbitrary",)),
        name="cast_w_in",
    )(w_in_t, w_in_t)


def _prepare_params(norm_w, w_in, conv_w, conv_b, dt_bias, a_log, d_skip, ssm_norm_w, q_norm_w, k_norm_w, w_out):
    w_a, w_qkvg, w_dt = _cast_w_in(w_in.T)
    qk_w = jnp.concatenate([jnp.tile(q_norm_w, ATT_HEADS), jnp.tile(k_norm_w, ATT_HEADS)]).reshape(1, 2 * D_ATT)
    convp = _pad_to(jnp.concatenate([conv_w, conv_b[None]], axis=0), SUBLANES, 0)
    hp = _pad_to(_pad_to(jnp.stack([dt_bias, a_log]), LANES, 1), SUBLANES, 0)
    dexp = jnp.repeat(d_skip, SSM_HEADDIM).reshape(1, D_SSM)
    head_of_col = np.arange(D_SSM) // SSM_HEADDIM
    expand = (np.arange(LANES)[:, None] == head_of_col[None, :]).astype(np.float32)
    expand = jnp.asarray(np.tile(expand, (N_SPLIT, 1)), dtype=BF16)
    return dict(norm_w=norm_w.reshape(1, D_MODEL), w_qkvg=w_qkvg, w_a=w_a, w_dt=w_dt, qk_w=qk_w,
                convp=convp, hp=hp, dexp=dexp, ssm_nw=ssm_norm_w.reshape(1, D_SSM), expand=expand,
                w_out=w_out.astype(BF16))


def _rel_bias_table(rel_bias, q_rows, hist_rows, span_pad, band_chunk):
    n_heads = rel_bias.shape[0]
    period = span_pad + q_rows
    k = np.arange(period)
    rel = np.clip(hist_rows + q_rows - 1 - k, -REL_CLIP, REL_CLIP) + REL_CLIP
    onehot = jnp.asarray(np.eye(N_REL, dtype=np.float32)[:, rel])
    diag_row = jnp.dot(rel_bias.astype(F32), onehot, precision=lax.Precision.HIGHEST)
    flat = jnp.tile(diag_row, (1, q_rows))[:, :q_rows * (period - 1)]
    table = flat.reshape(n_heads, q_rows, period - 1)[:, :, q_rows - 1:q_rows - 1 + span_pad]
    i_idx = np.arange(q_rows)[:, None]
    j_idx = np.arange(span_pad)[None, :]
    valid = j_idx < hist_rows + q_rows
    if band_chunk is not None:
        start = (i_idx // band_chunk) * band_chunk
        valid = valid & (j_idx >= start) & (j_idx < start + hist_rows + band_chunk)
    return jnp.where(jnp.asarray(np.broadcast_to(valid, (q_rows, span_pad)))[None], table / ATT_SCALE, MASK_VALUE)


def _layer(x, hist, s0, cache_k, cache_v, prm, rel_bias, *, n_streams, seq, prompt, proj_tm):
    t = n_streams * seq
    x2 = x.reshape(t, D_MODEL)
    proj = _in_projection(x2, prm["norm_w"], prm["w_qkvg"], prm["w_a"], prm["w_dt"], prm["qk_w"],
                          tm=proj_tm, emit_kv=not prompt)
    if prompt:
        qkv, rest, dt = proj
        q_rows, n_chunks = SSD_ROWS, seq // SSD_ROWS
        kv_rows = BAND_ROWS
        kv_new = _kv_rows(x2, prm["norm_w"], prm["w_qkvg"], prm["qk_w"], first_row=t - kv_rows, n_rows=kv_rows,
                          tm=512)
    else:
        qkv, rest, dt, kv_new = proj
        q_rows, n_chunks = seq, 1
        kv_rows = t
    y_ssm, s_new = _ssd_branch(rest, dt, hist, s0, prm["convp"], prm["hp"], prm["dexp"], prm["ssm_nw"],
                               prm["expand"], n_streams=n_streams, q_rows=q_rows, n_chunks=n_chunks)
    if prompt:
        bias = _rel_bias_table(rel_bias, ATT_Q_ROWS, BAND_ROWS, BAND_ROWS + ATT_Q_ROWS, CHUNK)
        y_att = _prompt_attention(qkv, rest, bias)
    else:
        bias = _rel_bias_table(rel_bias, seq, BAND_ROWS, BAND_ROWS + LANES, None)
        y_att = _sample_attention(qkv, rest, cache_k, cache_v, bias, n_streams=n_streams, q_rows=seq)
    y = _out_projection(y_ssm, y_att, prm["w_out"], x2, tm=1024, tn=1024)
    new_conv = rest.reshape(n_streams, seq, R_COLS)[:, seq - (CONV_WIDTH - 1):, R_X:]
    kv_streams = kv_rows // n_streams
    kh = kv_new[:, :D_ATT].reshape(n_streams, kv_streams, ATT_HEADS, ATT_HEAD_DIM)
    vh = kv_new[:, D_ATT:].reshape(n_streams, kv_streams, ATT_HEADS, ATT_HEAD_DIM)
    new_ssm = s_new.reshape(n_streams, SSM_HEADS, SSM_HEADDIM, D_STATE)
    return y.reshape(n_streams, seq, D_MODEL), new_conv, new_ssm, kh, vh


def kernel(x_prompt, x_sample, state_conv, state_ssm, cache_k, cache_v, norm_w, w_in, conv_w, conv_b, dt_bias, a_log, d_skip, ssm_norm_w, q_norm_w, k_norm_w, rel_bias, w_out):
    bp, lp, _ = x_prompt.shape
    bs, ls, _ = x_sample.shape
    assert bp == 1 and lp % 1024 == 0 and lp >= BAND_ROWS
    assert ls % (2 * SUBLANES) == 0 and ls <= LANES and (bs * ls) % 1024 == 0 and cache_k.shape[2] == BAND_ROWS
    assert norm_w.shape[0] == 1
    prm = _prepare_params(norm_w[0], w_in[0], conv_w[0], conv_b[0], dt_bias[0], a_log[0], d_skip[0],
                          ssm_norm_w[0], q_norm_w[0], k_norm_w[0], w_out[0])
    rb = rel_bias[0]

    zero_hist = jnp.zeros((bp, SUBLANES, CONV_DIM), F32)
    zero_state = jnp.zeros((bp, D_SSM, D_STATE), F32)
    yp, c1, s1, k1, v1 = _layer(x_prompt, zero_hist, zero_state, None, None, prm, rb,
                                n_streams=bp, seq=lp, prompt=True, proj_tm=1024)

    hist = jnp.pad(state_conv[0], ((0, 0), (SUBLANES - (CONV_WIDTH - 1), 0), (0, 0)))
    s0 = state_ssm[0].reshape(bs, D_SSM, D_STATE)
    half_heads = (bs, BAND_ROWS, ATT_HEADS // SUBLANES, SUBLANES, ATT_HEAD_DIM)
    ck = cache_k[0].reshape(half_heads)
    cv = cache_v[0].reshape(half_heads)
    ys, c2, s2, k2, v2 = _layer(x_sample, hist, s0, ck, cv, prm, rb, n_streams=bs, seq=ls, prompt=False,
                                proj_tm=512)

    return (yp, ys, c1[None], s1[None], k1[None], v1[None], c2[None], s2[None], k2[None], v2[None])
```

```python
import functools

import jax
import jax.numpy as jnp
import numpy as np
from jax import lax
from jax.experimental import pallas as pl
from jax.experimental.pallas import tpu as pltpu

F32 = jnp.float32
BF16 = jnp.bfloat16

D_MODEL = 2048
CHUNK = 64
LEFT_CHUNKS = 8
BAND_ROWS = LEFT_CHUNKS * CHUNK
D_SSM = 2048
D_ATT = 2048
SSM_HEADDIM = 64
SSM_HEADS = 32
SSM_GROUPS = 4
HEADS_PER_GROUP = SSM_HEADS // SSM_GROUPS
GROUP_COLS = HEADS_PER_GROUP * SSM_HEADDIM
D_STATE = 128
BC_COLS = 2 * SSM_GROUPS * D_STATE
CONV_WIDTH = 4
CONV_DIM = D_SSM + BC_COLS
ATT_HEAD_DIM = 128
ATT_HEADS = 16
REL_CLIP = 128
N_REL = 2 * REL_CLIP + 1
EPS = 1e-6
OFF_Z = 0
OFF_XBC = OFF_Z + D_SSM
OFF_DT = OFF_XBC + CONV_DIM
OFF_Q = OFF_DT + SSM_HEADS
OFF_K = OFF_Q + D_ATT
OFF_V = OFF_K + D_ATT
OFF_G = OFF_V + D_ATT

LANES = 128
SUBLANES = 8
VMEM_LIMIT_BYTES = 56 * 1024 * 1024

R_G = 0
R_Z = R_G + D_ATT
R_X = R_Z + D_SSM
R_BC = R_X + D_SSM
R_COLS = R_BC + BC_COLS
DT_COLS = LANES
QKV_K, QKV_V, QKV_Q = 0, 1, 2
N_SPLIT = 3

PROJ_TN = 1024
SSD_ROWS = 128
ATT_Q_ROWS = 128
KV_RING_SLOTS = 8
SOFTMAX_SLAB = 32
MASK_VALUE = -1e30
ATT_SCALE = ATT_HEAD_DIM ** -0.5
LOG2E = 1.4426950408889634


def _silu(v):
    return v * (1.0 / (1.0 + jnp.exp2(v * (-LOG2E))))


def _compiler_params(semantics):
    return pltpu.CompilerParams(dimension_semantics=semantics, vmem_limit_bytes=VMEM_LIMIT_BYTES)


def _normed_rows(x_ref, nw_ref):
    x = x_ref[...]
    ms = jnp.mean(x * x, axis=-1, keepdims=True)
    return (x * lax.rsqrt(ms + EPS) * nw_ref[...]).astype(BF16)


def _head_norm(a, use_norm, w):
    r = lax.rsqrt(jnp.mean(a * a, axis=-1, keepdims=True) + EPS)
    return a * jnp.where(use_norm, r, 1.0) * jnp.where(use_norm, w, 1.0)


def _inproj_kernel(*refs, n_qk, n_qkv, n_g, emit_kv):
    x_ref, nw_ref, wqkvg_ref, wa_ref, wdt_ref, qkw_ref, qkv_ref, rest_ref, dt_ref = refs[:9]
    kv_ref = refs[9] if emit_kv else None
    h_scr = refs[-1]
    j = pl.program_id(1)
    heads_per_tile = PROJ_TN // ATT_HEAD_DIM

    @pl.when(j == 0)
    def _():
        h = _normed_rows(x_ref, nw_ref)
        h_scr[...] = h
        dt_ref[...] = jnp.dot(h, wdt_ref[...], preferred_element_type=F32)

    def qkv_tile(use_norm):
        acc = jnp.dot(h_scr[...], wqkvg_ref[...], preferred_element_type=F32)
        for hh in range(heads_per_tile):
            sl = slice(hh * ATT_HEAD_DIM, (hh + 1) * ATT_HEAD_DIM)
            head = acc[:, sl]
            if use_norm:
                r = lax.rsqrt(jnp.mean(head * head, axis=-1, keepdims=True) + EPS)
                head = head * r * qkw_ref[:, sl]
            qkv_ref[0, hh] = head.astype(BF16)
            if emit_kv:
                kv_ref[:, sl] = head

    pl.when(j < n_qk)(functools.partial(qkv_tile, True))
    pl.when(jnp.logical_and(j >= n_qk, j < n_qkv))(functools.partial(qkv_tile, False))

    @pl.when(jnp.logical_and(j >= n_qkv, j < n_qkv + n_g))
    def _():
        rest_ref[...] = jnp.dot(h_scr[...], wqkvg_ref[...], preferred_element_type=F32)

    @pl.when(j >= n_qkv + n_g)
    def _():
        rest_ref[...] = jnp.dot(h_scr[...], wa_ref[...], preferred_element_type=F32)


def _in_projection(x, norm_w, w_qkvg, w_a, w_dt, qk_w, *, tm, emit_kv):
    t = x.shape[0]
    tn = PROJ_TN
    tiles_per_proj = D_ATT // tn
    n_qk, n_qkv, n_g = 2 * tiles_per_proj, 3 * tiles_per_proj, tiles_per_proj
    n_a = (D_SSM + CONV_DIM) // tn
    heads_per_tile = tn // ATT_HEAD_DIM
    grid = (t // tm, n_qkv + n_g + n_a)
    kern = functools.partial(_inproj_kernel, n_qk=n_qk, n_qkv=n_qkv, n_g=n_g, emit_kv=emit_kv)

    def qkv_index(i, j):
        jj = jnp.minimum(j, n_qkv - 1)
        plane = (jj // tiles_per_proj + QKV_Q) % 3
        return (plane, jj % tiles_per_proj, i, 0)

    out_specs = [
        pl.BlockSpec((1, heads_per_tile, tm, ATT_HEAD_DIM), qkv_index),
        pl.BlockSpec((tm, tn), lambda i, j: (i, jnp.maximum(j - n_qkv, 0))),
        pl.BlockSpec((tm, DT_COLS), lambda i, j: (i, 0)),
    ]
    out_shape = [
        jax.ShapeDtypeStruct((3, ATT_HEADS, t, ATT_HEAD_DIM), BF16),
        jax.ShapeDtypeStruct((t, R_COLS), F32),
        jax.ShapeDtypeStruct((t, DT_COLS), F32),
    ]
    if emit_kv:
        n_kv = n_qkv - tiles_per_proj
        out_specs.append(pl.BlockSpec((tm, tn), lambda i, j: (i, jnp.clip(j - tiles_per_proj, 0, n_kv - 1))))
        out_shape.append(jax.ShapeDtypeStruct((t, 2 * D_ATT), F32))
    return pl.pallas_call(
        kern,
        grid=grid,
        in_specs=[
            pl.BlockSpec((tm, D_MODEL), lambda i, j: (i, 0)),
            pl.BlockSpec((1, D_MODEL), lambda i, j: (0, 0)),
            pl.BlockSpec((D_MODEL, tn), lambda i, j: (0, jnp.minimum(j, n_qkv + n_g - 1))),
            pl.BlockSpec((D_MODEL, tn), lambda i, j: (0, jnp.clip(j - n_qkv - n_g, 0, n_a - 1))),
            pl.BlockSpec((D_MODEL, DT_COLS), lambda i, j: (0, 0)),
            pl.BlockSpec((1, tn), lambda i, j: (0, jnp.minimum(j, n_qk - 1))),
        ],
        out_specs=out_specs,
        out_shape=out_shape,
        scratch_shapes=[pltpu.VMEM((tm, D_MODEL), BF16)],
        compiler_params=_compiler_params(("parallel", "arbitrary")),
        name="in_projection",
    )(x, norm_w, w_qkvg, w_a, w_dt, qk_w)


def _kv_rows_kernel(x_ref, nw_ref, w_ref, qkw_ref, o_ref, h_scr, *, n_k):
    j = pl.program_id(1)

    @pl.when(j == 0)
    def _():
        h_scr[...] = _normed_rows(x_ref, nw_ref)

    acc = jnp.dot(h_scr[...], w_ref[...], preferred_element_type=F32)
    for hh in range(PROJ_TN // ATT_HEAD_DIM):
        sl = slice(hh * ATT_HEAD_DIM, (hh + 1) * ATT_HEAD_DIM)
        o_ref[:, sl] = _head_norm(acc[:, sl], j < n_k, qkw_ref[:, sl])


def _kv_rows(x, norm_w, w_qkvg, qk_w, *, first_row, n_rows, tm):
    tn = PROJ_TN
    tiles_per_proj = D_ATT // tn
    row0 = first_row // tm
    kern = functools.partial(_kv_rows_kernel, n_k=tiles_per_proj)
    return pl.pallas_call(
        kern,
        grid=(n_rows // tm, 2 * tiles_per_proj),
        in_specs=[
            pl.BlockSpec((tm, D_MODEL), lambda i, j: (row0 + i, 0)),
            pl.BlockSpec((1, D_MODEL), lambda i, j: (0, 0)),
            pl.BlockSpec((D_MODEL, tn), lambda i, j: (0, tiles_per_proj + j)),
            pl.BlockSpec((1, tn), lambda i, j: (0, jnp.minimum(tiles_per_proj + j, 2 * tiles_per_proj - 1))),
        ],
        out_specs=pl.BlockSpec((tm, tn), lambda i, j: (i, j)),
        out_shape=jax.ShapeDtypeStruct((n_rows, 2 * D_ATT), F32),
        scratch_shapes=[pltpu.VMEM((tm, D_MODEL), BF16)],
        compiler_params=_compiler_params(("parallel", "arbitrary")),
        name="kv_rows",
    )(x, norm_w, w_qkvg, qk_w)


def _transpose_rows_to_lanes(v):
    q = v.shape[0]
    if q < LANES:
        v = jnp.concatenate([v, jnp.zeros((LANES - q, LANES), v.dtype)], axis=0)
    return v.T[:, 0:q]


def _split_bf16(v):
    pieces = []
    rem = v
    for _ in range(N_SPLIT):
        piece = rem.astype(BF16)
        pieces.append(piece)
        rem = rem - piece.astype(F32)
    return jnp.concatenate(pieces, axis=1)


def _ssd_kernel(x_ref, bc_ref, z_ref, dt_ref, hist_ref, s0_ref, convp_ref, hp_ref, dexp_ref, nw_ref, expand_ref,
                y_ref, sout_ref, conv_scr, act_scr, st_scr, yd_scr, exp_scr, *, q_rows, n_chunks):
    c = pl.program_id(1)
    q = q_rows

    @pl.when(c == 0)
    def _():
        conv_scr[0:SUBLANES, :] = hist_ref[0]
        for g in range(SSM_GROUPS):
            st_scr[g] = s0_ref[0, g * GROUP_COLS:(g + 1) * GROUP_COLS, :].T

    conv_scr[SUBLANES:SUBLANES + q, 0:D_SSM] = x_ref[...]
    conv_scr[SUBLANES:SUBLANES + q, D_SSM:] = bc_ref[...]

    for c0 in range(0, CONV_DIM, GROUP_COLS):
        cols = slice(c0, c0 + GROUP_COLS)
        xp = conv_scr[:, cols]
        conv = convp_ref[CONV_WIDTH:CONV_WIDTH + 1, cols] + convp_ref[CONV_WIDTH - 1:CONV_WIDTH, cols] * xp[SUBLANES:]
        for shift in range(1, CONV_WIDTH):
            tap = CONV_WIDTH - 1 - shift
            conv = conv + convp_ref[tap:tap + 1, cols] * pltpu.roll(xp, shift, axis=0)[SUBLANES:]
        act_scr[:, cols] = _silu(conv)
    conv_scr[0:SUBLANES, :] = conv_scr[q:q + SUBLANES, :]

    v = dt_ref[...] + hp_ref[0:1, :]
    dt = jnp.maximum(v, 0.0) + jnp.log1p(jnp.exp(-jnp.abs(v)))
    a = dt * (-jnp.exp(hp_ref[1:2, :]))
    ii = lax.broadcasted_iota(jnp.int32, (q, q), 0)
    jj = lax.broadcasted_iota(jnp.int32, (q, q), 1)
    causal = ii >= jj
    acum = jnp.dot(causal.astype(F32), a, precision=lax.Precision.HIGHEST, preferred_element_type=F32)
    acum2 = acum * LOG2E
    row_t = _transpose_rows_to_lanes(acum2 - jnp.log(dt) * LOG2E)
    last = acum[q - 1:q, :]
    factors = jnp.concatenate(
        [jnp.exp(acum),
         jnp.exp(last - acum) * dt,
         jnp.broadcast_to(jnp.exp(last), (SUBLANES, LANES))], axis=0)
    exp_scr[...] = jnp.dot(_split_bf16(factors), expand_ref[...], preferred_element_type=F32)

    for g in range(SSM_GROUPS):
        cols = slice(g * GROUP_COLS, (g + 1) * GROUP_COLS)
        xact = act_scr[:, cols]
        bact = act_scr[:, D_SSM + g * D_STATE:D_SSM + (g + 1) * D_STATE].astype(BF16)
        cact = act_scr[:, D_SSM + (SSM_GROUPS + g) * D_STATE:D_SSM + (SSM_GROUPS + g + 1) * D_STATE].astype(BF16)
        cb = lax.dot_general(cact, bact, (((1,), (1,)), ((), ())), preferred_element_type=F32)
        st = st_scr[g]
        y_off = jnp.dot(cact, st.astype(BF16), preferred_element_type=F32)
        for r in range(HEADS_PER_GROUP):
            h = g * HEADS_PER_GROUP + r
            m = cb * jnp.exp2(jnp.where(causal, acum2[:, h:h + 1] - row_t[h:h + 1, :], -jnp.inf))
            xh = xact[:, r * SSM_HEADDIM:(r + 1) * SSM_HEADDIM]
            yd_scr[:, h * SSM_HEADDIM:(h + 1) * SSM_HEADDIM] = jnp.dot(
                m.astype(BF16), xh.astype(BF16), preferred_element_type=F32)
        y = yd_scr[:, cols] + y_off * exp_scr[0:q, cols] + dexp_ref[:, cols] * xact
        xw = (xact * exp_scr[q:2 * q, cols]).astype(BF16)
        upd = lax.dot_general(bact, xw, (((0,), (0,)), ((), ())), preferred_element_type=F32)
        st_scr[g] = st * exp_scr[2 * q:2 * q + 1, cols] + upd

        yg = y * _silu(z_ref[:, cols])
        rn = lax.rsqrt(jnp.mean(yg * yg, axis=-1, keepdims=True) + EPS)
        y_ref[:, cols] = (yg * rn * nw_ref[:, cols]).astype(y_ref.dtype)

    @pl.when(c == n_chunks - 1)
    def _():
        for g in range(SSM_GROUPS):
            sout_ref[0, g * GROUP_COLS:(g + 1) * GROUP_COLS, :] = st_scr[g].T


def _ssd_branch(rest, dt, hist, s0, convp, hp, dexp, ssm_nw, expand, *, n_streams, q_rows, n_chunks):
    t = rest.shape[0]

    def rows(b, c):
        return b * n_chunks + c

    def const(shape):
        return pl.BlockSpec(shape, lambda b, c: (0,) * len(shape))

    kern = functools.partial(_ssd_kernel, q_rows=q_rows, n_chunks=n_chunks)
    return pl.pallas_call(
        kern,
        grid=(n_streams, n_chunks),
        in_specs=[
            pl.BlockSpec((q_rows, D_SSM), lambda b, c: (rows(b, c), R_X // D_SSM)),
            pl.BlockSpec((q_rows, BC_COLS), lambda b, c: (rows(b, c), R_BC // BC_COLS)),
            pl.BlockSpec((q_rows, D_SSM), lambda b, c: (rows(b, c), R_Z // D_SSM)),
            pl.BlockSpec((q_rows, DT_COLS), lambda b, c: (rows(b, c), 0)),
            pl.BlockSpec((1, SUBLANES, CONV_DIM), lambda b, c: (b, 0, 0)),
            pl.BlockSpec((1, D_SSM, D_STATE), lambda b, c: (b, 0, 0)),
            const((SUBLANES, CONV_DIM)),
            const((SUBLANES, LANES)),
            const((1, D_SSM)),
            const((1, D_SSM)),
            const((N_SPLIT * LANES, D_SSM)),
        ],
        out_specs=[
            pl.BlockSpec((q_rows, D_SSM), lambda b, c: (rows(b, c), 0)),
            pl.BlockSpec((1, D_SSM, D_STATE), lambda b, c: (b, 0, 0)),
        ],
        out_shape=[
            jax.ShapeDtypeStruct((t, D_SSM), BF16),
            jax.ShapeDtypeStruct((n_streams, D_SSM, D_STATE), F32),
        ],
        scratch_shapes=[
            pltpu.VMEM((SUBLANES + q_rows, CONV_DIM), F32),
            pltpu.VMEM((q_rows, CONV_DIM), F32),
            pltpu.VMEM((SSM_GROUPS, D_STATE, GROUP_COLS), F32),
            pltpu.VMEM((q_rows, D_SSM), F32),
            pltpu.VMEM((2 * q_rows + SUBLANES, D_SSM), F32),
        ],
        compiler_params=_compiler_params(("parallel", "arbitrary")),
        name="ssd_branch",
    )(rest, rest, rest, dt, hist, s0, convp, hp, dexp, ssm_nw, expand)


def _softmax_stage(s_scr, p_scr, inv_scr, bias_ref, h, q_rows):
    exp2_scale = ATT_SCALE * LOG2E
    slab = min(SOFTMAX_SLAB, q_rows)
    bias_rows = bias_ref.shape[1]
    for r0 in range(0, q_rows, slab):
        b0 = r0 % bias_rows
        u = s_scr[h, r0:r0 + slab, :] + bias_ref[h, b0:b0 + slab, :]
        m = jnp.max(u, axis=-1, keepdims=True)
        e = jnp.exp2((u - m) * exp2_scale)
        p_scr[h, r0:r0 + slab, :] = e.astype(BF16)
        inv = 1.0 / jnp.sum(e, axis=-1, keepdims=True)
        inv_scr[h, r0:r0 + slab, :] = jnp.broadcast_to(inv, (slab, ATT_HEAD_DIM))


def _head_pipeline(qk_stage, softmax_stage, pv_stage):
    for step in range(ATT_HEADS + 2):
        if step < ATT_HEADS:
            qk_stage(step)
        if 1 <= step <= ATT_HEADS:
            softmax_stage(step - 1)
        if step >= 2:
            pv_stage(step - 2)


def _prompt_attn_kernel(*refs, n_hist_blocks):
    n_win = n_hist_blocks + 2
    tq = ATT_Q_ROWS
    q_ref, g_ref, kv_hbm, bias_ref, o_ref, s_scr, p_scr, inv_scr, kv_ring, kv_sems = refs
    step = pl.program_id(0)
    first_block = 2 * step - n_hist_blocks

    def kv_copy(block):
        slot = jnp.bitwise_and(block, KV_RING_SLOTS - 1)
        return pltpu.make_async_copy(kv_hbm.at[pl.ds(0, 2), :, pl.ds(block * tq, tq), :], kv_ring.at[slot],
                                     kv_sems.at[slot])

    @pl.when(step == 0)
    def _():
        first_slot = KV_RING_SLOTS - n_hist_blocks
        kv_ring[first_slot:] = jnp.zeros((n_hist_blocks,) + tuple(kv_ring.shape[1:]), kv_ring.dtype)
        kv_copy(0).start()
        kv_copy(1).start()

    @pl.when(step + 1 < pl.num_programs(0))
    def _():
        kv_copy(2 * step + 2).start()
        kv_copy(2 * step + 3).start()

    kv_copy(2 * step).wait()
    kv_copy(2 * step + 1).wait()

    def kv_block(blk, plane, h):
        slot = jnp.bitwise_and(first_block + blk + KV_RING_SLOTS, KV_RING_SLOTS - 1)
        return kv_ring[slot, plane, h]

    def block_rows(blk):
        lo = 0 if blk <= n_hist_blocks else tq
        hi = 2 * tq if blk >= 1 else tq
        return lo, hi

    def qk_stage(h, mask_history):
        for blk in range(n_win):
            lo, hi = block_rows(blk)
            s = lax.dot_general(q_ref[0, h, lo:hi, :], kv_block(blk, QKV_K, h), (((1,), (1,)), ((), ())),
                                preferred_element_type=F32)
            if mask_history and blk < n_hist_blocks:
                s = jnp.where(first_block + blk >= 0, s, MASK_VALUE)
            if lo == 0:
                s_scr[h, 0:tq, blk * tq:(blk + 1) * tq] = s[0:tq]
            if hi == 2 * tq:
                s_scr[h, tq:2 * tq, (blk - 1) * tq:blk * tq] = s[tq - lo:2 * tq - lo]

    def pv_stage(h):
        sl = slice(h * ATT_HEAD_DIM, (h + 1) * ATT_HEAD_DIM)
        o_first = jnp.zeros((tq, ATT_HEAD_DIM), F32)
        o_second = jnp.zeros((tq, ATT_HEAD_DIM), F32)
        for blk in range(n_win):
            lo, hi = block_rows(blk)
            parts = []
            if lo == 0:
                parts.append(p_scr[h, 0:tq, blk * tq:(blk + 1) * tq])
            if hi == 2 * tq:
                parts.append(p_scr[h, tq:2 * tq, (blk - 1) * tq:blk * tq])
            p = parts[0] if len(parts) == 1 else jnp.concatenate(parts, axis=0)
            o = jnp.dot(p, kv_block(blk, QKV_V, h), preferred_element_type=F32)
            if lo == 0:
                o_first = o_first + o[0:tq]
            if hi == 2 * tq:
                o_second = o_second + o[tq - lo:2 * tq - lo]
        o = jnp.concatenate([o_first, o_second], axis=0)
        o_ref[:, sl] = (o * inv_scr[h] * _silu(g_ref[:, sl])).astype(o_ref.dtype)

    softmax_stage = functools.partial(_softmax_stage, s_scr, p_scr, inv_scr, bias_ref, q_rows=2 * tq)

    @pl.when(first_block < 0)
    def _():
        _head_pipeline(functools.partial(qk_stage, mask_history=True), softmax_stage, pv_stage)

    @pl.when(first_block >= 0)
    def _():
        _head_pipeline(functools.partial(qk_stage, mask_history=False), softmax_stage, pv_stage)


def _prompt_attention(qkv, rest, bias):
    t = rest.shape[0]
    tq = ATT_Q_ROWS
    n_hist_blocks = BAND_ROWS // tq
    n_win = n_hist_blocks + 2
    span = BAND_ROWS + tq
    assert (QKV_K, QKV_V) == (0, 1) and n_win + 2 <= KV_RING_SLOTS
    in_specs = [pl.BlockSpec((1, ATT_HEADS, 2 * tq, ATT_HEAD_DIM), lambda i: (QKV_Q, 0, i, 0)),
                pl.BlockSpec((2 * tq, D_ATT), lambda i: (i, R_G // D_ATT)),
                pl.BlockSpec(memory_space=pl.ANY),
                pl.BlockSpec((ATT_HEADS, tq, span), lambda i: (0, 0, 0))]
    kern = functools.partial(_prompt_attn_kernel, n_hist_blocks=n_hist_blocks)
    return pl.pallas_call(
        kern,
        grid=(t // (2 * tq),),
        in_specs=in_specs,
        out_specs=pl.BlockSpec((2 * tq, D_ATT), lambda i: (i, 0)),
        out_shape=jax.ShapeDtypeStruct((t, D_ATT), BF16),
        scratch_shapes=[pltpu.VMEM((ATT_HEADS, 2 * tq, span), F32),
                        pltpu.VMEM((ATT_HEADS, 2 * tq, span), BF16),
                        pltpu.VMEM((ATT_HEADS, 2 * tq, ATT_HEAD_DIM), F32),
                        pltpu.VMEM((KV_RING_SLOTS, 2, ATT_HEADS, tq, ATT_HEAD_DIM), BF16),
                        pltpu.SemaphoreType.DMA((KV_RING_SLOTS,))],
        compiler_params=_compiler_params(("arbitrary",)),
        name="prompt_attention",
    )(qkv, rest, qkv, bias)


def _sample_attn_kernel(q_ref, kn_ref, vn_ref, g_ref, kc_lo, kc_hi, vc_lo, vc_hi, bias_ref, o_ref,
                        s_scr, p_scr, inv_scr, *, q_rows):
    new_pad = LANES - q_rows
    zpad = jnp.zeros((new_pad, ATT_HEAD_DIM), BF16)

    def cached(lo_ref, hi_ref, h):
        ref = (lo_ref if h < SUBLANES else hi_ref).reshape(BAND_ROWS * SUBLANES, ATT_HEAD_DIM)
        return ref[pl.ds(h % SUBLANES, BAND_ROWS, stride=SUBLANES), :].astype(BF16)

    def qk_stage(h):
        qh = q_ref[0, h]
        s_scr[h, :, 0:BAND_ROWS] = lax.dot_general(qh, cached(kc_lo, kc_hi, h), (((1,), (1,)), ((), ())),
                                                   preferred_element_type=F32)
        k_new = jnp.concatenate([kn_ref[0, h], zpad], axis=0)
        s_scr[h, :, BAND_ROWS:] = lax.dot_general(qh, k_new, (((1,), (1,)), ((), ())), preferred_element_type=F32)

    def pv_stage(h):
        sl = slice(h * ATT_HEAD_DIM, (h + 1) * ATT_HEAD_DIM)
        v_new = jnp.concatenate([vn_ref[0, h], zpad], axis=0)
        o = jnp.dot(p_scr[h, :, 0:BAND_ROWS], cached(vc_lo, vc_hi, h), preferred_element_type=F32)
        o = o + jnp.dot(p_scr[h, :, BAND_ROWS:], v_new, preferred_element_type=F32)
        o_ref[:, sl] = (o * inv_scr[h] * _silu(g_ref[:, sl])).astype(o_ref.dtype)

    _head_pipeline(qk_stage, functools.partial(_softmax_stage, s_scr, p_scr, inv_scr, bias_ref, q_rows=q_rows),
                   pv_stage)


def _sample_attention(qkv, rest, cache_k, cache_v, bias, *, n_streams, q_rows):
    t = rest.shape[0]
    span_pad = BAND_ROWS + LANES
    kern = functools.partial(_sample_attn_kernel, q_rows=q_rows)

    def new_spec(which):
        return pl.BlockSpec((1, ATT_HEADS, q_rows, ATT_HEAD_DIM), lambda b: (which, 0, b, 0))

    def cache_spec(half):
        return pl.BlockSpec((None, BAND_ROWS, None, SUBLANES, ATT_HEAD_DIM), lambda b: (b, 0, half, 0, 0))

    return pl.pallas_call(
        kern,
        grid=(n_streams,),
        in_specs=[
            new_spec(QKV_Q), new_spec(QKV_K), new_spec(QKV_V),
            pl.BlockSpec((q_rows, D_ATT), lambda b: (b, R_G // D_ATT)),
            cache_spec(0), cache_spec(1), cache_spec(0), cache_spec(1),
            pl.BlockSpec((ATT_HEADS, q_rows, span_pad), lambda b: (0, 0, 0)),
        ],
        out_specs=pl.BlockSpec((q_rows, D_ATT), lambda b: (b, 0)),
        out_shape=jax.ShapeDtypeStruct((t, D_ATT), BF16),
        scratch_shapes=[pltpu.VMEM((ATT_HEADS, q_rows, span_pad), F32),
                        pltpu.VMEM((ATT_HEADS, q_rows, span_pad), BF16),
                        pltpu.VMEM((ATT_HEADS, q_rows, ATT_HEAD_DIM), F32)],
        compiler_params=_compiler_params(("parallel",)),
        name="sample_attention",
    )(qkv, qkv, qkv, rest, cache_k, cache_k, cache_v, cache_v, bias)


def _outproj_kernel(ys_ref, ya_ref, w1_ref, w2_ref, x_ref, o_ref):
    acc = jnp.dot(ys_ref[...], w1_ref[...], preferred_element_type=F32)
    acc = acc + jnp.dot(ya_ref[...], w2_ref[...], preferred_element_type=F32)
    o_ref[...] = x_ref[...] + acc


def _out_projection(y_ssm, y_att, w_out, x, *, tm, tn):
    t = x.shape[0]
    return pl.pallas_call(
        _outproj_kernel,
        grid=(t // tm, D_MODEL // tn),
        in_specs=[
            pl.BlockSpec((tm, D_SSM), lambda i, j: (i, 0)),
            pl.BlockSpec((tm, D_ATT), lambda i, j: (i, 0)),
            pl.BlockSpec((D_SSM, tn), lambda i, j: (0, j)),
            pl.BlockSpec((D_ATT, tn), lambda i, j: (D_SSM // D_ATT, j)),
            pl.BlockSpec((tm, tn), lambda i, j: (i, j)),
        ],
        out_specs=pl.BlockSpec((tm, tn), lambda i, j: (i, j)),
        out_shape=jax.ShapeDtypeStruct((t, D_MODEL), F32),
        compiler_params=_compiler_params(("parallel", "arbitrary")),
        name="out_projection",
    )(y_ssm, y_att, w_out, w_out, x)


def _pad_to(v, size, axis):
    pad = [(0, 0)] * v.ndim
    pad[axis] = (0, size - v.shape[axis])
    return jnp.pad(v, pad)


def _cast_w_in_kernel(main_ref, next_ref, wa_ref, wq_ref, wdt_ref, *, n_a, row_shift):
    j = pl.program_id(0)
    tn = main_ref.shape[0]
    chunk = 256

    @pl.when(j == n_a)
    def _():
        row_id = lax.broadcasted_iota(jnp.int32, (DT_COLS, main_ref.shape[1]), 0)
        wdt_ref[...] = jnp.where(row_id < row_shift, main_ref[0:DT_COLS, :], 0.0).T.astype(BF16)

    def put(rows_of, out_ref):
        for r0 in range(0, tn, chunk):
            out_ref[:, r0:r0 + chunk] = rows_of(r0).T.astype(BF16)

    @pl.when(j < n_a)
    def _():
        put(lambda r0: main_ref[r0:r0 + chunk, :], wa_ref)

    @pl.when(j >= n_a)
    def _():
        def shifted(r0):
            lo = r0 + row_shift
            if lo + chunk <= tn:
                return main_ref[lo:lo + chunk, :]
            return jnp.concatenate([main_ref[lo:tn, :], next_ref[0:lo + chunk - tn, :]], axis=0)

        put(shifted, wq_ref)


def _cast_w_in(w_in_t):
    tn = PROJ_TN
    n_a = OFF_DT // tn
    n_q = (4 * D_ATT) // tn
    row_shift = OFF_Q - OFF_DT
    assert OFF_DT % tn == 0 and row_shift % SUBLANES == 0 and 0 < row_shift < LANES
    kern = functools.partial(_cast_w_in_kernel, n_a=n_a, row_shift=row_shift)
    return pl.pallas_call(
        kern,
        grid=(n_a + n_q,),
        in_specs=[
            pl.BlockSpec((tn, D_MODEL), lambda j: (j, 0)),
            pl.BlockSpec((LANES, D_MODEL), lambda j: ((j + 1) * (tn // LANES), 0)),
        ],
        out_specs=[
            pl.BlockSpec((D_MODEL, tn), lambda j: (0, jnp.minimum(j, n_a - 1))),
            pl.BlockSpec((D_MODEL, tn), lambda j: (0, jnp.maximum(j - n_a, 0))),
            pl.BlockSpec((D_MODEL, DT_COLS), lambda j: (0, 0)),
        ],
        out_shape=[
            jax.ShapeDtypeStruct((D_MODEL, n_a * tn), BF16),
            jax.ShapeDtypeStruct((D_MODEL, n_q * tn), BF16),
            jax.ShapeDtypeStruct((D_MODEL, DT_COLS), BF16),
        ],
        compiler_params=_compiler_params(("arbitrary",)),
        name="cast_w_in",
    )(w_in_t, w_in_t)


def _prepare_params(norm_w, w_in, conv_w, conv_b, dt_bias, a_log, d_skip, ssm_norm_w, q_norm_w, k_norm_w, w_out):
    w_a, w_qkvg, w_dt = _cast_w_in(w_in.T)
    qk_w = jnp.concatenate([jnp.tile(q_norm_w, ATT_HEADS), jnp.tile(k_norm_w, ATT_HEADS)]).reshape(1, 2 * D_ATT)
    convp = _pad_to(jnp.concatenate([conv_w, conv_b[None]], axis=0), SUBLANES, 0)
    hp = _pad_to(_pad_to(jnp.stack([dt_bias, a_log]), LANES, 1), SUBLANES, 0)
    dexp = jnp.repeat(d_skip, SSM_HEADDIM).reshape(1, D_SSM)
    head_of_col = np.arange(D_SSM) // SSM_HEADDIM
    expand = (np.arange(LANES)[:, None] == head_of_col[None, :]).astype(np.float32)
    expand = jnp.asarray(np.tile(expand, (N_SPLIT, 1)), dtype=BF16)
    return dict(norm_w=norm_w.reshape(1, D_MODEL), w_qkvg=w_qkvg, w_a=w_a, w_dt=w_dt, qk_w=qk_w,
                convp=convp, hp=hp, dexp=dexp, ssm_nw=ssm_norm_w.reshape(1, D_SSM), expand=expand,
                w_out=w_out.astype(BF16))


def _rel_bias_table(rel_bias, q_rows, hist_rows, span_pad, band_chunk):
    n_heads = rel_bias.shape[0]
    period = span_pad + q_rows
    k = np.arange(period)
    rel = np.clip(hist_rows + q_rows - 1 - k, -REL_CLIP, REL_CLIP) + REL_CLIP
    onehot = jnp.asarray(np.eye(N_REL, dtype=np.float32)[:, rel])
    diag_row = jnp.dot(rel_bias.astype(F32), onehot, precision=lax.Precision.HIGHEST)
    flat = jnp.tile(diag_row, (1, q_rows))[:, :q_rows * (period - 1)]
    table = flat.reshape(n_heads, q_rows, period - 1)[:, :, q_rows - 1:q_rows - 1 + span_pad]
    i_idx = np.arange(q_rows)[:, None]
    j_idx = np.arange(span_pad)[None, :]
    valid = j_idx < hist_rows + q_rows
    if band_chunk is not None:
        start = (i_idx // band_chunk) * band_chunk
        valid = valid & (j_idx >= start) & (j_idx < start + hist_rows + band_chunk)
    return jnp.where(jnp.asarray(np.broadcast_to(valid, (q_rows, span_pad)))[None], table / ATT_SCALE, MASK_VALUE)


def _layer(x, hist, s0, cache_k, cache_v, prm, rel_bias, *, n_streams, seq, prompt, proj_tm):
    t = n_streams * seq
    x2 = x.reshape(t, D_MODEL)
    proj = _in_projection(x2, prm["norm_w"], prm["w_qkvg"], prm["w_a"], prm["w_dt"], prm["qk_w"],
                          tm=proj_tm, emit_kv=not prompt)
    if prompt:
        qkv, rest, dt = proj
        q_rows, n_chunks = SSD_ROWS, seq // SSD_ROWS
        kv_rows = BAND_ROWS
        kv_new = _kv_rows(x2, prm["norm_w"], prm["w_qkvg"], prm["qk_w"], first_row=t - kv_rows, n_rows=kv_rows,
                          tm=512)
    else:
        qkv, rest, dt, kv_new = proj
        q_rows, n_chunks = seq, 1
        kv_rows = t
    y_ssm, s_new = _ssd_branch(rest, dt, hist, s0, prm["convp"], prm["hp"], prm["dexp"], prm["ssm_nw"],
                               prm["expand"], n_streams=n_streams, q_rows=q_rows, n_chunks=n_chunks)
    if prompt:
        bias = _rel_bias_table(rel_bias, ATT_Q_ROWS, BAND_ROWS, BAND_ROWS + ATT_Q_ROWS, CHUNK)
        y_att = _prompt_attention(qkv, rest, bias)
    else:
        bias = _rel_bias_table(rel_bias, seq, BAND_ROWS, BAND_ROWS + LANES, None)
        y_att = _sample_attention(qkv, rest, cache_k, cache_v, bias, n_streams=n_streams, q_rows=seq)
    y = _out_projection(y_ssm, y_att, prm["w_out"], x2, tm=1024, tn=1024)
    new_conv = rest.reshape(n_streams, seq, R_COLS)[:, seq - (CONV_WIDTH - 1):, R_X:]
    kv_streams = kv_rows // n_streams
    kh = kv_new[:, :D_ATT].reshape(n_streams, kv_streams, ATT_HEADS, ATT_HEAD_DIM)
    vh = kv_new[:, D_ATT:].reshape(n_streams, kv_streams, ATT_HEADS, ATT_HEAD_DIM)
    new_ssm = s_new.reshape(n_streams, SSM_HEADS, SSM_HEADDIM, D_STATE)
    return y.reshape(n_streams, seq, D_MODEL), new_conv, new_ssm, kh, vh


def kernel(x_prompt, x_sample, state_conv, state_ssm, cache_k, cache_v, norm_w, w_in, conv_w, conv_b, dt_bias, a_log, d_skip, ssm_norm_w, q_norm_w, k_norm_w, rel_bias, w_out):
    bp, lp, _ = x_prompt.shape
    bs, ls, _ = x_sample.shape
    assert bp == 1 and lp % 1024 == 0 and lp >= BAND_ROWS
    assert ls % (2 * SUBLANES) == 0 and ls <= LANES and (bs * ls) % 1024 == 0 and cache_k.shape[2] == BAND_ROWS
    assert norm_w.shape[0] == 1
    prm = _prepare_params(norm_w[0], w_in[0], conv_w[0], conv_b[0], dt_bias[0], a_log[0], d_skip[0],
                          ssm_norm_w[0], q_norm_w[0], k_norm_w[0], w_out[0])
    rb = rel_bias[0]

    zero_hist = jnp.zeros((bp, SUBLANES, CONV_DIM), F32)
    zero_state = jnp.zeros((bp, D_SSM, D_STATE), F32)
    yp, c1, s1, k1, v1 = _layer(x_prompt, zero_hist, zero_state, None, None, prm, rb,
                                n_streams=bp, seq=lp, prompt=True, proj_tm=1024)

    hist = jnp.pad(state_conv[0], ((0, 0), (SUBLANES - (CONV_WIDTH - 1), 0), (0, 0)))
    s0 = state_ssm[0].reshape(bs, D_SSM, D_STATE)
    half_heads = (bs, BAND_ROWS, ATT_HEADS // SUBLANES, SUBLANES, ATT_HEAD_DIM)
    ck = cache_k[0].reshape(half_heads)
    cv = cache_v[0].reshape(half_heads)
    ys, c2, s2, k2, v2 = _layer(x_sample, hist, s0, ck, cv, prm, rb, n_streams=bs, seq=ls, prompt=False,
                                proj_tm=512)

    return (yp, ys, c1[None], s1[None], k1[None], v1[None], c2[None], s2[None], k2[None], v2[None])
```

```python
import functools

import jax
import jax.numpy as jnp
import numpy as np
from jax import lax
from jax.experimental import pallas as pl
from jax.experimental.pallas import tpu as pltpu

F32 = jnp.float32
BF16 = jnp.bfloat16

D_MODEL = 2048
CHUNK = 64
LEFT_CHUNKS = 8
BAND_ROWS = LEFT_CHUNKS * CHUNK
D_SSM = 2048
D_ATT = 2048
SSM_HEADDIM = 64
SSM_HEADS = 32
SSM_GROUPS = 4
HEADS_PER_GROUP = SSM_HEADS // SSM_GROUPS
GROUP_COLS = HEADS_PER_GROUP * SSM_HEADDIM
D_STATE = 128
BC_COLS = 2 * SSM_GROUPS * D_STATE
CONV_WIDTH = 4
CONV_DIM = D_SSM + BC_COLS
ATT_HEAD_DIM = 128
ATT_HEADS = 16
REL_CLIP = 128
N_REL = 2 * REL_CLIP + 1
EPS = 1e-6
OFF_Z = 0
OFF_XBC = OFF_Z + D_SSM
OFF_DT = OFF_XBC + CONV_DIM
OFF_Q = OFF_DT + SSM_HEADS
OFF_K = OFF_Q + D_ATT
OFF_V = OFF_K + D_ATT
OFF_G = OFF_V + D_ATT

LANES = 128
SUBLANES = 8
VMEM_LIMIT_BYTES = 56 * 1024 * 1024

R_G = 0
R_Z = R_G + D_ATT
R_X = R_Z + D_SSM
R_BC = R_X + D_SSM
R_COLS = R_BC + BC_COLS
DT_COLS = LANES
QKV_K, QKV_V, QKV_Q = 0, 1, 2
N_SPLIT = 3

PROJ_TN = 1024
SSD_ROWS = 128
SSD_CHUNK_ROWS = 512
ATT_Q_ROWS = 128
SOFTMAX_SLAB = 32
MASK_VALUE = -1e30
ATT_SCALE = ATT_HEAD_DIM ** -0.5
LOG2E = 1.4426950408889634


def _silu(v):
    return v * (1.0 / (1.0 + jnp.exp2(v * (-LOG2E))))


def _compiler_params(semantics):
    return pltpu.CompilerParams(dimension_semantics=semantics, vmem_limit_bytes=VMEM_LIMIT_BYTES)


def _normed_rows(x_ref, nw_ref):
    x = x_ref[...]
    ms = jnp.mean(x * x, axis=-1, keepdims=True)
    return (x * lax.rsqrt(ms + EPS) * nw_ref[...]).astype(BF16)


def _head_norm(a, use_norm, w):
    r = lax.rsqrt(jnp.mean(a * a, axis=-1, keepdims=True) + EPS)
    return a * jnp.where(use_norm, r, 1.0) * jnp.where(use_norm, w, 1.0)


def _inproj_kernel(*refs, n_qk, n_qkv, n_g, emit_kv):
    x_ref, nw_ref, wqkvg_ref, wa_ref, wdt_ref, qkw_ref, qkv_ref, rest_ref, dt_ref = refs[:9]
    kv_ref = refs[9] if emit_kv else None
    h_scr = refs[-1]
    j = pl.program_id(1)
    heads_per_tile = PROJ_TN // ATT_HEAD_DIM

    @pl.when(j == 0)
    def _():
        h = _normed_rows(x_ref, nw_ref)
        h_scr[...] = h
        dt_ref[...] = jnp.dot(h, wdt_ref[...], preferred_element_type=F32)

    def qkv_tile(use_norm):
        acc = jnp.dot(h_scr[...], wqkvg_ref[...], preferred_element_type=F32)
        for hh in range(heads_per_tile):
            sl = slice(hh * ATT_HEAD_DIM, (hh + 1) * ATT_HEAD_DIM)
            head = acc[:, sl]
            if use_norm:
                r = lax.rsqrt(jnp.mean(head * head, axis=-1, keepdims=True) + EPS)
                head = head * r * qkw_ref[:, sl]
            qkv_ref[0, hh] = head.astype(BF16)
            if emit_kv:
                kv_ref[:, sl] = head

    pl.when(j < n_qk)(functools.partial(qkv_tile, True))
    pl.when(jnp.logical_and(j >= n_qk, j < n_qkv))(functools.partial(qkv_tile, False))

    @pl.when(jnp.logical_and(j >= n_qkv, j < n_qkv + n_g))
    def _():
        rest_ref[...] = jnp.dot(h_scr[...], wqkvg_ref[...], preferred_element_type=F32)

    @pl.when(j >= n_qkv + n_g)
    def _():
        rest_ref[...] = jnp.dot(h_scr[...], wa_ref[...], preferred_element_type=F32)


def _in_projection(x, norm_w, w_qkvg, w_a, w_dt, qk_w, *, tm, emit_kv):
    t = x.shape[0]
    tn = PROJ_TN
    tiles_per_proj = D_ATT // tn
    n_qk, n_qkv, n_g = 2 * tiles_per_proj, 3 * tiles_per_proj, tiles_per_proj
    n_a = (D_SSM + CONV_DIM) // tn
    heads_per_tile = tn // ATT_HEAD_DIM
    grid = (t // tm, n_qkv + n_g + n_a)
    kern = functools.partial(_inproj_kernel, n_qk=n_qk, n_qkv=n_qkv, n_g=n_g, emit_kv=emit_kv)

    def qkv_index(i, j):
        jj = jnp.minimum(j, n_qkv - 1)
        plane = (jj // tiles_per_proj + QKV_Q) % 3
        return (plane, jj % tiles_per_proj, i, 0)

    out_specs = [
        pl.BlockSpec((1, heads_per_tile, tm, ATT_HEAD_DIM), qkv_index),
        pl.BlockSpec((tm, tn), lambda i, j: (i, jnp.maximum(j - n_qkv, 0))),
        pl.BlockSpec((tm, DT_COLS), lambda i, j: (i, 0)),
    ]
    out_shape = [
        jax.ShapeDtypeStruct((3, ATT_HEADS, t, ATT_HEAD_DIM), BF16),
        jax.ShapeDtypeStruct((t, R_COLS), F32),
        jax.ShapeDtypeStruct((t, DT_COLS), F32),
    ]
    if emit_kv:
        n_kv = n_qkv - tiles_per_proj
        out_specs.append(pl.BlockSpec((tm, tn), lambda i, j: (i, jnp.clip(j - tiles_per_proj, 0, n_kv - 1))))
        out_shape.append(jax.ShapeDtypeStruct((t, 2 * D_ATT), F32))
    return pl.pallas_call(
        kern,
        grid=grid,
        in_specs=[
            pl.BlockSpec((tm, D_MODEL), lambda i, j: (i, 0)),
            pl.BlockSpec((1, D_MODEL), lambda i, j: (0, 0)),
            pl.BlockSpec((D_MODEL, tn), lambda i, j: (0, jnp.minimum(j, n_qkv + n_g - 1))),
            pl.BlockSpec((D_MODEL, tn), lambda i, j: (0, jnp.clip(j - n_qkv - n_g, 0, n_a - 1))),
            pl.BlockSpec((D_MODEL, DT_COLS), lambda i, j: (0, 0)),
            pl.BlockSpec((1, tn), lambda i, j: (0, jnp.minimum(j, n_qk - 1))),
        ],
        out_specs=out_specs,
        out_shape=out_shape,
        scratch_shapes=[pltpu.VMEM((tm, D_MODEL), BF16)],
        compiler_params=_compiler_params(("parallel", "arbitrary")),
        name="in_projection",
    )(x, norm_w, w_qkvg, w_a, w_dt, qk_w)


def _kv_rows_kernel(x_ref, nw_ref, w_ref, qkw_ref, o_ref, h_scr, *, n_k):
    j = pl.program_id(1)

    @pl.when(j == 0)
    def _():
        h_scr[...] = _normed_rows(x_ref, nw_ref)

    acc = jnp.dot(h_scr[...], w_ref[...], preferred_element_type=F32)
    for hh in range(PROJ_TN // ATT_HEAD_DIM):
        sl = slice(hh * ATT_HEAD_DIM, (hh + 1) * ATT_HEAD_DIM)
        o_ref[:, sl] = _head_norm(acc[:, sl], j < n_k, qkw_ref[:, sl])


def _kv_rows(x, norm_w, w_qkvg, qk_w, *, first_row, n_rows, tm):
    tn = PROJ_TN
    tiles_per_proj = D_ATT // tn
    row0 = first_row // tm
    kern = functools.partial(_kv_rows_kernel, n_k=tiles_per_proj)
    return pl.pallas_call(
        kern,
        grid=(n_rows // tm, 2 * tiles_per_proj),
        in_specs=[
            pl.BlockSpec((tm, D_MODEL), lambda i, j: (row0 + i, 0)),
            pl.BlockSpec((1, D_MODEL), lambda i, j: (0, 0)),
            pl.BlockSpec((D_MODEL, tn), lambda i, j: (0, tiles_per_proj + j)),
            pl.BlockSpec((1, tn), lambda i, j: (0, jnp.minimum(tiles_per_proj + j, 2 * tiles_per_proj - 1))),
        ],
        out_specs=pl.BlockSpec((tm, tn), lambda i, j: (i, j)),
        out_shape=jax.ShapeDtypeStruct((n_rows, 2 * D_ATT), F32),
        scratch_shapes=[pltpu.VMEM((tm, D_MODEL), BF16)],
        compiler_params=_compiler_params(("parallel", "arbitrary")),
        name="kv_rows",
    )(x, norm_w, w_qkvg, qk_w)


def _transpose_rows_to_lanes(v):
    q = v.shape[0]
    if q < LANES:
        v = jnp.concatenate([v, jnp.zeros((LANES - q, LANES), v.dtype)], axis=0)
    return v.T[:, 0:q]


def _split_bf16(v):
    pieces = []
    rem = v
    for _ in range(N_SPLIT):
        piece = rem.astype(BF16)
        pieces.append(piece)
        rem = rem - piece.astype(F32)
    return jnp.concatenate(pieces, axis=1)


def _ssd_kernel(x_ref, bc_ref, z_ref, dt_ref, hist_ref, s0_ref, convp_ref, hp_ref, dexp_ref, nw_ref, expand_ref,
                y_ref, sout_ref, conv_scr, act_scr, st_scr, yd_scr, exp_scr, *, q_rows, sub_rows, n_chunks):
    c = pl.program_id(1)
    q = q_rows

    @pl.when(c == 0)
    def _():
        conv_scr[0:SUBLANES, :] = hist_ref[0]
        for g in range(SSM_GROUPS):
            st_scr[g] = s0_ref[0, g * GROUP_COLS:(g + 1) * GROUP_COLS, :].T

    conv_scr[SUBLANES:SUBLANES + q, 0:D_SSM] = x_ref[...]
    conv_scr[SUBLANES:SUBLANES + q, D_SSM:] = bc_ref[...]

    for c0 in range(0, CONV_DIM, GROUP_COLS):
        cols = slice(c0, c0 + GROUP_COLS)
        xp = conv_scr[:, cols]
        conv = convp_ref[CONV_WIDTH:CONV_WIDTH + 1, cols] + convp_ref[CONV_WIDTH - 1:CONV_WIDTH, cols] * xp[SUBLANES:]
        for shift in range(1, CONV_WIDTH):
            tap = CONV_WIDTH - 1 - shift
            conv = conv + convp_ref[tap:tap + 1, cols] * pltpu.roll(xp, shift, axis=0)[SUBLANES:]
        act_scr[:, cols] = _silu(conv)
    conv_scr[0:SUBLANES, :] = conv_scr[q:q + SUBLANES, :]

    s = sub_rows
    ii = lax.broadcasted_iota(jnp.int32, (s, s), 0)
    jj = lax.broadcasted_iota(jnp.int32, (s, s), 1)
    causal = ii >= jj
    for sc in range(q // s):
        rows = slice(sc * s, (sc + 1) * s)
        v = dt_ref[rows, :] + hp_ref[0:1, :]
        dt = jnp.maximum(v, 0.0) + jnp.log1p(jnp.exp(-jnp.abs(v)))
        a = dt * (-jnp.exp(hp_ref[1:2, :]))
        acum = jnp.dot(causal.astype(F32), a, precision=lax.Precision.HIGHEST, preferred_element_type=F32)
        acum2 = acum * LOG2E
        row_t = _transpose_rows_to_lanes(acum2 - jnp.log(dt) * LOG2E)
        last = acum[s - 1:s, :]
        factors = jnp.concatenate(
            [jnp.exp(acum),
             jnp.exp(last - acum) * dt,
             jnp.broadcast_to(jnp.exp(last), (SUBLANES, LANES))], axis=0)
        exp_scr[sc] = jnp.dot(_split_bf16(factors), expand_ref[...], preferred_element_type=F32)

        for g in range(SSM_GROUPS):
            cols = slice(g * GROUP_COLS, (g + 1) * GROUP_COLS)
            xact = act_scr[rows, cols]
            bact = act_scr[rows, D_SSM + g * D_STATE:D_SSM + (g + 1) * D_STATE].astype(BF16)
            cact = act_scr[rows, D_SSM + (SSM_GROUPS + g) * D_STATE:D_SSM + (SSM_GROUPS + g + 1) * D_STATE].astype(BF16)
            cb = lax.dot_general(cact, bact, (((1,), (1,)), ((), ())), preferred_element_type=F32)
            st = st_scr[g]
            y_off = jnp.dot(cact, st.astype(BF16), preferred_element_type=F32)
            for r in range(HEADS_PER_GROUP):
                h = g * HEADS_PER_GROUP + r
                m = cb * jnp.exp2(jnp.where(causal, acum2[:, h:h + 1] - row_t[h:h + 1, :], -jnp.inf))
                xh = xact[:, r * SSM_HEADDIM:(r + 1) * SSM_HEADDIM]
                yd_scr[sc, :, h * SSM_HEADDIM:(h + 1) * SSM_HEADDIM] = jnp.dot(
                    m.astype(BF16), xh.astype(BF16), preferred_element_type=F32)
            y = yd_scr[sc, :, cols] + y_off * exp_scr[sc, 0:s, cols] + dexp_ref[:, cols] * xact
            xw = (xact * exp_scr[sc, s:2 * s, cols]).astype(BF16)
            upd = lax.dot_general(bact, xw, (((0,), (0,)), ((), ())), preferred_element_type=F32)
            st_scr[g] = st * exp_scr[sc, 2 * s:2 * s + 1, cols] + upd

            yg = y * _silu(z_ref[rows, cols])
            rn = lax.rsqrt(jnp.mean(yg * yg, axis=-1, keepdims=True) + EPS)
            y_ref[rows, cols] = (yg * rn * nw_ref[:, cols]).astype(y_ref.dtype)

    @pl.when(c == n_chunks - 1)
    def _():
        for g in range(SSM_GROUPS):
            sout_ref[0, g * GROUP_COLS:(g + 1) * GROUP_COLS, :] = st_scr[g].T


def _ssd_branch(rest, dt, hist, s0, convp, hp, dexp, ssm_nw, expand, *, n_streams, q_rows, sub_rows, n_chunks):
    t = rest.shape[0]
    n_sub = q_rows // sub_rows
    assert n_sub * sub_rows == q_rows

    def rows(b, c):
        return b * n_chunks + c

    def const(shape):
        return pl.BlockSpec(shape, lambda b, c: (0,) * len(shape))

    kern = functools.partial(_ssd_kernel, q_rows=q_rows, sub_rows=sub_rows, n_chunks=n_chunks)
    return pl.pallas_call(
        kern,
        grid=(n_streams, n_chunks),
        in_specs=[
            pl.BlockSpec((q_rows, D_SSM), lambda b, c: (rows(b, c), R_X // D_SSM)),
            pl.BlockSpec((q_rows, BC_COLS), lambda b, c: (rows(b, c), R_BC // BC_COLS)),
            pl.BlockSpec((q_rows, D_SSM), lambda b, c: (rows(b, c), R_Z // D_SSM)),
            pl.BlockSpec((q_rows, DT_COLS), lambda b, c: (rows(b, c), 0)),
            pl.BlockSpec((1, SUBLANES, CONV_DIM), lambda b, c: (b, 0, 0)),
            pl.BlockSpec((1, D_SSM, D_STATE), lambda b, c: (b, 0, 0)),
            const((SUBLANES, CONV_DIM)),
            const((SUBLANES, LANES)),
            const((1, D_SSM)),
            const((1, D_SSM)),
            const((N_SPLIT * LANES, D_SSM)),
        ],
        out_specs=[
            pl.BlockSpec((q_rows, D_SSM), lambda b, c: (rows(b, c), 0)),
            pl.BlockSpec((1, D_SSM, D_STATE), lambda b, c: (b, 0, 0)),
        ],
        out_shape=[
            jax.ShapeDtypeStruct((t, D_SSM), BF16),
            jax.ShapeDtypeStruct((n_streams, D_SSM, D_STATE), F32),
        ],
        scratch_shapes=[
            pltpu.VMEM((SUBLANES + q_rows, CONV_DIM), F32),
            pltpu.VMEM((q_rows, CONV_DIM), F32),
            pltpu.VMEM((SSM_GROUPS, D_STATE, GROUP_COLS), F32),
            pltpu.VMEM((n_sub, sub_rows, D_SSM), F32),
            pltpu.VMEM((n_sub, 2 * sub_rows + SUBLANES, D_SSM), F32),
        ],
        compiler_params=_compiler_params(("parallel", "arbitrary")),
        name="ssd_branch",
    )(rest, rest, rest, dt, hist, s0, convp, hp, dexp, ssm_nw, expand)


def _softmax_stage(s_scr, p_scr, inv_scr, bias_ref, h, q_rows):
    exp2_scale = ATT_SCALE * LOG2E
    slab = min(SOFTMAX_SLAB, q_rows)
    bias_rows = bias_ref.shape[1]
    for r0 in range(0, q_rows, slab):
        b0 = r0 % bias_rows
        u = s_scr[h, r0:r0 + slab, :] + bias_ref[h, b0:b0 + slab, :]
        m = jnp.max(u, axis=-1, keepdims=True)
        e = jnp.exp2((u - m) * exp2_scale)
        p_scr[h, r0:r0 + slab, :] = e.astype(BF16)
        inv = 1.0 / jnp.sum(e, axis=-1, keepdims=True)
        inv_scr[h, r0:r0 + slab, :] = jnp.broadcast_to(inv, (slab, ATT_HEAD_DIM))


def _head_pipeline(qk_stage, softmax_stage, pv_stage):
    for step in range(ATT_HEADS + 2):
        if step < ATT_HEADS:
            qk_stage(step)
        if 1 <= step <= ATT_HEADS:
            softmax_stage(step - 1)
        if step >= 2:
            pv_stage(step - 2)


def _prompt_attn_kernel(*refs, n_hist_blocks):
    n_win = n_hist_blocks + 2
    tq = ATT_Q_ROWS
    q_ref, g_ref = refs[0], refs[1]
    kv_refs = refs[2:2 + n_win]
    bias_ref, o_ref, s_scr, p_scr, inv_scr = refs[2 + n_win:]
    first_block = 2 * pl.program_id(0) - n_hist_blocks

    def block_rows(blk):
        lo = 0 if blk <= n_hist_blocks else tq
        hi = 2 * tq if blk >= 1 else tq
        return lo, hi

    def qk_stage(h, mask_history):
        for blk in range(n_win):
            lo, hi = block_rows(blk)
            s = lax.dot_general(q_ref[0, h, lo:hi, :], kv_refs[blk][QKV_K, h], (((1,), (1,)), ((), ())),
                                preferred_element_type=F32)
            if mask_history and blk < n_hist_blocks:
                s = jnp.where(first_block + blk >= 0, s, MASK_VALUE)
            if lo == 0:
                s_scr[h, 0:tq, blk * tq:(blk + 1) * tq] = s[0:tq]
            if hi == 2 * tq:
                s_scr[h, tq:2 * tq, (blk - 1) * tq:blk * tq] = s[tq - lo:2 * tq - lo]

    def pv_stage(h):
        sl = slice(h * ATT_HEAD_DIM, (h + 1) * ATT_HEAD_DIM)
        o_first = jnp.zeros((tq, ATT_HEAD_DIM), F32)
        o_second = jnp.zeros((tq, ATT_HEAD_DIM), F32)
        for blk in range(n_win):
            lo, hi = block_rows(blk)
            parts = []
            if lo == 0:
                parts.append(p_scr[h, 0:tq, blk * tq:(blk + 1) * tq])
            if hi == 2 * tq:
                parts.append(p_scr[h, tq:2 * tq, (blk - 1) * tq:blk * tq])
            p = parts[0] if len(parts) == 1 else jnp.concatenate(parts, axis=0)
            o = jnp.dot(p, kv_refs[blk][QKV_V, h], preferred_element_type=F32)
            if lo == 0:
                o_first = o_first + o[0:tq]
            if hi == 2 * tq:
                o_second = o_second + o[tq - lo:2 * tq - lo]
        o = jnp.concatenate([o_first, o_second], axis=0)
        o_ref[:, sl] = (o * inv_scr[h] * _silu(g_ref[:, sl])).astype(o_ref.dtype)

    softmax_stage = functools.partial(_softmax_stage, s_scr, p_scr, inv_scr, bias_ref, q_rows=2 * tq)

    @pl.when(first_block < 0)
    def _():
        _head_pipeline(functools.partial(qk_stage, mask_history=True), softmax_stage, pv_stage)

    @pl.when(first_block >= 0)
    def _():
        _head_pipeline(functools.partial(qk_stage, mask_history=False), softmax_stage, pv_stage)


def _prompt_attention(qkv, rest, bias):
    t = rest.shape[0]
    tq = ATT_Q_ROWS
    n_hist_blocks = BAND_ROWS // tq
    n_win = n_hist_blocks + 2
    span = BAND_ROWS + tq
    assert (QKV_K, QKV_V) == (0, 1)
    kv_block = (2, ATT_HEADS, tq, ATT_HEAD_DIM)

    def kv_spec(blk):
        return pl.BlockSpec(kv_block, lambda i: (0, 0, jnp.maximum(2 * i - n_hist_blocks + blk, 0), 0))

    in_specs = [pl.BlockSpec((1, ATT_HEADS, 2 * tq, ATT_HEAD_DIM), lambda i: (QKV_Q, 0, i, 0)),
                pl.BlockSpec((2 * tq, D_ATT), lambda i: (i, R_G // D_ATT))]
    in_specs += [kv_spec(blk) for blk in range(n_win)]
    in_specs += [pl.BlockSpec((ATT_HEADS, tq, span), lambda i: (0, 0, 0))]
    kern = functools.partial(_prompt_attn_kernel, n_hist_blocks=n_hist_blocks)
    return pl.pallas_call(
        kern,
        grid=(t // (2 * tq),),
        in_specs=in_specs,
        out_specs=pl.BlockSpec((2 * tq, D_ATT), lambda i: (i, 0)),
        out_shape=jax.ShapeDtypeStruct((t, D_ATT), BF16),
        scratch_shapes=[pltpu.VMEM((ATT_HEADS, 2 * tq, span), F32),
                        pltpu.VMEM((ATT_HEADS, 2 * tq, span), BF16),
                        pltpu.VMEM((ATT_HEADS, 2 * tq, ATT_HEAD_DIM), F32)],
        compiler_params=_compiler_params(("parallel",)),
        name="prompt_attention",
    )(qkv, rest, *([qkv] * n_win), bias)


def _sample_attn_kernel(q_ref, kn_ref, vn_ref, g_ref, kc_lo, kc_hi, vc_lo, vc_hi, bias_ref, o_ref,
                        s_scr, p_scr, inv_scr, *, q_rows):
    new_pad = LANES - q_rows
    zpad = jnp.zeros((new_pad, ATT_HEAD_DIM), BF16)

    def cached(lo_ref, hi_ref, h):
        ref = (lo_ref if h < SUBLANES else hi_ref).reshape(BAND_ROWS * SUBLANES, ATT_HEAD_DIM)
        return ref[pl.ds(h % SUBLANES, BAND_ROWS, stride=SUBLANES), :].astype(BF16)

    def qk_stage(h):
        qh = q_ref[0, h]
        s_scr[h, :, 0:BAND_ROWS] = lax.dot_general(qh, cached(kc_lo, kc_hi, h), (((1,), (1,)), ((), ())),
                                                   preferred_element_type=F32)
        k_new = jnp.concatenate([kn_ref[0, h], zpad], axis=0)
        s_scr[h, :, BAND_ROWS:] = lax.dot_general(qh, k_new, (((1,), (1,)), ((), ())), preferred_element_type=F32)

    def pv_stage(h):
        sl = slice(h * ATT_HEAD_DIM, (h + 1) * ATT_HEAD_DIM)
        v_new = jnp.concatenate([vn_ref[0, h], zpad], axis=0)
        o = jnp.dot(p_scr[h, :, 0:BAND_ROWS], cached(vc_lo, vc_hi, h), preferred_element_type=F32)
        o = o + jnp.dot(p_scr[h, :, BAND_ROWS:], v_new, preferred_element_type=F32)
        o_ref[:, sl] = (o * inv_scr[h] * _silu(g_ref[:, sl])).astype(o_ref.dtype)

    _head_pipeline(qk_stage, functools.partial(_softmax_stage, s_scr, p_scr, inv_scr, bias_ref, q_rows=q_rows),
                   pv_stage)


def _sample_attention(qkv, rest, cache_k, cache_v, bias, *, n_streams, q_rows):
    t = rest.shape[0]
    span_pad = BAND_ROWS + LANES
    kern = functools.partial(_sample_attn_kernel, q_rows=q_rows)

    def new_spec(which):
        return pl.BlockSpec((1, ATT_HEADS, q_rows, ATT_HEAD_DIM), lambda b: (which, 0, b, 0))

    def cache_spec(half):
        return pl.BlockSpec((None, BAND_ROWS, None, SUBLANES, ATT_HEAD_DIM), lambda b: (b, 0, half, 0, 0))

    return pl.pallas_call(
        kern,
        grid=(n_streams,),
        in_specs=[
            new_spec(QKV_Q), new_spec(QKV_K), new_spec(QKV_V),
            pl.BlockSpec((q_rows, D_ATT), lambda b: (b, R_G // D_ATT)),
            cache_spec(0), cache_spec(1), cache_spec(0), cache_spec(1),
            pl.BlockSpec((ATT_HEADS, q_rows, span_pad), lambda b: (0, 0, 0)),
        ],
        out_specs=pl.BlockSpec((q_rows, D_ATT), lambda b: (b, 0)),
        out_shape=jax.ShapeDtypeStruct((t, D_ATT), BF16),
        scratch_shapes=[pltpu.VMEM((ATT_HEADS, q_rows, span_pad), F32),
                        pltpu.VMEM((ATT_HEADS, q_rows, span_pad), BF16),
                        pltpu.VMEM((ATT_HEADS, q_rows, ATT_HEAD_DIM), F32)],
        compiler_params=_compiler_params(("parallel",)),
        name="sample_attention",
    )(qkv, qkv, qkv, rest, cache_k, cache_k, cache_v, cache_v, bias)


def _outproj_kernel(ys_ref, ya_ref, w1_ref, w2_ref, x_ref, o_ref):
    acc = jnp.dot(ys_ref[...], w1_ref[...], preferred_element_type=F32)
    acc = acc + jnp.dot(ya_ref[...], w2_ref[...], preferred_element_type=F32)
    o_ref[...] = x_ref[...] + acc


def _out_projection(y_ssm, y_att, w_out, x, *, tm, tn):
    t = x.shape[0]
    return pl.pallas_call(
        _outproj_kernel,
        grid=(t // tm, D_MODEL // tn),
        in_specs=[
            pl.BlockSpec((tm, D_SSM), lambda i, j: (i, 0)),
            pl.BlockSpec((tm, D_ATT), lambda i, j: (i, 0)),
            pl.BlockSpec((D_SSM, tn), lambda i, j: (0, j)),
            pl.BlockSpec((D_ATT, tn), lambda i, j: (D_SSM // D_ATT, j)),
            pl.BlockSpec((tm, tn), lambda i, j: (i, j)),
        ],
        out_specs=pl.BlockSpec((tm, tn), lambda i, j: (i, j)),
        out_shape=jax.ShapeDtypeStruct((t, D_MODEL), F32),
        compiler_params=_compiler_params(("parallel", "arbitrary")),
        name="out_projection",
    )(y_ssm, y_att, w_out, w_out, x)


def _pad_to(v, size, axis):
    pad = [(0, 0)] * v.ndim
    pad[axis] = (0, size - v.shape[axis])
    return jnp.pad(v, pad)


def _cast_w_in_kernel(main_ref, next_ref, wa_ref, wq_ref, wdt_ref, *, n_a, row_shift):
    j = pl.program_id(0)
    tn = main_ref.shape[0]
    chunk = 256

    @pl.when(j == n_a)
    def _():
        row_id = lax.broadcasted_iota(jnp.int32, (DT_COLS, main_ref.shape[1]), 0)
        wdt_ref[...] = jnp.where(row_id < row_shift, main_ref[0:DT_COLS, :], 0.0).T.astype(BF16)

    def put(rows_of, out_ref):
        for r0 in range(0, tn, chunk):
            out_ref[:, r0:r0 + chunk] = rows_of(r0).T.astype(BF16)

    @pl.when(j < n_a)
    def _():
        put(lambda r0: main_ref[r0:r0 + chunk, :], wa_ref)

    @pl.when(j >= n_a)
    def _():
        def shifted(r0):
            lo = r0 + row_shift
            if lo + chunk <= tn:
                return main_ref[lo:lo + chunk, :]
            return jnp.concatenate([main_ref[lo:tn, :], next_ref[0:lo + chunk - tn, :]], axis=0)

        put(shifted, wq_ref)


def _cast_w_in(w_in_t):
    tn = PROJ_TN
    n_a = OFF_DT // tn
    n_q = (4 * D_ATT) // tn
    row_shift = OFF_Q - OFF_DT
    assert OFF_DT % tn == 0 and row_shift % SUBLANES == 0 and 0 < row_shift < LANES
    kern = functools.partial(_cast_w_in_kernel, n_a=n_a, row_shift=row_shift)
    return pl.pallas_call(
        kern,
        grid=(n_a + n_q,),
        in_specs=[
            pl.BlockSpec((tn, D_MODEL), lambda j: (j, 0)),
            pl.BlockSpec((LANES, D_MODEL), lambda j: ((j + 1) * (tn // LANES), 0)),
        ],
        out_specs=[
            pl.BlockSpec((D_MODEL, tn), lambda j: (0, jnp.minimum(j, n_a - 1))),
            pl.BlockSpec((D_MODEL, tn), lambda j: (0, jnp.maximum(j - n_a, 0))),
            pl.BlockSpec((D_MODEL, DT_COLS), lambda j: (0, 0)),
        ],
        out_shape=[
            jax.ShapeDtypeStruct((D_MODEL, n_a * tn), BF16),
            jax.ShapeDtypeStruct((D_MODEL, n_q * tn), BF16),
            jax.ShapeDtypeStruct((D_MODEL, DT_COLS), BF16),
        ],
        compiler_params=_compiler_params(("arbitrary",)),
        name="cast_w_in",
    )(w_in_t, w_in_t)


def _prepare_params(norm_w, w_in, conv_w, conv_b, dt_bias, a_log, d_skip, ssm_norm_w, q_norm_w, k_norm_w, w_out):
    w_a, w_qkvg, w_dt = _cast_w_in(w_in.T)
    qk_w = jnp.concatenate([jnp.tile(q_norm_w, ATT_HEADS), jnp.tile(k_norm_w, ATT_HEADS)]).reshape(1, 2 * D_ATT)
    convp = _pad_to(jnp.concatenate([conv_w, conv_b[None]], axis=0), SUBLANES, 0)
    hp = _pad_to(_pad_to(jnp.stack([dt_bias, a_log]), LANES, 1), SUBLANES, 0)
    dexp = jnp.repeat(d_skip, SSM_HEADDIM).reshape(1, D_SSM)
    head_of_col = np.arange(D_SSM) // SSM_HEADDIM
    expand = (np.arange(LANES)[:, None] == head_of_col[None, :]).astype(np.float32)
    expand = jnp.asarray(np.tile(expand, (N_SPLIT, 1)), dtype=BF16)
    return dict(norm_w=norm_w.reshape(1, D_MODEL), w_qkvg=w_qkvg, w_a=w_a, w_dt=w_dt, qk_w=qk_w,
                convp=convp, hp=hp, dexp=dexp, ssm_nw=ssm_norm_w.reshape(1, D_SSM), expand=expand,
                w_out=w_out.astype(BF16))


def _rel_bias_table(rel_bias, q_rows, hist_rows, span_pad, band_chunk):
    n_heads = rel_bias.shape[0]
    period = span_pad + q_rows
    k = np.arange(period)
    rel = np.clip(hist_rows + q_rows - 1 - k, -REL_CLIP, REL_CLIP) + REL_CLIP
    onehot = jnp.asarray(np.eye(N_REL, dtype=np.float32)[:, rel])
    diag_row = jnp.dot(rel_bias.astype(F32), onehot, precision=lax.Precision.HIGHEST)
    flat = jnp.tile(diag_row, (1, q_rows))[:, :q_rows * (period - 1)]
    table = flat.reshape(n_heads, q_rows, period - 1)[:, :, q_rows - 1:q_rows - 1 + span_pad]
    i_idx = np.arange(q_rows)[:, None]
    j_idx = np.arange(span_pad)[None, :]
    valid = j_idx < hist_rows + q_rows
    if band_chunk is not None:
        start = (i_idx // band_chunk) * band_chunk
        valid = valid & (j_idx >= start) & (j_idx < start + hist_rows + band_chunk)
    return jnp.where(jnp.asarray(np.broadcast_to(valid, (q_rows, span_pad)))[None], table / ATT_SCALE, MASK_VALUE)


def _layer(x, hist, s0, cache_k, cache_v, prm, rel_bias, *, n_streams, seq, prompt, proj_tm):
    t = n_streams * seq
    x2 = x.reshape(t, D_MODEL)
    proj = _in_projection(x2, prm["norm_w"], prm["w_qkvg"], prm["w_a"], prm["w_dt"], prm["qk_w"],
                          tm=proj_tm, emit_kv=not prompt)
    if prompt:
        qkv, rest, dt = proj
        q_rows, sub_rows, n_chunks = SSD_CHUNK_ROWS, SSD_ROWS, seq // SSD_CHUNK_ROWS
        kv_rows = BAND_ROWS
        kv_new = _kv_rows(x2, prm["norm_w"], prm["w_qkvg"], prm["qk_w"], first_row=t - kv_rows, n_rows=kv_rows,
                          tm=512)
    else:
        qkv, rest, dt, kv_new = proj
        q_rows, sub_rows, n_chunks = seq, seq, 1
        kv_rows = t
    y_ssm, s_new = _ssd_branch(rest, dt, hist, s0, prm["convp"], prm["hp"], prm["dexp"], prm["ssm_nw"],
                               prm["expand"], n_streams=n_streams, q_rows=q_rows, sub_rows=sub_rows,
                               n_chunks=n_chunks)
    if prompt:
        bias = _rel_bias_table(rel_bias, ATT_Q_ROWS, BAND_ROWS, BAND_ROWS + ATT_Q_ROWS, CHUNK)
        y_att = _prompt_attention(qkv, rest, bias)
    else:
        bias = _rel_bias_table(rel_bias, seq, BAND_ROWS, BAND_ROWS + LANES, None)
        y_att = _sample_attention(qkv, rest, cache_k, cache_v, bias, n_streams=n_streams, q_rows=seq)
    y = _out_projection(y_ssm, y_att, prm["w_out"], x2, tm=1024, tn=1024)
    new_conv = rest.reshape(n_streams, seq, R_COLS)[:, seq - (CONV_WIDTH - 1):, R_X:]
    kv_streams = kv_rows // n_streams
    kh = kv_new[:, :D_ATT].reshape(n_streams, kv_streams, ATT_HEADS, ATT_HEAD_DIM)
    vh = kv_new[:, D_ATT:].reshape(n_streams, kv_streams, ATT_HEADS, ATT_HEAD_DIM)
    new_ssm = s_new.reshape(n_streams, SSM_HEADS, SSM_HEADDIM, D_STATE)
    return y.reshape(n_streams, seq, D_MODEL), new_conv, new_ssm, kh, vh


def kernel(x_prompt, x_sample, state_conv, state_ssm, cache_k, cache_v, norm_w, w_in, conv_w, conv_b, dt_bias, a_log, d_skip, ssm_norm_w, q_norm_w, k_norm_w, rel_bias, w_out):
    bp, lp, _ = x_prompt.shape
    bs, ls, _ = x_sample.shape
    assert bp == 1 and lp % 1024 == 0 and lp >= BAND_ROWS
    assert ls % (2 * SUBLANES) == 0 and ls <= LANES and (bs * ls) % 1024 == 0 and cache_k.shape[2] == BAND_ROWS
    assert norm_w.shape[0] == 1
    prm = _prepare_params(norm_w[0], w_in[0], conv_w[0], conv_b[0], dt_bias[0], a_log[0], d_skip[0],
                          ssm_norm_w[0], q_norm_w[0], k_norm_w[0], w_out[0])
    rb = rel_bias[0]

    zero_hist = jnp.zeros((bp, SUBLANES, CONV_DIM), F32)
    zero_state = jnp.zeros((bp, D_SSM, D_STATE), F32)
    yp, c1, s1, k1, v1 = _layer(x_prompt, zero_hist, zero_state, None, None, prm, rb,
                                n_streams=bp, seq=lp, prompt=True, proj_tm=1024)

    hist = jnp.pad(state_conv[0], ((0, 0), (SUBLANES - (CONV_WIDTH - 1), 0), (0, 0)))
    s0 = state_ssm[0].reshape(bs, D_SSM, D_STATE)
    half_heads = (bs, BAND_ROWS, ATT_HEADS // SUBLANES, SUBLANES, ATT_HEAD_DIM)
    ck = cache_k[0].reshape(half_heads)
    cv = cache_v[0].reshape(half_heads)
    ys, c2, s2, k2, v2 = _layer(x_sample, hist, s0, ck, cv, prm, rb, n_streams=bs, seq=ls, prompt=False,
                                proj_tm=512)

    return (yp, ys, c1[None], s1[None], k1[None], v1[None], c2[None], s2[None], k2[None], v2[None])
```

```python
import functools

import jax
import jax.numpy as jnp
import numpy as np
from jax import lax
from jax.experimental import pallas as pl
from jax.experimental.pallas import tpu as pltpu

F32 = jnp.float32
BF16 = jnp.bfloat16

D_MODEL = 2048
CHUNK = 64
LEFT_CHUNKS = 8
BAND_ROWS = LEFT_CHUNKS * CHUNK
D_SSM = 2048
D_ATT = 2048
SSM_HEADDIM = 64
SSM_HEADS = 32
SSM_GROUPS = 4
HEADS_PER_GROUP = SSM_HEADS // SSM_GROUPS
GROUP_COLS = HEADS_PER_GROUP * SSM_HEADDIM
D_STATE = 128
BC_COLS = 2 * SSM_GROUPS * D_STATE
CONV_WIDTH = 4
CONV_DIM = D_SSM + BC_COLS
ATT_HEAD_DIM = 128
ATT_HEADS = 16
REL_CLIP = 128
N_REL = 2 * REL_CLIP + 1
EPS = 1e-6
OFF_Z = 0
OFF_XBC = OFF_Z + D_SSM
OFF_DT = OFF_XBC + CONV_DIM
OFF_Q = OFF_DT + SSM_HEADS
OFF_K = OFF_Q + D_ATT
OFF_V = OFF_K + D_ATT
OFF_G = OFF_V + D_ATT

LANES = 128
SUBLANES = 8
VMEM_LIMIT_BYTES = 56 * 1024 * 1024

R_G = 0
R_Z = R_G + D_ATT
R_X = R_Z + D_SSM
R_BC = R_X + D_SSM
R_COLS = R_BC + BC_COLS
DT_COLS = LANES
QKV_K, QKV_V, QKV_Q = 0, 1, 2
N_SPLIT = 3

PROJ_TN = 1024
PROMPT_PROJ_TM = 1024
SAMPLE_PROJ_TM = 512
KV_ROWS_TM = 512
OUT_PROJ_TM = 1024
OUT_PROJ_TN = 1024
CAST_CHUNK = 256
SSD_ROWS = 128
SSD_CHUNK_ROWS = 512
ATT_Q_ROWS = 128
SOFTMAX_SLAB = 32
HEAD_PIPELINE_SKEW = 2
MASK_VALUE = -1e30
ATT_SCALE = ATT_HEAD_DIM ** -0.5
LOG2E = 1.4426950408889634


def _silu(v):
    return v * (1.0 / (1.0 + jnp.exp2(v * (-LOG2E))))


def _compiler_params(semantics):
    return pltpu.CompilerParams(dimension_semantics=semantics, vmem_limit_bytes=VMEM_LIMIT_BYTES)


def _normed_rows(x_ref, nw_ref):
    x = x_ref[...]
    ms = jnp.mean(x * x, axis=-1, keepdims=True)
    return (x * lax.rsqrt(ms + EPS) * nw_ref[...]).astype(BF16)


def _head_norm(a, use_norm, w):
    r = lax.rsqrt(jnp.mean(a * a, axis=-1, keepdims=True) + EPS)
    return a * jnp.where(use_norm, r, 1.0) * jnp.where(use_norm, w, 1.0)


def _inproj_kernel(*refs, n_qk, n_qkv, n_g, emit_kv):
    x_ref, nw_ref, wqkvg_ref, wa_ref, wdt_ref, qkw_ref, qkv_ref, rest_ref, dt_ref = refs[:9]
    kv_ref = refs[9] if emit_kv else None
    h_scr = refs[-1]
    j = pl.program_id(1)
    heads_per_tile = PROJ_TN // ATT_HEAD_DIM

    @pl.when(j == 0)
    def _():
        h = _normed_rows(x_ref, nw_ref)
        h_scr[...] = h
        dt_ref[...] = jnp.dot(h, wdt_ref[...], preferred_element_type=F32)

    def qkv_tile(use_norm):
        acc = jnp.dot(h_scr[...], wqkvg_ref[...], preferred_element_type=F32)
        for hh in range(heads_per_tile):
            sl = slice(hh * ATT_HEAD_DIM, (hh + 1) * ATT_HEAD_DIM)
            head = acc[:, sl]
            if use_norm:
                r = lax.rsqrt(jnp.mean(head * head, axis=-1, keepdims=True) + EPS)
                head = head * r * qkw_ref[:, sl]
            qkv_ref[0, hh] = head.astype(BF16)
            if emit_kv:
                kv_ref[:, sl] = head

    pl.when(j < n_qk)(functools.partial(qkv_tile, True))
    pl.when(jnp.logical_and(j >= n_qk, j < n_qkv))(functools.partial(qkv_tile, False))

    @pl.when(jnp.logical_and(j >= n_qkv, j < n_qkv + n_g))
    def _():
        rest_ref[...] = jnp.dot(h_scr[...], wqkvg_ref[...], preferred_element_type=F32)

    @pl.when(j >= n_qkv + n_g)
    def _():
        rest_ref[...] = jnp.dot(h_scr[...], wa_ref[...], preferred_element_type=F32)


def _in_projection(x, norm_w, w_qkvg, w_a, w_dt, qk_w, *, tm, emit_kv):
    t = x.shape[0]
    tn = PROJ_TN
    tiles_per_proj = D_ATT // tn
    n_qk, n_qkv, n_g = 2 * tiles_per_proj, 3 * tiles_per_proj, tiles_per_proj
    n_a = (D_SSM + CONV_DIM) // tn
    heads_per_tile = tn // ATT_HEAD_DIM
    grid = (t // tm, n_qkv + n_g + n_a)
    kern = functools.partial(_inproj_kernel, n_qk=n_qk, n_qkv=n_qkv, n_g=n_g, emit_kv=emit_kv)

    def qkv_index(i, j):
        jj = jnp.minimum(j, n_qkv - 1)
        plane = (jj // tiles_per_proj + QKV_Q) % 3
        return (plane, jj % tiles_per_proj, i, 0)

    out_specs = [
        pl.BlockSpec((1, heads_per_tile, tm, ATT_HEAD_DIM), qkv_index),
        pl.BlockSpec((tm, tn), lambda i, j: (i, jnp.maximum(j - n_qkv, 0))),
        pl.BlockSpec((tm, DT_COLS), lambda i, j: (i, 0)),
    ]
    out_shape = [
        jax.ShapeDtypeStruct((3, ATT_HEADS, t, ATT_HEAD_DIM), BF16),
        jax.ShapeDtypeStruct((t, R_COLS), F32),
        jax.ShapeDtypeStruct((t, DT_COLS), F32),
    ]
    if emit_kv:
        n_kv = n_qkv - tiles_per_proj
        out_specs.append(pl.BlockSpec((tm, tn), lambda i, j: (i, jnp.clip(j - tiles_per_proj, 0, n_kv - 1))))
        out_shape.append(jax.ShapeDtypeStruct((t, 2 * D_ATT), F32))
    return pl.pallas_call(
        kern,
        grid=grid,
        in_specs=[
            pl.BlockSpec((tm, D_MODEL), lambda i, j: (i, 0)),
            pl.BlockSpec((1, D_MODEL), lambda i, j: (0, 0)),
            pl.BlockSpec((D_MODEL, tn), lambda i, j: (0, jnp.minimum(j, n_qkv + n_g - 1))),
            pl.BlockSpec((D_MODEL, tn), lambda i, j: (0, jnp.clip(j - n_qkv - n_g, 0, n_a - 1))),
            pl.BlockSpec((D_MODEL, DT_COLS), lambda i, j: (0, 0)),
            pl.BlockSpec((1, tn), lambda i, j: (0, jnp.minimum(j, n_qk - 1))),
        ],
        out_specs=out_specs,
        out_shape=out_shape,
        scratch_shapes=[pltpu.VMEM((tm, D_MODEL), BF16)],
        compiler_params=_compiler_params(("parallel", "arbitrary")),
        name="in_projection",
    )(x, norm_w, w_qkvg, w_a, w_dt, qk_w)


def _kv_rows_kernel(x_ref, nw_ref, w_ref, qkw_ref, o_ref, h_scr, *, n_k):
    j = pl.program_id(1)

    @pl.when(j == 0)
    def _():
        h_scr[...] = _normed_rows(x_ref, nw_ref)

    acc = jnp.dot(h_scr[...], w_ref[...], preferred_element_type=F32)
    for hh in range(PROJ_TN // ATT_HEAD_DIM):
        sl = slice(hh * ATT_HEAD_DIM, (hh + 1) * ATT_HEAD_DIM)
        o_ref[:, sl] = _head_norm(acc[:, sl], j < n_k, qkw_ref[:, sl])


def _kv_rows(x, norm_w, w_qkvg, qk_w, *, first_row, n_rows, tm):
    tn = PROJ_TN
    tiles_per_proj = D_ATT // tn
    row0 = first_row // tm
    kern = functools.partial(_kv_rows_kernel, n_k=tiles_per_proj)
    return pl.pallas_call(
        kern,
        grid=(n_rows // tm, 2 * tiles_per_proj),
        in_specs=[
            pl.BlockSpec((tm, D_MODEL), lambda i, j: (row0 + i, 0)),
            pl.BlockSpec((1, D_MODEL), lambda i, j: (0, 0)),
            pl.BlockSpec((D_MODEL, tn), lambda i, j: (0, tiles_per_proj + j)),
            pl.BlockSpec((1, tn), lambda i, j: (0, jnp.minimum(tiles_per_proj + j, 2 * tiles_per_proj - 1))),
        ],
        out_specs=pl.BlockSpec((tm, tn), lambda i, j: (i, j)),
        out_shape=jax.ShapeDtypeStruct((n_rows, 2 * D_ATT), F32),
        scratch_shapes=[pltpu.VMEM((tm, D_MODEL), BF16)],
        compiler_params=_compiler_params(("parallel", "arbitrary")),
        name="kv_rows",
    )(x, norm_w, w_qkvg, qk_w)


def _transpose_rows_to_lanes(v):
    q = v.shape[0]
    if q < LANES:
        v = jnp.concatenate([v, jnp.zeros((LANES - q, LANES), v.dtype)], axis=0)
    return v.T[:, 0:q]


def _split_bf16(v):
    pieces = []
    rem = v
    for _ in range(N_SPLIT):
        piece = rem.astype(BF16)
        pieces.append(piece)
        rem = rem - piece.astype(F32)
    return jnp.concatenate(pieces, axis=1)


def _ssd_kernel(x_ref, bc_ref, z_ref, dt_ref, hist_ref, s0_ref, convp_ref, hp_ref, dexp_ref, nw_ref, expand_ref,
                y_ref, sout_ref, conv_scr, act_scr, st_scr, yd_scr, exp_scr, *, q_rows, sub_rows, n_chunks):
    c = pl.program_id(1)
    q = q_rows

    @pl.when(c == 0)
    def _():
        conv_scr[0:SUBLANES, :] = hist_ref[0]
        for g in range(SSM_GROUPS):
            st_scr[g] = s0_ref[0, g * GROUP_COLS:(g + 1) * GROUP_COLS, :].T

    conv_scr[SUBLANES:SUBLANES + q, 0:D_SSM] = x_ref[...]
    conv_scr[SUBLANES:SUBLANES + q, D_SSM:] = bc_ref[...]

    for c0 in range(0, CONV_DIM, GROUP_COLS):
        cols = slice(c0, c0 + GROUP_COLS)
        xp = conv_scr[:, cols]
        conv = convp_ref[CONV_WIDTH:CONV_WIDTH + 1, cols] + convp_ref[CONV_WIDTH - 1:CONV_WIDTH, cols] * xp[SUBLANES:]
        for shift in range(1, CONV_WIDTH):
            tap = CONV_WIDTH - 1 - shift
            conv = conv + convp_ref[tap:tap + 1, cols] * pltpu.roll(xp, shift, axis=0)[SUBLANES:]
        act_scr[:, cols] = _silu(conv)
    conv_scr[0:SUBLANES, :] = conv_scr[q:q + SUBLANES, :]

    s = sub_rows
    ii = lax.broadcasted_iota(jnp.int32, (s, s), 0)
    jj = lax.broadcasted_iota(jnp.int32, (s, s), 1)
    causal = ii >= jj
    for sc in range(q // s):
        rows = slice(sc * s, (sc + 1) * s)
        v = dt_ref[rows, :] + hp_ref[0:1, :]
        dt = jnp.maximum(v, 0.0) + jnp.log1p(jnp.exp(-jnp.abs(v)))
        a = dt * (-jnp.exp(hp_ref[1:2, :]))
        acum = jnp.dot(causal.astype(F32), a, precision=lax.Precision.HIGHEST, preferred_element_type=F32)
        acum2 = acum * LOG2E
        row_t = _transpose_rows_to_lanes(acum2 - jnp.log(dt) * LOG2E)
        last = acum[s - 1:s, :]
        factors = jnp.concatenate(
            [jnp.exp(acum),
             jnp.exp(last - acum) * dt,
             jnp.broadcast_to(jnp.exp(last), (SUBLANES, LANES))], axis=0)
        exp_scr[sc] = jnp.dot(_split_bf16(factors), expand_ref[...], preferred_element_type=F32)

        for g in range(SSM_GROUPS):
            cols = slice(g * GROUP_COLS, (g + 1) * GROUP_COLS)
            xact = act_scr[rows, cols]
            bact = act_scr[rows, D_SSM + g * D_STATE:D_SSM + (g + 1) * D_STATE].astype(BF16)
            cact = act_scr[rows, D_SSM + (SSM_GROUPS + g) * D_STATE:D_SSM + (SSM_GROUPS + g + 1) * D_STATE].astype(BF16)
            cb = lax.dot_general(cact, bact, (((1,), (1,)), ((), ())), preferred_element_type=F32)
            st = st_scr[g]
            y_off = jnp.dot(cact, st.astype(BF16), preferred_element_type=F32)
            for r in range(HEADS_PER_GROUP):
                h = g * HEADS_PER_GROUP + r
                m = cb * jnp.exp2(jnp.where(causal, acum2[:, h:h + 1] - row_t[h:h + 1, :], -jnp.inf))
                xh = xact[:, r * SSM_HEADDIM:(r + 1) * SSM_HEADDIM]
                yd_scr[sc, :, h * SSM_HEADDIM:(h + 1) * SSM_HEADDIM] = jnp.dot(
                    m.astype(BF16), xh.astype(BF16), preferred_element_type=F32)
            y = yd_scr[sc, :, cols] + y_off * exp_scr[sc, 0:s, cols] + dexp_ref[:, cols] * xact
            xw = (xact * exp_scr[sc, s:2 * s, cols]).astype(BF16)
            upd = lax.dot_general(bact, xw, (((0,), (0,)), ((), ())), preferred_element_type=F32)
            st_scr[g] = st * exp_scr[sc, 2 * s:2 * s + 1, cols] + upd

            yg = y * _silu(z_ref[rows, cols])
            rn = lax.rsqrt(jnp.mean(yg * yg, axis=-1, keepdims=True) + EPS)
            y_ref[rows, cols] = (yg * rn * nw_ref[:, cols]).astype(y_ref.dtype)

    @pl.when(c == n_chunks - 1)
    def _():
        for g in range(SSM_GROUPS):
            sout_ref[0, g * GROUP_COLS:(g + 1) * GROUP_COLS, :] = st_scr[g].T


def _ssd_branch(rest, dt, hist, s0, convp, hp, dexp, ssm_nw, expand, *, n_streams, q_rows, sub_rows, n_chunks):
    t = rest.shape[0]
    n_sub = q_rows // sub_rows
    assert n_sub * sub_rows == q_rows

    def rows(b, c):
        return b * n_chunks + c

    def const(shape):
        return pl.BlockSpec(shape, lambda b, c: (0,) * len(shape))

    kern = functools.partial(_ssd_kernel, q_rows=q_rows, sub_rows=sub_rows, n_chunks=n_chunks)
    return pl.pallas_call(
        kern,
        grid=(n_streams, n_chunks),
        in_specs=[
            pl.BlockSpec((q_rows, D_SSM), lambda b, c: (rows(b, c), R_X // D_SSM)),
            pl.BlockSpec((q_rows, BC_COLS), lambda b, c: (rows(b, c), R_BC // BC_COLS)),
            pl.BlockSpec((q_rows, D_SSM), lambda b, c: (rows(b, c), R_Z // D_SSM)),
            pl.BlockSpec((q_rows, DT_COLS), lambda b, c: (rows(b, c), 0)),
            pl.BlockSpec((1, SUBLANES, CONV_DIM), lambda b, c: (b, 0, 0)),
            pl.BlockSpec((1, D_SSM, D_STATE), lambda b, c: (b, 0, 0)),
            const((SUBLANES, CONV_DIM)),
            const((SUBLANES, LANES)),
            const((1, D_SSM)),
            const((1, D_SSM)),
            const((N_SPLIT * LANES, D_SSM)),
        ],
        out_specs=[
            pl.BlockSpec((q_rows, D_SSM), lambda b, c: (rows(b, c), 0)),
            pl.BlockSpec((1, D_SSM, D_STATE), lambda b, c: (b, 0, 0)),
        ],
        out_shape=[
            jax.ShapeDtypeStruct((t, D_SSM), BF16),
            jax.ShapeDtypeStruct((n_streams, D_SSM, D_STATE), F32),
        ],
        scratch_shapes=[
            pltpu.VMEM((SUBLANES + q_rows, CONV_DIM), F32),
            pltpu.VMEM((q_rows, CONV_DIM), F32),
            pltpu.VMEM((SSM_GROUPS, D_STATE, GROUP_COLS), F32),
            pltpu.VMEM((n_sub, sub_rows, D_SSM), F32),
            pltpu.VMEM((n_sub, 2 * sub_rows + SUBLANES, D_SSM), F32),
        ],
        compiler_params=_compiler_params(("parallel", "arbitrary")),
        name="ssd_branch",
    )(rest, rest, rest, dt, hist, s0, convp, hp, dexp, ssm_nw, expand)


def _softmax_stage(s_scr, p_scr, inv_scr, bias_ref, h, q_rows):
    exp2_scale = ATT_SCALE * LOG2E
    slab = min(SOFTMAX_SLAB, q_rows)
    bias_rows = bias_ref.shape[1]
    for r0 in range(0, q_rows, slab):
        b0 = r0 % bias_rows
        u = s_scr[h, r0:r0 + slab, :] + bias_ref[h, b0:b0 + slab, :]
        m = jnp.max(u, axis=-1, keepdims=True)
        e = jnp.exp2((u - m) * exp2_scale)
        p_scr[h, r0:r0 + slab, :] = e.astype(BF16)
        inv = 1.0 / jnp.sum(e, axis=-1, keepdims=True)
        inv_scr[h, r0:r0 + slab, :] = jnp.broadcast_to(inv, (slab, ATT_HEAD_DIM))


def _head_pipeline(qk_stage, softmax_stage, pv_stage):
    skew = HEAD_PIPELINE_SKEW
    for step in range(ATT_HEADS + 2 * skew):
        if step < ATT_HEADS:
            qk_stage(step)
        if skew <= step < ATT_HEADS + skew:
            softmax_stage(step - skew)
        if step >= 2 * skew:
            pv_stage(step - 2 * skew)


def _prompt_attn_kernel(*refs, n_hist_blocks):
    n_win = n_hist_blocks + 2
    tq = ATT_Q_ROWS
    q_ref, g_ref = refs[0], refs[1]
    kv_refs = refs[2:2 + n_win]
    bias_ref, o_ref, s_scr, p_scr, inv_scr = refs[2 + n_win:]
    first_block = 2 * pl.program_id(0) - n_hist_blocks

    def block_rows(blk):
        lo = 0 if blk <= n_hist_blocks else tq
        hi = 2 * tq if blk >= 1 else tq
        return lo, hi

    def qk_stage(h, mask_history):
        for blk in range(n_win):
            lo, hi = block_rows(blk)
            s = lax.dot_general(q_ref[0, h, lo:hi, :], kv_refs[blk][QKV_K, h], (((1,), (1,)), ((), ())),
                                preferred_element_type=F32)
            if mask_history and blk < n_hist_blocks:
                s = jnp.where(first_block + blk >= 0, s, MASK_VALUE)
            if lo == 0:
                s_scr[h, 0:tq, blk * tq:(blk + 1) * tq] = s[0:tq]
            if hi == 2 * tq:
                s_scr[h, tq:2 * tq, (blk - 1) * tq:blk * tq] = s[tq - lo:2 * tq - lo]

    def pv_stage(h):
        sl = slice(h * ATT_HEAD_DIM, (h + 1) * ATT_HEAD_DIM)
        o_first = jnp.zeros((tq, ATT_HEAD_DIM), F32)
        o_second = jnp.zeros((tq, ATT_HEAD_DIM), F32)
        for blk in range(n_win):
            lo, hi = block_rows(blk)
            parts = []
            if lo == 0:
                parts.append(p_scr[h, 0:tq, blk * tq:(blk + 1) * tq])
            if hi == 2 * tq:
                parts.append(p_scr[h, tq:2 * tq, (blk - 1) * tq:blk * tq])
            p = parts[0] if len(parts) == 1 else jnp.concatenate(parts, axis=0)
            o = jnp.dot(p, kv_refs[blk][QKV_V, h], preferred_element_type=F32)
            if lo == 0:
                o_first = o_first + o[0:tq]
            if hi == 2 * tq:
                o_second = o_second + o[tq - lo:2 * tq - lo]
        o = jnp.concatenate([o_first, o_second], axis=0)
        o_ref[:, sl] = (o * inv_scr[h] * _silu(g_ref[:, sl])).astype(o_ref.dtype)

    softmax_stage = functools.partial(_softmax_stage, s_scr, p_scr, inv_scr, bias_ref, q_rows=2 * tq)

    @pl.when(first_block < 0)
    def _():
        _head_pipeline(functools.partial(qk_stage, mask_history=True), softmax_stage, pv_stage)

    @pl.when(first_block >= 0)
    def _():
        _head_pipeline(functools.partial(qk_stage, mask_history=False), softmax_stage, pv_stage)


def _prompt_attention(qkv, rest, bias):
    t = rest.shape[0]
    tq = ATT_Q_ROWS
    n_hist_blocks = BAND_ROWS // tq
    n_win = n_hist_blocks + 2
    span = BAND_ROWS + tq
    assert (QKV_K, QKV_V) == (0, 1)
    kv_block = (2, ATT_HEADS, tq, ATT_HEAD_DIM)

    def kv_spec(blk):
        return pl.BlockSpec(kv_block, lambda i: (0, 0, jnp.maximum(2 * i - n_hist_blocks + blk, 0), 0))

    in_specs = [pl.BlockSpec((1, ATT_HEADS, 2 * tq, ATT_HEAD_DIM), lambda i: (QKV_Q, 0, i, 0)),
                pl.BlockSpec((2 * tq, D_ATT), lambda i: (i, R_G // D_ATT))]
    in_specs += [kv_spec(blk) for blk in range(n_win)]
    in_specs += [pl.BlockSpec((ATT_HEADS, tq, span), lambda i: (0, 0, 0))]
    kern = functools.partial(_prompt_attn_kernel, n_hist_blocks=n_hist_blocks)
    return pl.pallas_call(
        kern,
        grid=(t // (2 * tq),),
        in_specs=in_specs,
        out_specs=pl.BlockSpec((2 * tq, D_ATT), lambda i: (i, 0)),
        out_shape=jax.ShapeDtypeStruct((t, D_ATT), BF16),
        scratch_shapes=[pltpu.VMEM((ATT_HEADS, 2 * tq, span), F32),
                        pltpu.VMEM((ATT_HEADS, 2 * tq, span), BF16),
                        pltpu.VMEM((ATT_HEADS, 2 * tq, ATT_HEAD_DIM), F32)],
        compiler_params=_compiler_params(("parallel",)),
        name="prompt_attention",
    )(qkv, rest, *([qkv] * n_win), bias)


def _sample_attn_kernel(q_ref, kn_ref, vn_ref, g_ref, kc_lo, kc_hi, vc_lo, vc_hi, bias_ref, o_ref,
                        s_scr, p_scr, inv_scr, *, q_rows):
    new_pad = LANES - q_rows
    zpad = jnp.zeros((new_pad, ATT_HEAD_DIM), BF16)

    def cached(lo_ref, hi_ref, h):
        ref = (lo_ref if h < SUBLANES else hi_ref).reshape(BAND_ROWS * SUBLANES, ATT_HEAD_DIM)
        return ref[pl.ds(h % SUBLANES, BAND_ROWS, stride=SUBLANES), :].astype(BF16)

    def qk_stage(h):
        qh = q_ref[0, h]
        s_scr[h, :, 0:BAND_ROWS] = lax.dot_general(qh, cached(kc_lo, kc_hi, h), (((1,), (1,)), ((), ())),
                                                   preferred_element_type=F32)
        k_new = jnp.concatenate([kn_ref[0, h], zpad], axis=0)
        s_scr[h, :, BAND_ROWS:] = lax.dot_general(qh, k_new, (((1,), (1,)), ((), ())), preferred_element_type=F32)

    def pv_stage(h):
        sl = slice(h * ATT_HEAD_DIM, (h + 1) * ATT_HEAD_DIM)
        v_new = jnp.concatenate([vn_ref[0, h], zpad], axis=0)
        o = jnp.dot(p_scr[h, :, 0:BAND_ROWS], cached(vc_lo, vc_hi, h), preferred_element_type=F32)
        o = o + jnp.dot(p_scr[h, :, BAND_ROWS:], v_new, preferred_element_type=F32)
        o_ref[:, sl] = (o * inv_scr[h] * _silu(g_ref[:, sl])).astype(o_ref.dtype)

    _head_pipeline(qk_stage, functools.partial(_softmax_stage, s_scr, p_scr, inv_scr, bias_ref, q_rows=q_rows),
                   pv_stage)


def _sample_attention(qkv, rest, cache_k, cache_v, bias, *, n_streams, q_rows):
    t = rest.shape[0]
    span_pad = BAND_ROWS + LANES
    kern = functools.partial(_sample_attn_kernel, q_rows=q_rows)

    def new_spec(which):
        return pl.BlockSpec((1, ATT_HEADS, q_rows, ATT_HEAD_DIM), lambda b: (which, 0, b, 0))

    def cache_spec(half):
        return pl.BlockSpec((None, BAND_ROWS, None, SUBLANES, ATT_HEAD_DIM), lambda b: (b, 0, half, 0, 0))

    return pl.pallas_call(
        kern,
        grid=(n_streams,),
        in_specs=[
            new_spec(QKV_Q), new_spec(QKV_K), new_spec(QKV_V),
            pl.BlockSpec((q_rows, D_ATT), lambda b: (b, R_G // D_ATT)),
            cache_spec(0), cache_spec(1), cache_spec(0), cache_spec(1),
            pl.BlockSpec((ATT_HEADS, q_rows, span_pad), lambda b: (0, 0, 0)),
        ],
        out_specs=pl.BlockSpec((q_rows, D_ATT), lambda b: (b, 0)),
        out_shape=jax.ShapeDtypeStruct((t, D_ATT), BF16),
        scratch_shapes=[pltpu.VMEM((ATT_HEADS, q_rows, span_pad), F32),
                        pltpu.VMEM((ATT_HEADS, q_rows, span_pad), BF16),
                        pltpu.VMEM((ATT_HEADS, q_rows, ATT_HEAD_DIM), F32)],
        compiler_params=_compiler_params(("parallel",)),
        name="sample_attention",
    )(qkv, qkv, qkv, rest, cache_k, cache_k, cache_v, cache_v, bias)


def _outproj_kernel(ys_ref, ya_ref, w1_ref, w2_ref, x_ref, o_ref):
    acc = jnp.dot(ys_ref[...], w1_ref[...], preferred_element_type=F32)
    acc = acc + jnp.dot(ya_ref[...], w2_ref[...], preferred_element_type=F32)
    o_ref[...] = x_ref[...] + acc


def _out_projection(y_ssm, y_att, w_out, x, *, tm, tn):
    t = x.shape[0]
    return pl.pallas_call(
        _outproj_kernel,
        grid=(t // tm, D_MODEL // tn),
        in_specs=[
            pl.BlockSpec((tm, D_SSM), lambda i, j: (i, 0)),
            pl.BlockSpec((tm, D_ATT), lambda i, j: (i, 0)),
            pl.BlockSpec((D_SSM, tn), lambda i, j: (0, j)),
            pl.BlockSpec((D_ATT, tn), lambda i, j: (D_SSM // D_ATT, j)),
            pl.BlockSpec((tm, tn), lambda i, j: (i, j)),
        ],
        out_specs=pl.BlockSpec((tm, tn), lambda i, j: (i, j)),
        out_shape=jax.ShapeDtypeStruct((t, D_MODEL), F32),
        compiler_params=_compiler_params(("parallel", "arbitrary")),
        name="out_projection",
    )(y_ssm, y_att, w_out, w_out, x)


def _pad_to(v, size, axis):
    pad = [(0, 0)] * v.ndim
    pad[axis] = (0, size - v.shape[axis])
    return jnp.pad(v, pad)


def _cast_w_in_kernel(main_ref, next_ref, wa_ref, wq_ref, wdt_ref, *, n_a, row_shift):
    j = pl.program_id(0)
    tn = main_ref.shape[0]
    chunk = CAST_CHUNK

    @pl.when(j == n_a)
    def _():
        row_id = lax.broadcasted_iota(jnp.int32, (DT_COLS, main_ref.shape[1]), 0)
        wdt_ref[...] = jnp.where(row_id < row_shift, main_ref[0:DT_COLS, :], 0.0).T.astype(BF16)

    def put(rows_of, out_ref):
        for r0 in range(0, tn, chunk):
            out_ref[:, r0:r0 + chunk] = rows_of(r0).T.astype(BF16)

    @pl.when(j < n_a)
    def _():
        put(lambda r0: main_ref[r0:r0 + chunk, :], wa_ref)

    @pl.when(j >= n_a)
    def _():
        def shifted(r0):
            lo = r0 + row_shift
            if lo + chunk <= tn:
                return main_ref[lo:lo + chunk, :]
            return jnp.concatenate([main_ref[lo:tn, :], next_ref[0:lo + chunk - tn, :]], axis=0)

        put(shifted, wq_ref)


def _cast_w_in(w_in_t):
    tn = PROJ_TN
    n_a = OFF_DT // tn
    n_q = (4 * D_ATT) // tn
    row_shift = OFF_Q - OFF_DT
    assert OFF_DT % tn == 0 and row_shift % SUBLANES == 0 and 0 < row_shift < LANES
    kern = functools.partial(_cast_w_in_kernel, n_a=n_a, row_shift=row_shift)
    return pl.pallas_call(
        kern,
        grid=(n_a + n_q,),
        in_specs=[
            pl.BlockSpec((tn, D_MODEL), lambda j: (j, 0)),
            pl.BlockSpec((LANES, D_MODEL), lambda j: ((j + 1) * (tn // LANES), 0)),
        ],
        out_specs=[
            pl.BlockSpec((D_MODEL, tn), lambda j: (0, jnp.minimum(j, n_a - 1))),
            pl.BlockSpec((D_MODEL, tn), lambda j: (0, jnp.maximum(j - n_a, 0))),
            pl.BlockSpec((D_MODEL, DT_COLS), lambda j: (0, 0)),
        ],
        out_shape=[
            jax.ShapeDtypeStruct((D_MODEL, n_a * tn), BF16),
            jax.ShapeDtypeStruct((D_MODEL, n_q * tn), BF16),
            jax.ShapeDtypeStruct((D_MODEL, DT_COLS), BF16),
        ],
        compiler_params=_compiler_params(("arbitrary",)),
        name="cast_w_in",
    )(w_in_t, w_in_t)


def _prepare_params(norm_w, w_in, conv_w, conv_b, dt_bias, a_log, d_skip, ssm_norm_w, q_norm_w, k_norm_w, w_out):
    w_a, w_qkvg, w_dt = _cast_w_in(w_in.T)
    qk_w = jnp.concatenate([jnp.tile(q_norm_w, ATT_HEADS), jnp.tile(k_norm_w, ATT_HEADS)]).reshape(1, 2 * D_ATT)
    convp = _pad_to(jnp.concatenate([conv_w, conv_b[None]], axis=0), SUBLANES, 0)
    hp = _pad_to(_pad_to(jnp.stack([dt_bias, a_log]), LANES, 1), SUBLANES, 0)
    dexp = jnp.repeat(d_skip, SSM_HEADDIM).reshape(1, D_SSM)
    head_of_col = np.arange(D_SSM) // SSM_HEADDIM
    expand = (np.arange(LANES)[:, None] == head_of_col[None, :]).astype(np.float32)
    expand = jnp.asarray(np.tile(expand, (N_SPLIT, 1)), dtype=BF16)
    return dict(norm_w=norm_w.reshape(1, D_MODEL), w_qkvg=w_qkvg, w_a=w_a, w_dt=w_dt, qk_w=qk_w,
                convp=convp, hp=hp, dexp=dexp, ssm_nw=ssm_norm_w.reshape(1, D_SSM), expand=expand,
                w_out=w_out.astype(BF16))


def _rel_bias_table(rel_bias, q_rows, hist_rows, span_pad, band_chunk):
    n_heads = rel_bias.shape[0]
    period = span_pad + q_rows
    k = np.arange(period)
    rel = np.clip(hist_rows + q_rows - 1 - k, -REL_CLIP, REL_CLIP) + REL_CLIP
    onehot = jnp.asarray(np.eye(N_REL, dtype=np.float32)[:, rel])
    diag_row = jnp.dot(rel_bias.astype(F32), onehot, precision=lax.Precision.HIGHEST)
    flat = jnp.tile(diag_row, (1, q_rows))[:, :q_rows * (period - 1)]
    table = flat.reshape(n_heads, q_rows, period - 1)[:, :, q_rows - 1:q_rows - 1 + span_pad]
    i_idx = np.arange(q_rows)[:, None]
    j_idx = np.arange(span_pad)[None, :]
    valid = j_idx < hist_rows + q_rows
    if band_chunk is not None:
        start = (i_idx // band_chunk) * band_chunk
        valid = valid & (j_idx >= start) & (j_idx < start + hist_rows + band_chunk)
    return jnp.where(jnp.asarray(np.broadcast_to(valid, (q_rows, span_pad)))[None], table / ATT_SCALE, MASK_VALUE)


def _layer(x, hist, s0, cache_k, cache_v, prm, rel_bias, *, n_streams, seq, prompt):
    t = n_streams * seq
    x2 = x.reshape(t, D_MODEL)
    proj = _in_projection(x2, prm["norm_w"], prm["w_qkvg"], prm["w_a"], prm["w_dt"], prm["qk_w"],
                          tm=PROMPT_PROJ_TM if prompt else SAMPLE_PROJ_TM, emit_kv=not prompt)
    if prompt:
        qkv, rest, dt = proj
        q_rows, sub_rows, n_chunks = SSD_CHUNK_ROWS, SSD_ROWS, seq // SSD_CHUNK_ROWS
        kv_rows = BAND_ROWS
        kv_new = _kv_rows(x2, prm["norm_w"], prm["w_qkvg"], prm["qk_w"], first_row=t - kv_rows, n_rows=kv_rows,
                          tm=KV_ROWS_TM)
    else:
        qkv, rest, dt, kv_new = proj
        q_rows, sub_rows, n_chunks = seq, seq, 1
        kv_rows = t
    y_ssm, s_new = _ssd_branch(rest, dt, hist, s0, prm["convp"], prm["hp"], prm["dexp"], prm["ssm_nw"],
                               prm["expand"], n_streams=n_streams, q_rows=q_rows, sub_rows=sub_rows,
                               n_chunks=n_chunks)
    if prompt:
        bias = _rel_bias_table(rel_bias, ATT_Q_ROWS, BAND_ROWS, BAND_ROWS + ATT_Q_ROWS, CHUNK)
        y_att = _prompt_attention(qkv, rest, bias)
    else:
        bias = _rel_bias_table(rel_bias, seq, BAND_ROWS, BAND_ROWS + LANES, None)
        y_att = _sample_attention(qkv, rest, cache_k, cache_v, bias, n_streams=n_streams, q_rows=seq)
    y = _out_projection(y_ssm, y_att, prm["w_out"], x2, tm=OUT_PROJ_TM, tn=OUT_PROJ_TN)
    new_conv = rest.reshape(n_streams, seq, R_COLS)[:, seq - (CONV_WIDTH - 1):, R_X:]
    kv_streams = kv_rows // n_streams
    kh = kv_new[:, :D_ATT].reshape(n_streams, kv_streams, ATT_HEADS, ATT_HEAD_DIM)
    vh = kv_new[:, D_ATT:].reshape(n_streams, kv_streams, ATT_HEADS, ATT_HEAD_DIM)
    new_ssm = s_new.reshape(n_streams, SSM_HEADS, SSM_HEADDIM, D_STATE)
    return y.reshape(n_streams, seq, D_MODEL), new_conv, new_ssm, kh, vh


def kernel(x_prompt, x_sample, state_conv, state_ssm, cache_k, cache_v, norm_w, w_in, conv_w, conv_b, dt_bias, a_log, d_skip, ssm_norm_w, q_norm_w, k_norm_w, rel_bias, w_out):
    bp, lp, _ = x_prompt.shape
    bs, ls, _ = x_sample.shape
    assert bp == 1 and lp >= BAND_ROWS and BAND_ROWS % KV_ROWS_TM == 0
    assert all(lp % rows == 0 for rows in (PROMPT_PROJ_TM, OUT_PROJ_TM, SSD_CHUNK_ROWS, 2 * ATT_Q_ROWS))
    assert ls % (2 * SUBLANES) == 0 and ls <= LANES and cache_k.shape[2] == BAND_ROWS
    assert all((bs * ls) % rows == 0 for rows in (SAMPLE_PROJ_TM, OUT_PROJ_TM))
    assert norm_w.shape[0] == 1
    prm = _prepare_params(norm_w[0], w_in[0], conv_w[0], conv_b[0], dt_bias[0], a_log[0], d_skip[0],
                          ssm_norm_w[0], q_norm_w[0], k_norm_w[0], w_out[0])
    rb = rel_bias[0]

    zero_hist = jnp.zeros((bp, SUBLANES, CONV_DIM), F32)
    zero_state = jnp.zeros((bp, D_SSM, D_STATE), F32)
    yp, c1, s1, k1, v1 = _layer(x_prompt, zero_hist, zero_state, None, None, prm, rb,
                                n_streams=bp, seq=lp, prompt=True)

    hist = jnp.pad(state_conv[0], ((0, 0), (SUBLANES - (CONV_WIDTH - 1), 0), (0, 0)))
    s0 = state_ssm[0].reshape(bs, D_SSM, D_STATE)
    half_heads = (bs, BAND_ROWS, ATT_HEADS // SUBLANES, SUBLANES, ATT_HEAD_DIM)
    ck = cache_k[0].reshape(half_heads)
    cv = cache_v[0].reshape(half_heads)
    ys, c2, s2, k2, v2 = _layer(x_sample, hist, s0, ck, cv, prm, rb, n_streams=bs, seq=ls, prompt=False)

    return (yp, ys, c1[None], s1[None], k1[None], v1[None], c2[None], s2[None], k2[None], v2[None])
```

```python
import functools

import jax
import jax.numpy as jnp
import numpy as np
from jax import lax
from jax.experimental import pallas as pl
from jax.experimental.pallas import tpu as pltpu

F32 = jnp.float32
BF16 = jnp.bfloat16

D_MODEL = 2048
CHUNK = 64
LEFT_CHUNKS = 8
BAND_ROWS = LEFT_CHUNKS * CHUNK
D_SSM = 2048
D_ATT = 2048
SSM_HEADDIM = 64
SSM_HEADS = 32
SSM_GROUPS = 4
HEADS_PER_GROUP = SSM_HEADS // SSM_GROUPS
GROUP_COLS = HEADS_PER_GROUP * SSM_HEADDIM
D_STATE = 128
BC_COLS = 2 * SSM_GROUPS * D_STATE
CONV_WIDTH = 4
CONV_DIM = D_SSM + BC_COLS
ATT_HEAD_DIM = 128
ATT_HEADS = 16
REL_CLIP = 128
N_REL = 2 * REL_CLIP + 1
EPS = 1e-6
OFF_Z = 0
OFF_XBC = OFF_Z + D_SSM
OFF_DT = OFF_XBC + CONV_DIM
OFF_Q = OFF_DT + SSM_HEADS
OFF_K = OFF_Q + D_ATT
OFF_V = OFF_K + D_ATT
OFF_G = OFF_V + D_ATT

LANES = 128
SUBLANES = 8
VMEM_LIMIT_BYTES = 56 * 1024 * 1024

R_G = 0
R_Z = R_G + D_ATT
R_X = R_Z + D_SSM
R_BC = R_X + D_SSM
R_COLS = R_BC + BC_COLS
DT_COLS = LANES
QKV_K, QKV_V, QKV_Q = 0, 1, 2
N_SPLIT = 3

PROJ_TN = 1024
PROMPT_PROJ_TM = 1024
SAMPLE_PROJ_TM = 512
KV_ROWS_TM = 512
OUT_PROJ_TM = 1024
OUT_PROJ_TN = 1024
CAST_CHUNK = 256
SSD_ROWS = 128
SSD_CHUNK_ROWS = 512
ATT_Q_ROWS = 128
SOFTMAX_SLAB = 32
HEAD_PIPELINE_SKEW = 2
MASK_VALUE = -1e30
ATT_SCALE = ATT_HEAD_DIM ** -0.5
LOG2E = 1.4426950408889634


def _silu(v):
    return v * (1.0 / (1.0 + jnp.exp2(v * (-LOG2E))))


def _compiler_params(semantics):
    return pltpu.CompilerParams(dimension_semantics=semantics, vmem_limit_bytes=VMEM_LIMIT_BYTES)


def _normed_rows(x_ref, nw_ref):
    x = x_ref[...]
    ms = jnp.mean(x * x, axis=-1, keepdims=True)
    return (x * lax.rsqrt(ms + EPS) * nw_ref[...]).astype(BF16)


def _store_cache_rows(out_ref, head_rows, head):
    out_ref[pl.ds(head, head_rows.shape[0], stride=ATT_HEADS), :] = head_rows


def _inproj_kernel(*refs, n_qk, n_qkv, n_g, emit_kv):
    x_ref, nw_ref, wqkvg_ref, wa_ref, wdt_ref, qkw_ref, qkv_ref, rest_ref, dt_ref = refs[:9]
    k_out, v_out = refs[9:11] if emit_kv else (None, None)
    h_scr = refs[-1]
    j = pl.program_id(1)
    heads_per_tile = PROJ_TN // ATT_HEAD_DIM
    tiles_per_proj = n_qkv - n_qk
    first_head = (j % tiles_per_proj) * heads_per_tile

    @pl.when(j == 0)
    def _():
        h = _normed_rows(x_ref, nw_ref)
        h_scr[...] = h
        dt_ref[...] = jnp.dot(h, wdt_ref[...], preferred_element_type=F32)

    def qkv_tile(use_norm):
        acc = jnp.dot(h_scr[...], wqkvg_ref[...], preferred_element_type=F32)
        for hh in range(heads_per_tile):
            sl = slice(hh * ATT_HEAD_DIM, (hh + 1) * ATT_HEAD_DIM)
            head = acc[:, sl]
            if use_norm:
                r = lax.rsqrt(jnp.mean(head * head, axis=-1, keepdims=True) + EPS)
                head = head * r * qkw_ref[:, sl]
            qkv_ref[0, hh] = head.astype(BF16)
            if emit_kv:
                _store_cache_rows(k_out if use_norm else v_out, head, first_head + hh)

    pl.when(j < n_qk)(functools.partial(qkv_tile, True))
    pl.when(jnp.logical_and(j >= n_qk, j < n_qkv))(functools.partial(qkv_tile, False))

    @pl.when(jnp.logical_and(j >= n_qkv, j < n_qkv + n_g))
    def _():
        rest_ref[...] = jnp.dot(h_scr[...], wqkvg_ref[...], preferred_element_type=F32)

    @pl.when(j >= n_qkv + n_g)
    def _():
        rest_ref[...] = jnp.dot(h_scr[...], wa_ref[...], preferred_element_type=F32)


def _in_projection(x, norm_w, w_qkvg, w_a, w_dt, qk_w, *, tm, emit_kv):
    t = x.shape[0]
    tn = PROJ_TN
    tiles_per_proj = D_ATT // tn
    n_qk, n_qkv, n_g = 2 * tiles_per_proj, 3 * tiles_per_proj, tiles_per_proj
    n_a = (D_SSM + CONV_DIM) // tn
    heads_per_tile = tn // ATT_HEAD_DIM
    grid = (t // tm, n_qkv + n_g + n_a)
    kern = functools.partial(_inproj_kernel, n_qk=n_qk, n_qkv=n_qkv, n_g=n_g, emit_kv=emit_kv)

    def qkv_index(i, j):
        jj = jnp.minimum(j, n_qkv - 1)
        plane = (jj // tiles_per_proj + QKV_Q) % 3
        return (plane, jj % tiles_per_proj, i, 0)

    out_specs = [
        pl.BlockSpec((1, heads_per_tile, tm, ATT_HEAD_DIM), qkv_index),
        pl.BlockSpec((tm, tn), lambda i, j: (i, jnp.maximum(j - n_qkv, 0))),
        pl.BlockSpec((tm, DT_COLS), lambda i, j: (i, 0)),
    ]
    out_shape = [
        jax.ShapeDtypeStruct((3, ATT_HEADS, t, ATT_HEAD_DIM), BF16),
        jax.ShapeDtypeStruct((t, R_COLS), F32),
        jax.ShapeDtypeStruct((t, DT_COLS), F32),
    ]
    if emit_kv:
        for _ in range(2):
            out_specs.append(pl.BlockSpec((tm * ATT_HEADS, ATT_HEAD_DIM), lambda i, j: (i, 0)))
            out_shape.append(jax.ShapeDtypeStruct((t * ATT_HEADS, ATT_HEAD_DIM), F32))
    return pl.pallas_call(
        kern,
        grid=grid,
        in_specs=[
            pl.BlockSpec((tm, D_MODEL), lambda i, j: (i, 0)),
            pl.BlockSpec((1, D_MODEL), lambda i, j: (0, 0)),
            pl.BlockSpec((D_MODEL, tn), lambda i, j: (0, jnp.minimum(j, n_qkv + n_g - 1))),
            pl.BlockSpec((D_MODEL, tn), lambda i, j: (0, jnp.clip(j - n_qkv - n_g, 0, n_a - 1))),
            pl.BlockSpec((D_MODEL, DT_COLS), lambda i, j: (0, 0)),
            pl.BlockSpec((1, tn), lambda i, j: (0, jnp.minimum(j, n_qk - 1))),
        ],
        out_specs=out_specs,
        out_shape=out_shape,
        scratch_shapes=[pltpu.VMEM((tm, D_MODEL), BF16)],
        compiler_params=_compiler_params(("parallel", "arbitrary")),
        name="in_projection",
    )(x, norm_w, w_qkvg, w_a, w_dt, qk_w)


def _kv_rows_kernel(x_ref, nw_ref, w_ref, qkw_ref, k_out, v_out, h_scr, *, n_k):
    j = pl.program_id(1)
    heads_per_tile = PROJ_TN // ATT_HEAD_DIM
    first_head = (j % n_k) * heads_per_tile

    @pl.when(j == 0)
    def _():
        h_scr[...] = _normed_rows(x_ref, nw_ref)

    def tile(is_k):
        acc = jnp.dot(h_scr[...], w_ref[...], preferred_element_type=F32)
        for hh in range(heads_per_tile):
            sl = slice(hh * ATT_HEAD_DIM, (hh + 1) * ATT_HEAD_DIM)
            head = acc[:, sl]
            if is_k:
                r = lax.rsqrt(jnp.mean(head * head, axis=-1, keepdims=True) + EPS)
                head = head * r * qkw_ref[:, sl]
            _store_cache_rows(k_out if is_k else v_out, head, first_head + hh)

    pl.when(j < n_k)(functools.partial(tile, True))
    pl.when(j >= n_k)(functools.partial(tile, False))


def _kv_rows(x, norm_w, w_qkvg, qk_w, *, first_row, n_rows, tm):
    tn = PROJ_TN
    tiles_per_proj = D_ATT // tn
    row0 = first_row // tm
    kern = functools.partial(_kv_rows_kernel, n_k=tiles_per_proj)
    out_spec = pl.BlockSpec((tm * ATT_HEADS, ATT_HEAD_DIM), lambda i, j: (i, 0))
    out_shape = jax.ShapeDtypeStruct((n_rows * ATT_HEADS, ATT_HEAD_DIM), F32)
    return pl.pallas_call(
        kern,
        grid=(n_rows // tm, 2 * tiles_per_proj),
        in_specs=[
            pl.BlockSpec((tm, D_MODEL), lambda i, j: (row0 + i, 0)),
            pl.BlockSpec((1, D_MODEL), lambda i, j: (0, 0)),
            pl.BlockSpec((D_MODEL, tn), lambda i, j: (0, tiles_per_proj + j)),
            pl.BlockSpec((1, tn), lambda i, j: (0, jnp.minimum(tiles_per_proj + j, 2 * tiles_per_proj - 1))),
        ],
        out_specs=[out_spec, out_spec],
        out_shape=[out_shape, out_shape],
        scratch_shapes=[pltpu.VMEM((tm, D_MODEL), BF16)],
        compiler_params=_compiler_params(("parallel", "arbitrary")),
        name="kv_rows",
    )(x, norm_w, w_qkvg, qk_w)


def _transpose_rows_to_lanes(v):
    q = v.shape[0]
    if q < LANES:
        v = jnp.concatenate([v, jnp.zeros((LANES - q, LANES), v.dtype)], axis=0)
    return v.T[:, 0:q]


def _split_bf16(v):
    pieces = []
    rem = v
    for _ in range(N_SPLIT):
        piece = rem.astype(BF16)
        pieces.append(piece)
        rem = rem - piece.astype(F32)
    return jnp.concatenate(pieces, axis=1)


def _ssd_kernel(x_ref, bc_ref, z_ref, dt_ref, hist_ref, s0_ref, convp_ref, hp_ref, dexp_ref, nw_ref, expand_ref,
                y_ref, sout_ref, conv_scr, act_scr, st_scr, yd_scr, exp_scr, *, q_rows, sub_rows, n_chunks):
    c = pl.program_id(1)
    q = q_rows

    @pl.when(c == 0)
    def _():
        conv_scr[0:SUBLANES, :] = hist_ref[0]
        for g in range(SSM_GROUPS):
            st_scr[g] = s0_ref[0, g * GROUP_COLS:(g + 1) * GROUP_COLS, :].T

    conv_scr[SUBLANES:SUBLANES + q, 0:D_SSM] = x_ref[...]
    conv_scr[SUBLANES:SUBLANES + q, D_SSM:] = bc_ref[...]

    for c0 in range(0, CONV_DIM, GROUP_COLS):
        cols = slice(c0, c0 + GROUP_COLS)
        xp = conv_scr[:, cols]
        conv = convp_ref[CONV_WIDTH:CONV_WIDTH + 1, cols] + convp_ref[CONV_WIDTH - 1:CONV_WIDTH, cols] * xp[SUBLANES:]
        for shift in range(1, CONV_WIDTH):
            tap = CONV_WIDTH - 1 - shift
            conv = conv + convp_ref[tap:tap + 1, cols] * pltpu.roll(xp, shift, axis=0)[SUBLANES:]
        act_scr[:, cols] = _silu(conv)
    conv_scr[0:SUBLANES, :] = conv_scr[q:q + SUBLANES, :]

    s = sub_rows
    ii = lax.broadcasted_iota(jnp.int32, (s, s), 0)
    jj = lax.broadcasted_iota(jnp.int32, (s, s), 1)
    causal = ii >= jj
    for sc in range(q // s):
        rows = slice(sc * s, (sc + 1) * s)
        v = dt_ref[rows, :] + hp_ref[0:1, :]
        dt = jnp.maximum(v, 0.0) + jnp.log1p(jnp.exp(-jnp.abs(v)))
        a = dt * (-jnp.exp(hp_ref[1:2, :]))
        acum = jnp.dot(causal.astype(F32), a, precision=lax.Precision.HIGHEST, preferred_element_type=F32)
        acum2 = acum * LOG2E
        row_t = _transpose_rows_to_lanes(acum2 - jnp.log(dt) * LOG2E)
        last = acum[s - 1:s, :]
        factors = jnp.concatenate(
            [jnp.exp(acum),
             jnp.exp(last - acum) * dt,
             jnp.broadcast_to(jnp.exp(last), (SUBLANES, LANES))], axis=0)
        exp_scr[sc] = jnp.dot(_split_bf16(factors), expand_ref[...], preferred_element_type=F32)

        for g in range(SSM_GROUPS):
            cols = slice(g * GROUP_COLS, (g + 1) * GROUP_COLS)
            xact = act_scr[rows, cols]
            bact = act_scr[rows, D_SSM + g * D_STATE:D_SSM + (g + 1) * D_STATE].astype(BF16)
            cact = act_scr[rows, D_SSM + (SSM_GROUPS + g) * D_STATE:D_SSM + (SSM_GROUPS + g + 1) * D_STATE].astype(BF16)
            cb = lax.dot_general(cact, bact, (((1,), (1,)), ((), ())), preferred_element_type=F32)
            st = st_scr[g]
            y_off = jnp.dot(cact, st.astype(BF16), preferred_element_type=F32)
            for r in range(HEADS_PER_GROUP):
                h = g * HEADS_PER_GROUP + r
                m = cb * jnp.exp2(jnp.where(causal, acum2[:, h:h + 1] - row_t[h:h + 1, :], -jnp.inf))
                xh = xact[:, r * SSM_HEADDIM:(r + 1) * SSM_HEADDIM]
                yd_scr[sc, :, h * SSM_HEADDIM:(h + 1) * SSM_HEADDIM] = jnp.dot(
                    m.astype(BF16), xh.astype(BF16), preferred_element_type=F32)
            y = yd_scr[sc, :, cols] + y_off * exp_scr[sc, 0:s, cols] + dexp_ref[:, cols] * xact
            xw = (xact * exp_scr[sc, s:2 * s, cols]).astype(BF16)
            upd = lax.dot_general(bact, xw, (((0,), (0,)), ((), ())), preferred_element_type=F32)
            st_scr[g] = st * exp_scr[sc, 2 * s:2 * s + 1, cols] + upd

            yg = y * _silu(z_ref[rows, cols])
            rn = lax.rsqrt(jnp.mean(yg * yg, axis=-1, keepdims=True) + EPS)
            y_ref[rows, cols] = (yg * rn * nw_ref[:, cols]).astype(y_ref.dtype)

    @pl.when(c == n_chunks - 1)
    def _():
        for g in range(SSM_GROUPS):
            sout_ref[0, g * GROUP_COLS:(g + 1) * GROUP_COLS, :] = st_scr[g].T


def _ssd_branch(rest, dt, hist, s0, convp, hp, dexp, ssm_nw, expand, *, n_streams, q_rows, sub_rows, n_chunks):
    t = rest.shape[0]
    n_sub = q_rows // sub_rows
    assert n_sub * sub_rows == q_rows

    def rows(b, c):
        return b * n_chunks + c

    def const(shape):
        return pl.BlockSpec(shape, lambda b, c: (0,) * len(shape))

    kern = functools.partial(_ssd_kernel, q_rows=q_rows, sub_rows=sub_rows, n_chunks=n_chunks)
    return pl.pallas_call(
        kern,
        grid=(n_streams, n_chunks),
        in_specs=[
            pl.BlockSpec((q_rows, D_SSM), lambda b, c: (rows(b, c), R_X // D_SSM)),
            pl.BlockSpec((q_rows, BC_COLS), lambda b, c: (rows(b, c), R_BC // BC_COLS)),
            pl.BlockSpec((q_rows, D_SSM), lambda b, c: (rows(b, c), R_Z // D_SSM)),
            pl.BlockSpec((q_rows, DT_COLS), lambda b, c: (rows(b, c), 0)),
            pl.BlockSpec((1, SUBLANES, CONV_DIM), lambda b, c: (b, 0, 0)),
            pl.BlockSpec((1, D_SSM, D_STATE), lambda b, c: (b, 0, 0)),
            const((SUBLANES, CONV_DIM)),
            const((SUBLANES, LANES)),
            const((1, D_SSM)),
            const((1, D_SSM)),
            const((N_SPLIT * LANES, D_SSM)),
        ],
        out_specs=[
            pl.BlockSpec((q_rows, D_SSM), lambda b, c: (rows(b, c), 0)),
            pl.BlockSpec((1, D_SSM, D_STATE), lambda b, c: (b, 0, 0)),
        ],
        out_shape=[
            jax.ShapeDtypeStruct((t, D_SSM), BF16),
            jax.ShapeDtypeStruct((n_streams, D_SSM, D_STATE), F32),
        ],
        scratch_shapes=[
            pltpu.VMEM((SUBLANES + q_rows, CONV_DIM), F32),
            pltpu.VMEM((q_rows, CONV_DIM), F32),
            pltpu.VMEM((SSM_GROUPS, D_STATE, GROUP_COLS), F32),
            pltpu.VMEM((n_sub, sub_rows, D_SSM), F32),
            pltpu.VMEM((n_sub, 2 * sub_rows + SUBLANES, D_SSM), F32),
        ],
        compiler_params=_compiler_params(("parallel", "arbitrary")),
        name="ssd_branch",
    )(rest, rest, rest, dt, hist, s0, convp, hp, dexp, ssm_nw, expand)


def _softmax_stage(s_scr, p_scr, inv_scr, bias_ref, h, q_rows):
    exp2_scale = ATT_SCALE * LOG2E
    slab = min(SOFTMAX_SLAB, q_rows)
    bias_rows = bias_ref.shape[1]
    for r0 in range(0, q_rows, slab):
        b0 = r0 % bias_rows
        u = s_scr[h, r0:r0 + slab, :] + bias_ref[h, b0:b0 + slab, :]
        m = jnp.max(u, axis=-1, keepdims=True)
        e = jnp.exp2((u - m) * exp2_scale)
        p_scr[h, r0:r0 + slab, :] = e.astype(BF16)
        inv = 1.0 / jnp.sum(e, axis=-1, keepdims=True)
        inv_scr[h, r0:r0 + slab, :] = jnp.broadcast_to(inv, (slab, ATT_HEAD_DIM))


def _head_pipeline(qk_stage, softmax_stage, pv_stage):
    skew = HEAD_PIPELINE_SKEW
    for step in range(ATT_HEADS + 2 * skew):
        if step < ATT_HEADS:
            qk_stage(step)
        if skew <= step < ATT_HEADS + skew:
            softmax_stage(step - skew)
        if step >= 2 * skew:
            pv_stage(step - 2 * skew)


def _prompt_attn_kernel(*refs, n_hist_blocks):
    n_win = n_hist_blocks + 2
    tq = ATT_Q_ROWS
    q_ref, g_ref = refs[0], refs[1]
    kv_refs = refs[2:2 + n_win]
    bias_ref, o_ref, s_scr, p_scr, inv_scr = refs[2 + n_win:]
    first_block = 2 * pl.program_id(0) - n_hist_blocks

    def block_rows(blk):
        lo = 0 if blk <= n_hist_blocks else tq
        hi = 2 * tq if blk >= 1 else tq
        return lo, hi

    def qk_stage(h, mask_history):
        for blk in range(n_win):
            lo, hi = block_rows(blk)
            s = lax.dot_general(q_ref[0, h, lo:hi, :], kv_refs[blk][QKV_K, h], (((1,), (1,)), ((), ())),
                                preferred_element_type=F32)
            if mask_history and blk < n_hist_blocks:
                s = jnp.where(first_block + blk >= 0, s, MASK_VALUE)
            if lo == 0:
                s_scr[h, 0:tq, blk * tq:(blk + 1) * tq] = s[0:tq]
            if hi == 2 * tq:
                s_scr[h, tq:2 * tq, (blk - 1) * tq:blk * tq] = s[tq - lo:2 * tq - lo]

    def pv_stage(h):
        sl = slice(h * ATT_HEAD_DIM, (h + 1) * ATT_HEAD_DIM)
        o_first = jnp.zeros((tq, ATT_HEAD_DIM), F32)
        o_second = jnp.zeros((tq, ATT_HEAD_DIM), F32)
        for blk in range(n_win):
            lo, hi = block_rows(blk)
            parts = []
            if lo == 0:
                parts.append(p_scr[h, 0:tq, blk * tq:(blk + 1) * tq])
            if hi == 2 * tq:
                parts.append(p_scr[h, tq:2 * tq, (blk - 1) * tq:blk * tq])
            p = parts[0] if len(parts) == 1 else jnp.concatenate(parts, axis=0)
            o = jnp.dot(p, kv_refs[blk][QKV_V, h], preferred_element_type=F32)
            if lo == 0:
                o_first = o_first + o[0:tq]
            if hi == 2 * tq:
                o_second = o_second + o[tq - lo:2 * tq - lo]
        o = jnp.concatenate([o_first, o_second], axis=0)
        o_ref[:, sl] = (o * inv_scr[h] * _silu(g_ref[:, sl])).astype(o_ref.dtype)

    softmax_stage = functools.partial(_softmax_stage, s_scr, p_scr, inv_scr, bias_ref, q_rows=2 * tq)

    @pl.when(first_block < 0)
    def _():
        _head_pipeline(functools.partial(qk_stage, mask_history=True), softmax_stage, pv_stage)

    @pl.when(first_block >= 0)
    def _():
        _head_pipeline(functools.partial(qk_stage, mask_history=False), softmax_stage, pv_stage)


def _prompt_attention(qkv, rest, bias):
    t = rest.shape[0]
    tq = ATT_Q_ROWS
    n_hist_blocks = BAND_ROWS // tq
    n_win = n_hist_blocks + 2
    span = BAND_ROWS + tq
    assert (QKV_K, QKV_V) == (0, 1)
    kv_block = (2, ATT_HEADS, tq, ATT_HEAD_DIM)

    def kv_spec(blk):
        return pl.BlockSpec(kv_block, lambda i: (0, 0, jnp.maximum(2 * i - n_hist_blocks + blk, 0), 0))

    in_specs = [pl.BlockSpec((1, ATT_HEADS, 2 * tq, ATT_HEAD_DIM), lambda i: (QKV_Q, 0, i, 0)),
                pl.BlockSpec((2 * tq, D_ATT), lambda i: (i, R_G // D_ATT))]
    in_specs += [kv_spec(blk) for blk in range(n_win)]
    in_specs += [pl.BlockSpec((ATT_HEADS, tq, span), lambda i: (0, 0, 0))]
    kern = functools.partial(_prompt_attn_kernel, n_hist_blocks=n_hist_blocks)
    return pl.pallas_call(
        kern,
        grid=(t // (2 * tq),),
        in_specs=in_specs,
        out_specs=pl.BlockSpec((2 * tq, D_ATT), lambda i: (i, 0)),
        out_shape=jax.ShapeDtypeStruct((t, D_ATT), BF16),
        scratch_shapes=[pltpu.VMEM((ATT_HEADS, 2 * tq, span), F32),
                        pltpu.VMEM((ATT_HEADS, 2 * tq, span), BF16),
                        pltpu.VMEM((ATT_HEADS, 2 * tq, ATT_HEAD_DIM), F32)],
        compiler_params=_compiler_params(("parallel",)),
        name="prompt_attention",
    )(qkv, rest, *([qkv] * n_win), bias)


def _sample_attn_kernel(q_ref, kn_ref, vn_ref, g_ref, kc_lo, kc_hi, vc_lo, vc_hi, bias_ref, o_ref,
                        s_scr, p_scr, inv_scr, *, q_rows):
    new_pad = LANES - q_rows
    zpad = jnp.zeros((new_pad, ATT_HEAD_DIM), BF16)

    def cached(lo_ref, hi_ref, h):
        ref = (lo_ref if h < SUBLANES else hi_ref).reshape(BAND_ROWS * SUBLANES, ATT_HEAD_DIM)
        return ref[pl.ds(h % SUBLANES, BAND_ROWS, stride=SUBLANES), :].astype(BF16)

    def qk_stage(h):
        qh = q_ref[0, h]
        s_scr[h, :, 0:BAND_ROWS] = lax.dot_general(qh, cached(kc_lo, kc_hi, h), (((1,), (1,)), ((), ())),
                                                   preferred_element_type=F32)
        k_new = jnp.concatenate([kn_ref[0, h], zpad], axis=0)
        s_scr[h, :, BAND_ROWS:] = lax.dot_general(qh, k_new, (((1,), (1,)), ((), ())), preferred_element_type=F32)

    def pv_stage(h):
        sl = slice(h * ATT_HEAD_DIM, (h + 1) * ATT_HEAD_DIM)
        v_new = jnp.concatenate([vn_ref[0, h], zpad], axis=0)
        o = jnp.dot(p_scr[h, :, 0:BAND_ROWS], cached(vc_lo, vc_hi, h), preferred_element_type=F32)
        o = o + jnp.dot(p_scr[h, :, BAND_ROWS:], v_new, preferred_element_type=F32)
        o_ref[:, sl] = (o * inv_scr[h] * _silu(g_ref[:, sl])).astype(o_ref.dtype)

    _head_pipeline(qk_stage, functools.partial(_softmax_stage, s_scr, p_scr, inv_scr, bias_ref, q_rows=q_rows),
                   pv_stage)


def _sample_attention(qkv, rest, cache_k, cache_v, bias, *, n_streams, q_rows):
    t = rest.shape[0]
    span_pad = BAND_ROWS + LANES
    kern = functools.partial(_sample_attn_kernel, q_rows=q_rows)

    def new_spec(which):
        return pl.BlockSpec((1, ATT_HEADS, q_rows, ATT_HEAD_DIM), lambda b: (which, 0, b, 0))

    def cache_spec(half):
        return pl.BlockSpec((None, BAND_ROWS, None, SUBLANES, ATT_HEAD_DIM), lambda b: (b, 0, half, 0, 0))

    return pl.pallas_call(
        kern,
        grid=(n_streams,),
        in_specs=[
            new_spec(QKV_Q), new_spec(QKV_K), new_spec(QKV_V),
            pl.BlockSpec((q_rows, D_ATT), lambda b: (b, R_G // D_ATT)),
            cache_spec(0), cache_spec(1), cache_spec(0), cache_spec(1),
            pl.BlockSpec((ATT_HEADS, q_rows, span_pad), lambda b: (0, 0, 0)),
        ],
        out_specs=pl.BlockSpec((q_rows, D_ATT), lambda b: (b, 0)),
        out_shape=jax.ShapeDtypeStruct((t, D_ATT), BF16),
        scratch_shapes=[pltpu.VMEM((ATT_HEADS, q_rows, span_pad), F32),
                        pltpu.VMEM((ATT_HEADS, q_rows, span_pad), BF16),
                        pltpu.VMEM((ATT_HEADS, q_rows, ATT_HEAD_DIM), F32)],
        compiler_params=_compiler_params(("parallel",)),
        name="sample_attention",
    )(qkv, qkv, qkv, rest, cache_k, cache_k, cache_v, cache_v, bias)


def _outproj_kernel(ys_ref, ya_ref, w1_ref, w2_ref, x_ref, o_ref):
    acc = jnp.dot(ys_ref[...], w1_ref[...], preferred_element_type=F32)
    acc = acc + jnp.dot(ya_ref[...], w2_ref[...], preferred_element_type=F32)
    o_ref[...] = x_ref[...] + acc


def _out_projection(y_ssm, y_att, w_out, x, *, tm, tn):
    t = x.shape[0]
    return pl.pallas_call(
        _outproj_kernel,
        grid=(t // tm, D_MODEL // tn),
        in_specs=[
            pl.BlockSpec((tm, D_SSM), lambda i, j: (i, 0)),
            pl.BlockSpec((tm, D_ATT), lambda i, j: (i, 0)),
            pl.BlockSpec((D_SSM, tn), lambda i, j: (0, j)),
            pl.BlockSpec((D_ATT, tn), lambda i, j: (D_SSM // D_ATT, j)),
            pl.BlockSpec((tm, tn), lambda i, j: (i, j)),
        ],
        out_specs=pl.BlockSpec((tm, tn), lambda i, j: (i, j)),
        out_shape=jax.ShapeDtypeStruct((t, D_MODEL), F32),
        compiler_params=_compiler_params(("parallel", "arbitrary")),
        name="out_projection",
    )(y_ssm, y_att, w_out, w_out, x)


def _pad_to(v, size, axis):
    pad = [(0, 0)] * v.ndim
    pad[axis] = (0, size - v.shape[axis])
    return jnp.pad(v, pad)


def _cast_w_in_kernel(main_ref, next_ref, wa_ref, wq_ref, wdt_ref, *, n_a, row_shift):
    j = pl.program_id(0)
    tn = main_ref.shape[0]
    chunk = CAST_CHUNK

    @pl.when(j == n_a)
    def _():
        row_id = lax.broadcasted_iota(jnp.int32, (DT_COLS, main_ref.shape[1]), 0)
        wdt_ref[...] = jnp.where(row_id < row_shift, main_ref[0:DT_COLS, :], 0.0).T.astype(BF16)

    def put(rows_of, out_ref):
        for r0 in range(0, tn, chunk):
            out_ref[:, r0:r0 + chunk] = rows_of(r0).T.astype(BF16)

    @pl.when(j < n_a)
    def _():
        put(lambda r0: main_ref[r0:r0 + chunk, :], wa_ref)

    @pl.when(j >= n_a)
    def _():
        def shifted(r0):
            lo = r0 + row_shift
            if lo + chunk <= tn:
                return main_ref[lo:lo + chunk, :]
            return jnp.concatenate([main_ref[lo:tn, :], next_ref[0:lo + chunk - tn, :]], axis=0)

        put(shifted, wq_ref)


def _cast_w_in(w_in_t):
    tn = PROJ_TN
    n_a = OFF_DT // tn
    n_q = (4 * D_ATT) // tn
    row_shift = OFF_Q - OFF_DT
    assert OFF_DT % tn == 0 and row_shift % SUBLANES == 0 and 0 < row_shift < LANES
    kern = functools.partial(_cast_w_in_kernel, n_a=n_a, row_shift=row_shift)
    return pl.pallas_call(
        kern,
        grid=(n_a + n_q,),
        in_specs=[
            pl.BlockSpec((tn, D_MODEL), lambda j: (j, 0)),
            pl.BlockSpec((LANES, D_MODEL), lambda j: ((j + 1) * (tn // LANES), 0)),
        ],
        out_specs=[
            pl.BlockSpec((D_MODEL, tn), lambda j: (0, jnp.minimum(j, n_a - 1))),
            pl.BlockSpec((D_MODEL, tn), lambda j: (0, jnp.maximum(j - n_a, 0))),
            pl.BlockSpec((D_MODEL, DT_COLS), lambda j: (0, 0)),
        ],
        out_shape=[
            jax.ShapeDtypeStruct((D_MODEL, n_a * tn), BF16),
            jax.ShapeDtypeStruct((D_MODEL, n_q * tn), BF16),
            jax.ShapeDtypeStruct((D_MODEL, DT_COLS), BF16),
        ],
        compiler_params=_compiler_params(("arbitrary",)),
        name="cast_w_in",
    )(w_in_t, w_in_t)


def _prepare_params(norm_w, w_in, conv_w, conv_b, dt_bias, a_log, d_skip, ssm_norm_w, q_norm_w, k_norm_w, w_out):
    w_a, w_qkvg, w_dt = _cast_w_in(w_in.T)
    qk_w = jnp.concatenate([jnp.tile(q_norm_w, ATT_HEADS), jnp.tile(k_norm_w, ATT_HEADS)]).reshape(1, 2 * D_ATT)
    convp = _pad_to(jnp.concatenate([conv_w, conv_b[None]], axis=0), SUBLANES, 0)
    hp = _pad_to(_pad_to(jnp.stack([dt_bias, a_log]), LANES, 1), SUBLANES, 0)
    dexp = jnp.repeat(d_skip, SSM_HEADDIM).reshape(1, D_SSM)
    head_of_col = np.arange(D_SSM) // SSM_HEADDIM
    expand = (np.arange(LANES)[:, None] == head_of_col[None, :]).astype(np.float32)
    expand = jnp.asarray(np.tile(expand, (N_SPLIT, 1)), dtype=BF16)
    return dict(norm_w=norm_w.reshape(1, D_MODEL), w_qkvg=w_qkvg, w_a=w_a, w_dt=w_dt, qk_w=qk_w,
                convp=convp, hp=hp, dexp=dexp, ssm_nw=ssm_norm_w.reshape(1, D_SSM), expand=expand,
                w_out=w_out.astype(BF16))


def _rel_bias_table(rel_bias, q_rows, hist_rows, span_pad, band_chunk):
    n_heads = rel_bias.shape[0]
    period = span_pad + q_rows
    k = np.arange(period)
    rel = np.clip(hist_rows + q_rows - 1 - k, -REL_CLIP, REL_CLIP) + REL_CLIP
    onehot = jnp.asarray(np.eye(N_REL, dtype=np.float32)[:, rel])
    diag_row = jnp.dot(rel_bias.astype(F32), onehot, precision=lax.Precision.HIGHEST)
    flat = jnp.tile(diag_row, (1, q_rows))[:, :q_rows * (period - 1)]
    table = flat.reshape(n_heads, q_rows, period - 1)[:, :, q_rows - 1:q_rows - 1 + span_pad]
    i_idx = np.arange(q_rows)[:, None]
    j_idx = np.arange(span_pad)[None, :]
    valid = j_idx < hist_rows + q_rows
    if band_chunk is not None:
        start = (i_idx // band_chunk) * band_chunk
        valid = valid & (j_idx >= start) & (j_idx < start + hist_rows + band_chunk)
    return jnp.where(jnp.asarray(np.broadcast_to(valid, (q_rows, span_pad)))[None], table / ATT_SCALE, MASK_VALUE)


def _layer(x, hist, s0, cache_k, cache_v, prm, rel_bias, *, n_streams, seq, prompt):
    t = n_streams * seq
    x2 = x.reshape(t, D_MODEL)
    proj = _in_projection(x2, prm["norm_w"], prm["w_qkvg"], prm["w_a"], prm["w_dt"], prm["qk_w"],
                          tm=PROMPT_PROJ_TM if prompt else SAMPLE_PROJ_TM, emit_kv=not prompt)
    if prompt:
        qkv, rest, dt = proj
        q_rows, sub_rows, n_chunks = SSD_CHUNK_ROWS, SSD_ROWS, seq // SSD_CHUNK_ROWS
        kv_rows = BAND_ROWS
        k_new, v_new = _kv_rows(x2, prm["norm_w"], prm["w_qkvg"], prm["qk_w"], first_row=t - kv_rows,
                                n_rows=kv_rows, tm=KV_ROWS_TM)
    else:
        qkv, rest, dt, k_new, v_new = proj
        q_rows, sub_rows, n_chunks = seq, seq, 1
        kv_rows = t
    y_ssm, s_new = _ssd_branch(rest, dt, hist, s0, prm["convp"], prm["hp"], prm["dexp"], prm["ssm_nw"],
                               prm["expand"], n_streams=n_streams, q_rows=q_rows, sub_rows=sub_rows,
                               n_chunks=n_chunks)
    if prompt:
        bias = _rel_bias_table(rel_bias, ATT_Q_ROWS, BAND_ROWS, BAND_ROWS + ATT_Q_ROWS, CHUNK)
        y_att = _prompt_attention(qkv, rest, bias)
    else:
        bias = _rel_bias_table(rel_bias, seq, BAND_ROWS, BAND_ROWS + LANES, None)
        y_att = _sample_attention(qkv, rest, cache_k, cache_v, bias, n_streams=n_streams, q_rows=seq)
    y = _out_projection(y_ssm, y_att, prm["w_out"], x2, tm=OUT_PROJ_TM, tn=OUT_PROJ_TN)
    new_conv = rest.reshape(n_streams, seq, R_COLS)[:, seq - (CONV_WIDTH - 1):, R_X:]
    kv_streams = kv_rows // n_streams
    kh = k_new.reshape(n_streams, kv_streams, ATT_HEADS, ATT_HEAD_DIM)
    vh = v_new.reshape(n_streams, kv_streams, ATT_HEADS, ATT_HEAD_DIM)
    new_ssm = s_new.reshape(n_streams, SSM_HEADS, SSM_HEADDIM, D_STATE)
    return y.reshape(n_streams, seq, D_MODEL), new_conv, new_ssm, kh, vh


def kernel(x_prompt, x_sample, state_conv, state_ssm, cache_k, cache_v, norm_w, w_in, conv_w, conv_b, dt_bias, a_log, d_skip, ssm_norm_w, q_norm_w, k_norm_w, rel_bias, w_out):
    bp, lp, _ = x_prompt.shape
    bs, ls, _ = x_sample.shape
    assert bp == 1 and lp >= BAND_ROWS and BAND_ROWS % KV_ROWS_TM == 0
    assert all(lp % rows == 0 for rows in (PROMPT_PROJ_TM, OUT_PROJ_TM, SSD_CHUNK_ROWS, 2 * ATT_Q_ROWS))
    assert ls % (2 * SUBLANES) == 0 and ls <= LANES and cache_k.shape[2] == BAND_ROWS
    assert all((bs * ls) % rows == 0 for rows in (SAMPLE_PROJ_TM, OUT_PROJ_TM))
    assert norm_w.shape[0] == 1
    prm = _prepare_params(norm_w[0], w_in[0], conv_w[0], conv_b[0], dt_bias[0], a_log[0], d_skip[0],
                          ssm_norm_w[0], q_norm_w[0], k_norm_w[0], w_out[0])
    rb = rel_bias[0]

    zero_hist = jnp.zeros((bp, SUBLANES, CONV_DIM), F32)
    zero_state = jnp.zeros((bp, D_SSM, D_STATE), F32)
    yp, c1, s1, k1, v1 = _layer(x_prompt, zero_hist, zero_state, None, None, prm, rb,
                                n_streams=bp, seq=lp, prompt=True)

    hist = jnp.pad(state_conv[0], ((0, 0), (SUBLANES - (CONV_WIDTH - 1), 0), (0, 0)))
    s0 = state_ssm[0].reshape(bs, D_SSM, D_STATE)
    half_heads = (bs, BAND_ROWS, ATT_HEADS // SUBLANES, SUBLANES, ATT_HEAD_DIM)
    ck = cache_k[0].reshape(half_heads)
    cv = cache_v[0].reshape(half_heads)
    ys, c2, s2, k2, v2 = _layer(x_sample, hist, s0, ck, cv, prm, rb, n_streams=bs, seq=ls, prompt=False)

    return (yp, ys, c1[None], s1[None], k1[None], v1[None], c2[None], s2[None], k2[None], v2[None])
```

```python
import functools

import jax
import jax.numpy as jnp
import numpy as np
from jax import lax
from jax.experimental import pallas as pl
from jax.experimental.pallas import tpu as pltpu

F32 = jnp.float32
BF16 = jnp.bfloat16

D_MODEL = 2048
CHUNK = 64
LEFT_CHUNKS = 8
BAND_ROWS = LEFT_CHUNKS * CHUNK
D_SSM = 2048
D_ATT = 2048
SSM_HEADDIM = 64
SSM_HEADS = 32
SSM_GROUPS = 4
HEADS_PER_GROUP = SSM_HEADS // SSM_GROUPS
GROUP_COLS = HEADS_PER_GROUP * SSM_HEADDIM
D_STATE = 128
BC_COLS = 2 * SSM_GROUPS * D_STATE
CONV_WIDTH = 4
CONV_DIM = D_SSM + BC_COLS
ATT_HEAD_DIM = 128
ATT_HEADS = 16
REL_CLIP = 128
N_REL = 2 * REL_CLIP + 1
EPS = 1e-6
OFF_Z = 0
OFF_XBC = OFF_Z + D_SSM
OFF_DT = OFF_XBC + CONV_DIM
OFF_Q = OFF_DT + SSM_HEADS
OFF_K = OFF_Q + D_ATT
OFF_V = OFF_K + D_ATT
OFF_G = OFF_V + D_ATT

LANES = 128
SUBLANES = 8
VMEM_LIMIT_BYTES = 56 * 1024 * 1024

R_G = 0
R_Z = R_G + D_ATT
R_X = R_Z + D_SSM
R_BC = R_X + D_SSM
R_COLS = R_BC + BC_COLS
DT_COLS = LANES
QKV_K, QKV_V, QKV_Q = 0, 1, 2
N_SPLIT = 3

PROJ_TN = 1024
PROMPT_PROJ_TM = 1024
SAMPLE_PROJ_TM = 512
KV_ROWS_TM = 512
OUT_PROJ_TM = 1024
OUT_PROJ_TN = 1024
CAST_CHUNK = 256
SSD_ROWS = 128
SSD_CHUNK_ROWS = 512
ATT_Q_ROWS = 128
SOFTMAX_SLAB = 32
HEAD_PIPELINE_SKEW = 2
MASK_VALUE = -1e30
ATT_SCALE = ATT_HEAD_DIM ** -0.5
LOG2E = 1.4426950408889634


def _silu(v):
    return v * (1.0 / (1.0 + jnp.exp2(v * (-LOG2E))))


def _compiler_params(semantics):
    return pltpu.CompilerParams(dimension_semantics=semantics, vmem_limit_bytes=VMEM_LIMIT_BYTES)


def _normed_rows(x_ref, nw_ref):
    x = x_ref[...]
    ms = jnp.mean(x * x, axis=-1, keepdims=True)
    return (x * lax.rsqrt(ms + EPS) * nw_ref[...]).astype(BF16)


def _store_cache_rows(out_ref, head_rows, head):
    out_ref[pl.ds(head, head_rows.shape[0], stride=ATT_HEADS), :] = head_rows


def _inproj_kernel(*refs, n_qk, n_qkv, n_g, emit_kv):
    x_ref, nw_ref, wqkvg_ref, wa_ref, wdt_ref, qkw_ref, qkv_ref, rest_ref, dt_ref = refs[:9]
    k_out, v_out = refs[9:11] if emit_kv else (None, None)
    h_scr = refs[-1]
    j = pl.program_id(1)
    heads_per_tile = PROJ_TN // ATT_HEAD_DIM
    tiles_per_proj = n_qkv - n_qk
    first_head = (j % tiles_per_proj) * heads_per_tile

    @pl.when(j == 0)
    def _():
        h = _normed_rows(x_ref, nw_ref)
        h_scr[...] = h
        dt_ref[...] = jnp.dot(h, wdt_ref[...], preferred_element_type=F32)

    def qkv_tile(use_norm):
        acc = jnp.dot(h_scr[...], wqkvg_ref[...], preferred_element_type=F32)
        for hh in range(heads_per_tile):
            sl = slice(hh * ATT_HEAD_DIM, (hh + 1) * ATT_HEAD_DIM)
            head = acc[:, sl]
            if use_norm:
                r = lax.rsqrt(jnp.mean(head * head, axis=-1, keepdims=True) + EPS)
                head = head * r * qkw_ref[:, sl]
            qkv_ref[0, hh] = head.astype(BF16)
            if emit_kv:
                _store_cache_rows(k_out if use_norm else v_out, head, first_head + hh)

    pl.when(j < n_qk)(functools.partial(qkv_tile, True))
    pl.when(jnp.logical_and(j >= n_qk, j < n_qkv))(functools.partial(qkv_tile, False))

    @pl.when(jnp.logical_and(j >= n_qkv, j < n_qkv + n_g))
    def _():
        rest_ref[...] = jnp.dot(h_scr[...], wqkvg_ref[...], preferred_element_type=F32)

    @pl.when(j >= n_qkv + n_g)
    def _():
        rest_ref[...] = jnp.dot(h_scr[...], wa_ref[...], preferred_element_type=F32)


def _in_projection(x, norm_w, w_qkvg, w_a, w_dt, qk_w, *, tm, emit_kv):
    t = x.shape[0]
    tn = PROJ_TN
    tiles_per_proj = D_ATT // tn
    n_qk, n_qkv, n_g = 2 * tiles_per_proj, 3 * tiles_per_proj, tiles_per_proj
    n_a = (D_SSM + CONV_DIM) // tn
    heads_per_tile = tn // ATT_HEAD_DIM
    grid = (t // tm, n_qkv + n_g + n_a)
    kern = functools.partial(_inproj_kernel, n_qk=n_qk, n_qkv=n_qkv, n_g=n_g, emit_kv=emit_kv)

    def qkv_index(i, j):
        jj = jnp.minimum(j, n_qkv - 1)
        plane = (jj // tiles_per_proj + QKV_Q) % 3
        return (plane, jj % tiles_per_proj, i, 0)

    out_specs = [
        pl.BlockSpec((1, heads_per_tile, tm, ATT_HEAD_DIM), qkv_index),
        pl.BlockSpec((tm, tn), lambda i, j: (i, jnp.maximum(j - n_qkv, 0))),
        pl.BlockSpec((tm, DT_COLS), lambda i, j: (i, 0)),
    ]
    out_shape = [
        jax.ShapeDtypeStruct((3, ATT_HEADS, t, ATT_HEAD_DIM), BF16),
        jax.ShapeDtypeStruct((t, R_COLS), F32),
        jax.ShapeDtypeStruct((t, DT_COLS), F32),
    ]
    if emit_kv:
        for _ in range(2):
            out_specs.append(pl.BlockSpec((tm * ATT_HEADS, ATT_HEAD_DIM), lambda i, j: (i, 0)))
            out_shape.append(jax.ShapeDtypeStruct((t * ATT_HEADS, ATT_HEAD_DIM), F32))
    return pl.pallas_call(
        kern,
        grid=grid,
        in_specs=[
            pl.BlockSpec((tm, D_MODEL), lambda i, j: (i, 0)),
            pl.BlockSpec((1, D_MODEL), lambda i, j: (0, 0)),
            pl.BlockSpec((D_MODEL, tn), lambda i, j: (0, jnp.minimum(j, n_qkv + n_g - 1))),
            pl.BlockSpec((D_MODEL, tn), lambda i, j: (0, jnp.clip(j - n_qkv - n_g, 0, n_a - 1))),
            pl.BlockSpec((D_MODEL, DT_COLS), lambda i, j: (0, 0)),
            pl.BlockSpec((1, tn), lambda i, j: (0, jnp.minimum(j, n_qk - 1))),
        ],
        out_specs=out_specs,
        out_shape=out_shape,
        scratch_shapes=[pltpu.VMEM((tm, D_MODEL), BF16)],
        compiler_params=_compiler_params(("parallel", "arbitrary")),
        name="in_projection",
    )(x, norm_w, w_qkvg, w_a, w_dt, qk_w)


def _kv_rows_kernel(x_ref, nw_ref, w_ref, qkw_ref, k_out, v_out, h_scr, *, n_k):
    j = pl.program_id(1)
    heads_per_tile = PROJ_TN // ATT_HEAD_DIM
    first_head = (j % n_k) * heads_per_tile

    @pl.when(j == 0)
    def _():
        h_scr[...] = _normed_rows(x_ref, nw_ref)

    def tile(is_k):
        acc = jnp.dot(h_scr[...], w_ref[...], preferred_element_type=F32)
        for hh in range(heads_per_tile):
            sl = slice(hh * ATT_HEAD_DIM, (hh + 1) * ATT_HEAD_DIM)
            head = acc[:, sl]
            if is_k:
                r = lax.rsqrt(jnp.mean(head * head, axis=-1, keepdims=True) + EPS)
                head = head * r * qkw_ref[:, sl]
            _store_cache_rows(k_out if is_k else v_out, head, first_head + hh)

    pl.when(j < n_k)(functools.partial(tile, True))
    pl.when(j >= n_k)(functools.partial(tile, False))


def _kv_rows(x, norm_w, w_qkvg, qk_w, *, first_row, n_rows, tm):
    tn = PROJ_TN
    tiles_per_proj = D_ATT // tn
    row0 = first_row // tm
    kern = functools.partial(_kv_rows_kernel, n_k=tiles_per_proj)
    out_spec = pl.BlockSpec((tm * ATT_HEADS, ATT_HEAD_DIM), lambda i, j: (i, 0))
    out_shape = jax.ShapeDtypeStruct((n_rows * ATT_HEADS, ATT_HEAD_DIM), F32)
    return pl.pallas_call(
        kern,
        grid=(n_rows // tm, 2 * tiles_per_proj),
        in_specs=[
            pl.BlockSpec((tm, D_MODEL), lambda i, j: (row0 + i, 0)),
            pl.BlockSpec((1, D_MODEL), lambda i, j: (0, 0)),
            pl.BlockSpec((D_MODEL, tn), lambda i, j: (0, tiles_per_proj + j)),
            pl.BlockSpec((1, tn), lambda i, j: (0, jnp.minimum(tiles_per_proj + j, 2 * tiles_per_proj - 1))),
        ],
        out_specs=[out_spec, out_spec],
        out_shape=[out_shape, out_shape],
        scratch_shapes=[pltpu.VMEM((tm, D_MODEL), BF16)],
        compiler_params=_compiler_params(("parallel", "arbitrary")),
        name="kv_rows",
    )(x, norm_w, w_qkvg, qk_w)


def _transpose_rows_to_lanes(v):
    q = v.shape[0]
    if q < LANES:
        v = jnp.concatenate([v, jnp.zeros((LANES - q, LANES), v.dtype)], axis=0)
    return v.T[:, 0:q]


def _split_bf16(v):
    pieces = []
    rem = v
    for _ in range(N_SPLIT):
        piece = rem.astype(BF16)
        pieces.append(piece)
        rem = rem - piece.astype(F32)
    return jnp.concatenate(pieces, axis=1)


def _ssd_kernel(x_ref, bc_ref, z_ref, dt_ref, hist_ref, s0_ref, convp_ref, hp_ref, dexp_ref, nw_ref, expand_ref,
                y_ref, sout_ref, conv_scr, act_scr, st_scr, yd_scr, exp_scr, *, q_rows, sub_rows, n_chunks):
    c = pl.program_id(1)
    q = q_rows

    @pl.when(c == 0)
    def _():
        conv_scr[0:SUBLANES, :] = hist_ref[0]
        for g in range(SSM_GROUPS):
            st_scr[g] = s0_ref[0, g * GROUP_COLS:(g + 1) * GROUP_COLS, :].T

    conv_scr[SUBLANES:SUBLANES + q, 0:D_SSM] = x_ref[...]
    conv_scr[SUBLANES:SUBLANES + q, D_SSM:] = bc_ref[...]

    for c0 in range(0, CONV_DIM, GROUP_COLS):
        cols = slice(c0, c0 + GROUP_COLS)
        xp = conv_scr[:, cols]
        conv = convp_ref[CONV_WIDTH:CONV_WIDTH + 1, cols] + convp_ref[CONV_WIDTH - 1:CONV_WIDTH, cols] * xp[SUBLANES:]
        for shift in range(1, CONV_WIDTH):
            tap = CONV_WIDTH - 1 - shift
            conv = conv + convp_ref[tap:tap + 1, cols] * pltpu.roll(xp, shift, axis=0)[SUBLANES:]
        act_scr[:, cols] = _silu(conv)
    conv_scr[0:SUBLANES, :] = conv_scr[q:q + SUBLANES, :]

    s = sub_rows
    ii = lax.broadcasted_iota(jnp.int32, (s, s), 0)
    jj = lax.broadcasted_iota(jnp.int32, (s, s), 1)
    causal = ii >= jj
    for sc in range(q // s):
        rows = slice(sc * s, (sc + 1) * s)
        v = dt_ref[rows, :] + hp_ref[0:1, :]
        dt = jnp.maximum(v, 0.0) + jnp.log1p(jnp.exp(-jnp.abs(v)))
        a = dt * (-jnp.exp(hp_ref[1:2, :]))
        acum = jnp.dot(causal.astype(F32), a, precision=lax.Precision.HIGHEST, preferred_element_type=F32)
        acum2 = acum * LOG2E
        row_t = _transpose_rows_to_lanes(acum2 - jnp.log(dt) * LOG2E)
        last = acum[s - 1:s, :]
        factors = jnp.concatenate(
            [jnp.exp(acum),
             jnp.exp(last - acum) * dt,
             jnp.broadcast_to(jnp.exp(last), (SUBLANES, LANES))], axis=0)
        exp_scr[sc] = jnp.dot(_split_bf16(factors), expand_ref[...], preferred_element_type=F32)

        for g in range(SSM_GROUPS):
            cols = slice(g * GROUP_COLS, (g + 1) * GROUP_COLS)
            xact = act_scr[rows, cols]
            bact = act_scr[rows, D_SSM + g * D_STATE:D_SSM + (g + 1) * D_STATE].astype(BF16)
            cact = act_scr[rows, D_SSM + (SSM_GROUPS + g) * D_STATE:D_SSM + (SSM_GROUPS + g + 1) * D_STATE].astype(BF16)
            cb = lax.dot_general(cact, bact, (((1,), (1,)), ((), ())), preferred_element_type=F32)
            st = st_scr[g]
            y_off = jnp.dot(cact, st.astype(BF16), preferred_element_type=F32)
            for r in range(HEADS_PER_GROUP):
                h = g * HEADS_PER_GROUP + r
                m = cb * jnp.exp2(jnp.where(causal, acum2[:, h:h + 1] - row_t[h:h + 1, :], -jnp.inf))
                xh = xact[:, r * SSM_HEADDIM:(r + 1) * SSM_HEADDIM]
                yd_scr[sc, :, h * SSM_HEADDIM:(h + 1) * SSM_HEADDIM] = jnp.dot(
                    m.astype(BF16), xh.astype(BF16), preferred_element_type=F32)
            y = yd_scr[sc, :, cols] + y_off * exp_scr[sc, 0:s, cols] + dexp_ref[:, cols] * xact
            xw = (xact * exp_scr[sc, s:2 * s, cols]).astype(BF16)
            upd = lax.dot_general(bact, xw, (((0,), (0,)), ((), ())), preferred_element_type=F32)
            st_scr[g] = st * exp_scr[sc, 2 * s:2 * s + 1, cols] + upd

            yg = y * _silu(z_ref[rows, cols])
            rn = lax.rsqrt(jnp.mean(yg * yg, axis=-1, keepdims=True) + EPS)
            y_ref[rows, cols] = (yg * rn * nw_ref[:, cols]).astype(y_ref.dtype)

    @pl.when(c == n_chunks - 1)
    def _():
        for g in range(SSM_GROUPS):
            sout_ref[0, g * GROUP_COLS:(g + 1) * GROUP_COLS, :] = st_scr[g].T


def _ssd_branch(rest, dt, hist, s0, convp, hp, dexp, ssm_nw, expand, *, n_streams, q_rows, sub_rows, n_chunks):
    t = rest.shape[0]
    n_sub = q_rows // sub_rows
    assert n_sub * sub_rows == q_rows

    def rows(b, c):
        return b * n_chunks + c

    def const(shape):
        return pl.BlockSpec(shape, lambda b, c: (0,) * len(shape))

    kern = functools.partial(_ssd_kernel, q_rows=q_rows, sub_rows=sub_rows, n_chunks=n_chunks)
    return pl.pallas_call(
        kern,
        grid=(n_streams, n_chunks),
        in_specs=[
            pl.BlockSpec((q_rows, D_SSM), lambda b, c: (rows(b, c), R_X // D_SSM)),
            pl.BlockSpec((q_rows, BC_COLS), lambda b, c: (rows(b, c), R_BC // BC_COLS)),
            pl.BlockSpec((q_rows, D_SSM), lambda b, c: (rows(b, c), R_Z // D_SSM)),
            pl.BlockSpec((q_rows, DT_COLS), lambda b, c: (rows(b, c), 0)),
            pl.BlockSpec((1, SUBLANES, CONV_DIM), lambda b, c: (b, 0, 0)),
            pl.BlockSpec((1, D_SSM, D_STATE), lambda b, c: (b, 0, 0)),
            const((SUBLANES, CONV_DIM)),
            const((SUBLANES, LANES)),
            const((1, D_SSM)),
            const((1, D_SSM)),
            const((N_SPLIT * LANES, D_SSM)),
        ],
        out_specs=[
            pl.BlockSpec((q_rows, D_SSM), lambda b, c: (rows(b, c), 0)),
            pl.BlockSpec((1, D_SSM, D_STATE), lambda b, c: (b, 0, 0)),
        ],
        out_shape=[
            jax.ShapeDtypeStruct((t, D_SSM), BF16),
            jax.ShapeDtypeStruct((n_streams, D_SSM, D_STATE), F32),
        ],
        scratch_shapes=[
            pltpu.VMEM((SUBLANES + q_rows, CONV_DIM), F32),
            pltpu.VMEM((q_rows, CONV_DIM), F32),
            pltpu.VMEM((SSM_GROUPS, D_STATE, GROUP_COLS), F32),
            pltpu.VMEM((n_sub, sub_rows, D_SSM), F32),
            pltpu.VMEM((n_sub, 2 * sub_rows + SUBLANES, D_SSM), F32),
        ],
        compiler_params=_compiler_params(("parallel", "arbitrary")),
        name="ssd_branch",
    )(rest, rest, rest, dt, hist, s0, convp, hp, dexp, ssm_nw, expand)


def _softmax_stage(s_scr, p_scr, inv_scr, bias_ref, h, q_rows):
    exp2_scale = ATT_SCALE * LOG2E
    slab = min(SOFTMAX_SLAB, q_rows)
    bias_rows = bias_ref.shape[1]
    for r0 in range(0, q_rows, slab):
        b0 = r0 % bias_rows
        u = s_scr[h, r0:r0 + slab, :] + bias_ref[h, b0:b0 + slab, :]
        m = jnp.max(u, axis=-1, keepdims=True)
        e = jnp.exp2((u - m) * exp2_scale)
        p_scr[h, r0:r0 + slab, :] = e.astype(BF16)
        inv = 1.0 / jnp.sum(e, axis=-1, keepdims=True)
        inv_scr[h, r0:r0 + slab, :] = jnp.broadcast_to(inv, (slab, ATT_HEAD_DIM))


def _head_pipeline(qk_stage, softmax_stage, pv_stage):
    skew = HEAD_PIPELINE_SKEW
    for step in range(ATT_HEADS + 2 * skew):
        if step < ATT_HEADS:
            qk_stage(step)
        if skew <= step < ATT_HEADS + skew:
            softmax_stage(step - skew)
        if step >= 2 * skew:
            pv_stage(step - 2 * skew)


def _prompt_attn_kernel(*refs, n_hist_blocks):
    n_win = n_hist_blocks + 2
    tq = ATT_Q_ROWS
    q_ref, g_ref = refs[0], refs[1]
    kv_refs = refs[2:2 + n_win]
    bias_ref, o_ref, s_scr, p_scr, inv_scr = refs[2 + n_win:]
    first_block = 2 * pl.program_id(0) - n_hist_blocks

    def block_rows(blk):
        lo = 0 if blk <= n_hist_blocks else tq
        hi = 2 * tq if blk >= 1 else tq
        return lo, hi

    def qk_stage(h, mask_history):
        for blk in range(n_win):
            lo, hi = block_rows(blk)
            s = lax.dot_general(q_ref[0, h, lo:hi, :], kv_refs[blk][QKV_K, h], (((1,), (1,)), ((), ())),
                                preferred_element_type=F32)
            if mask_history and blk < n_hist_blocks:
                s = jnp.where(first_block + blk >= 0, s, MASK_VALUE)
            if lo == 0:
                s_scr[h, 0:tq, blk * tq:(blk + 1) * tq] = s[0:tq]
            if hi == 2 * tq:
                s_scr[h, tq:2 * tq, (blk - 1) * tq:blk * tq] = s[tq - lo:2 * tq - lo]

    def pv_stage(h):
        sl = slice(h * ATT_HEAD_DIM, (h + 1) * ATT_HEAD_DIM)
        o_first = jnp.zeros((tq, ATT_HEAD_DIM), F32)
        o_second = jnp.zeros((tq, ATT_HEAD_DIM), F32)
        for blk in range(n_win):
            lo, hi = block_rows(blk)
            parts = []
            if lo == 0:
                parts.append(p_scr[h, 0:tq, blk * tq:(blk + 1) * tq])
            if hi == 2 * tq:
                parts.append(p_scr[h, tq:2 * tq, (blk - 1) * tq:blk * tq])
            p = parts[0] if len(parts) == 1 else jnp.concatenate(parts, axis=0)
            o = jnp.dot(p, kv_refs[blk][QKV_V, h], preferred_element_type=F32)
            if lo == 0:
                o_first = o_first + o[0:tq]
            if hi == 2 * tq:
                o_second = o_second + o[tq - lo:2 * tq - lo]
        o = jnp.concatenate([o_first, o_second], axis=0)
        o_ref[:, sl] = (o * inv_scr[h] * _silu(g_ref[:, sl])).astype(o_ref.dtype)

    softmax_stage = functools.partial(_softmax_stage, s_scr, p_scr, inv_scr, bias_ref, q_rows=2 * tq)

    @pl.when(first_block < 0)
    def _():
        _head_pipeline(functools.partial(qk_stage, mask_history=True), softmax_stage, pv_stage)

    @pl.when(first_block >= 0)
    def _():
        _head_pipeline(functools.partial(qk_stage, mask_history=False), softmax_stage, pv_stage)


def _prompt_attention(qkv, rest, bias):
    t = rest.shape[0]
    tq = ATT_Q_ROWS
    n_hist_blocks = BAND_ROWS // tq
    n_win = n_hist_blocks + 2
    span = BAND_ROWS + tq
    assert (QKV_K, QKV_V) == (0, 1)
    kv_block = (2, ATT_HEADS, tq, ATT_HEAD_DIM)

    def kv_spec(blk):
        return pl.BlockSpec(kv_block, lambda i: (0, 0, jnp.maximum(2 * i - n_hist_blocks + blk, 0), 0))

    in_specs = [pl.BlockSpec((1, ATT_HEADS, 2 * tq, ATT_HEAD_DIM), lambda i: (QKV_Q, 0, i, 0)),
                pl.BlockSpec((2 * tq, D_ATT), lambda i: (i, R_G // D_ATT))]
    in_specs += [kv_spec(blk) for blk in range(n_win)]
    in_specs += [pl.BlockSpec((ATT_HEADS, tq, span), lambda i: (0, 0, 0))]
    kern = functools.partial(_prompt_attn_kernel, n_hist_blocks=n_hist_blocks)
    return pl.pallas_call(
        kern,
        grid=(t // (2 * tq),),
        in_specs=in_specs,
        out_specs=pl.BlockSpec((2 * tq, D_ATT), lambda i: (i, 0)),
        out_shape=jax.ShapeDtypeStruct((t, D_ATT), BF16),
        scratch_shapes=[pltpu.VMEM((ATT_HEADS, 2 * tq, span), F32),
                        pltpu.VMEM((ATT_HEADS, 2 * tq, span), BF16),
                        pltpu.VMEM((ATT_HEADS, 2 * tq, ATT_HEAD_DIM), F32)],
        compiler_params=_compiler_params(("parallel",)),
        name="prompt_attention",
    )(qkv, rest, *([qkv] * n_win), bias)


def _sample_attn_kernel(q_ref, kn_ref, vn_ref, g_ref, kc_lo, kc_hi, vc_lo, vc_hi, bias_ref, o_ref,
                        s_scr, p_scr, inv_scr, *, q_rows):
    new_pad = LANES - q_rows
    zpad = jnp.zeros((new_pad, ATT_HEAD_DIM), BF16)

    def cached(lo_ref, hi_ref, h):
        ref = (lo_ref if h < SUBLANES else hi_ref).reshape(BAND_ROWS * SUBLANES, ATT_HEAD_DIM)
        return ref[pl.ds(h % SUBLANES, BAND_ROWS, stride=SUBLANES), :].astype(BF16)

    def qk_stage(h):
        qh = q_ref[0, h]
        s_scr[h, :, 0:BAND_ROWS] = lax.dot_general(qh, cached(kc_lo, kc_hi, h), (((1,), (1,)), ((), ())),
                                                   preferred_element_type=F32)
        k_new = jnp.concatenate([kn_ref[0, h], zpad], axis=0)
        s_scr[h, :, BAND_ROWS:] = lax.dot_general(qh, k_new, (((1,), (1,)), ((), ())), preferred_element_type=F32)

    def pv_stage(h):
        sl = slice(h * ATT_HEAD_DIM, (h + 1) * ATT_HEAD_DIM)
        v_new = jnp.concatenate([vn_ref[0, h], zpad], axis=0)
        o = jnp.dot(p_scr[h, :, 0:BAND_ROWS], cached(vc_lo, vc_hi, h), preferred_element_type=F32)
        o = o + jnp.dot(p_scr[h, :, BAND_ROWS:], v_new, preferred_element_type=F32)
        o_ref[:, sl] = (o * inv_scr[h] * _silu(g_ref[:, sl])).astype(o_ref.dtype)

    _head_pipeline(qk_stage, functools.partial(_softmax_stage, s_scr, p_scr, inv_scr, bias_ref, q_rows=q_rows),
                   pv_stage)


def _sample_attention(qkv, rest, cache_k, cache_v, bias, *, n_streams, q_rows):
    t = rest.shape[0]
    span_pad = BAND_ROWS + LANES
    kern = functools.partial(_sample_attn_kernel, q_rows=q_rows)

    def new_spec(which):
        return pl.BlockSpec((1, ATT_HEADS, q_rows, ATT_HEAD_DIM), lambda b: (which, 0, b, 0))

    def cache_spec(half):
        return pl.BlockSpec((None, BAND_ROWS, None, SUBLANES, ATT_HEAD_DIM), lambda b: (b, 0, half, 0, 0))

    return pl.pallas_call(
        kern,
        grid=(n_streams,),
        in_specs=[
            new_spec(QKV_Q), new_spec(QKV_K), new_spec(QKV_V),
            pl.BlockSpec((q_rows, D_ATT), lambda b: (b, R_G // D_ATT)),
            cache_spec(0), cache_spec(1), cache_spec(0), cache_spec(1),
            pl.BlockSpec((ATT_HEADS, q_rows, span_pad), lambda b: (0, 0, 0)),
        ],
        out_specs=pl.BlockSpec((q_rows, D_ATT), lambda b: (b, 0)),
        out_shape=jax.ShapeDtypeStruct((t, D_ATT), BF16),
        scratch_shapes=[pltpu.VMEM((ATT_HEADS, q_rows, span_pad), F32),
                        pltpu.VMEM((ATT_HEADS, q_rows, span_pad), BF16),
                        pltpu.VMEM((ATT_HEADS, q_rows, ATT_HEAD_DIM), F32)],
        compiler_params=_compiler_params(("parallel",)),
        name="sample_attention",
    )(qkv, qkv, qkv, rest, cache_k, cache_k, cache_v, cache_v, bias)


def _outproj_kernel(ys_ref, ya_ref, w1_ref, w2_ref, x_ref, o_ref):
    acc = jnp.dot(ys_ref[...], w1_ref[...], preferred_element_type=F32)
    acc = acc + jnp.dot(ya_ref[...], w2_ref[...], preferred_element_type=F32)
    o_ref[...] = x_ref[...] + acc


def _out_projection(y_ssm, y_att, w_out, x, *, tm, tn):
    t = x.shape[0]
    return pl.pallas_call(
        _outproj_kernel,
        grid=(t // tm, D_MODEL // tn),
        in_specs=[
            pl.BlockSpec((tm, D_SSM), lambda i, j: (i, 0)),
            pl.BlockSpec((tm, D_ATT), lambda i, j: (i, 0)),
            pl.BlockSpec((D_SSM, tn), lambda i, j: (0, j)),
            pl.BlockSpec((D_ATT, tn), lambda i, j: (D_SSM // D_ATT, j)),
            pl.BlockSpec((tm, tn), lambda i, j: (i, j)),
        ],
        out_specs=pl.BlockSpec((tm, tn), lambda i, j: (i, j)),
        out_shape=jax.ShapeDtypeStruct((t, D_MODEL), F32),
        compiler_params=_compiler_params(("parallel", "arbitrary")),
        name="out_projection",
    )(y_ssm, y_att, w_out, w_out, x)


def _pad_to(v, size, axis):
    pad = [(0, 0)] * v.ndim
    pad[axis] = (0, size - v.shape[axis])
    return jnp.pad(v, pad)


def _cast_w_in_kernel(main_ref, next_ref, wa_ref, wq_ref, wdt_ref, *, n_a, row_shift):
    j = pl.program_id(0)
    tn = main_ref.shape[0]
    chunk = CAST_CHUNK

    @pl.when(j == n_a)
    def _():
        row_id = lax.broadcasted_iota(jnp.int32, (DT_COLS, main_ref.shape[1]), 0)
        wdt_ref[...] = jnp.where(row_id < row_shift, main_ref[0:DT_COLS, :], 0.0).T.astype(BF16)

    def put(rows_of, out_ref):
        for r0 in range(0, tn, chunk):
            out_ref[:, r0:r0 + chunk] = rows_of(r0).T.astype(BF16)

    @pl.when(j < n_a)
    def _():
        put(lambda r0: main_ref[r0:r0 + chunk, :], wa_ref)

    @pl.when(j >= n_a)
    def _():
        def shifted(r0):
            lo = r0 + row_shift
            if lo + chunk <= tn:
                return main_ref[lo:lo + chunk, :]
            return jnp.concatenate([main_ref[lo:tn, :], next_ref[0:lo + chunk - tn, :]], axis=0)

        put(shifted, wq_ref)


def _cast_w_in(w_in_t):
    tn = PROJ_TN
    n_a = OFF_DT // tn
    n_q = (4 * D_ATT) // tn
    row_shift = OFF_Q - OFF_DT
    assert OFF_DT % tn == 0 and row_shift % SUBLANES == 0 and 0 < row_shift < LANES
    kern = functools.partial(_cast_w_in_kernel, n_a=n_a, row_shift=row_shift)
    return pl.pallas_call(
        kern,
        grid=(n_a + n_q,),
        in_specs=[
            pl.BlockSpec((tn, D_MODEL), lambda j: (j, 0)),
            pl.BlockSpec((LANES, D_MODEL), lambda j: ((j + 1) * (tn // LANES), 0)),
        ],
        out_specs=[
            pl.BlockSpec((D_MODEL, tn), lambda j: (0, jnp.minimum(j, n_a - 1))),
            pl.BlockSpec((D_MODEL, tn), lambda j: (0, jnp.maximum(j - n_a, 0))),
            pl.BlockSpec((D_MODEL, DT_COLS), lambda j: (0, 0)),
        ],
        out_shape=[
            jax.ShapeDtypeStruct((D_MODEL, n_a * tn), BF16),
            jax.ShapeDtypeStruct((D_MODEL, n_q * tn), BF16),
            jax.ShapeDtypeStruct((D_MODEL, DT_COLS), BF16),
        ],
        compiler_params=_compiler_params(("arbitrary",)),
        name="cast_w_in",
    )(w_in_t, w_in_t)


def _prepare_params(norm_w, w_in, conv_w, conv_b, dt_bias, a_log, d_skip, ssm_norm_w, q_norm_w, k_norm_w, w_out):
    w_a, w_qkvg, w_dt = _cast_w_in(w_in.T)
    qk_w = jnp.concatenate([jnp.tile(q_norm_w, ATT_HEADS), jnp.tile(k_norm_w, ATT_HEADS)]).reshape(1, 2 * D_ATT)
    convp = _pad_to(jnp.concatenate([conv_w, conv_b[None]], axis=0), SUBLANES, 0)
    hp = _pad_to(_pad_to(jnp.stack([dt_bias, a_log]), LANES, 1), SUBLANES, 0)
    dexp = jnp.repeat(d_skip, SSM_HEADDIM).reshape(1, D_SSM)
    head_of_col = np.arange(D_SSM) // SSM_HEADDIM
    expand = (np.arange(LANES)[:, None] == head_of_col[None, :]).astype(np.float32)
    expand = jnp.asarray(np.tile(expand, (N_SPLIT, 1)), dtype=BF16)
    return dict(norm_w=norm_w.reshape(1, D_MODEL), w_qkvg=w_qkvg, w_a=w_a, w_dt=w_dt, qk_w=qk_w,
                convp=convp, hp=hp, dexp=dexp, ssm_nw=ssm_norm_w.reshape(1, D_SSM), expand=expand,
                w_out=w_out.astype(BF16))


def _rel_bias_table(rel_bias, q_rows, hist_rows, span_pad, band_chunk):
    n_heads = rel_bias.shape[0]
    const_cols = max(0, (hist_rows - REL_CLIP) // LANES * LANES)
    hist_v, span_v = hist_rows - const_cols, span_pad - const_cols
    period = span_v + q_rows
    k = np.arange(period)
    rel = np.clip(hist_v + q_rows - 1 - k, -REL_CLIP, REL_CLIP) + REL_CLIP
    onehot = jnp.asarray(np.eye(N_REL, dtype=np.float32)[:, rel])
    diag_row = jnp.dot(rel_bias.astype(F32), onehot, precision=lax.Precision.HIGHEST)
    flat = jnp.tile(diag_row, (1, q_rows))[:, :q_rows * (period - 1)]
    table = flat.reshape(n_heads, q_rows, period - 1)[:, :, q_rows - 1:q_rows - 1 + span_v]
    far = jnp.broadcast_to(rel_bias[:, N_REL - 1].astype(F32)[:, None, None], (n_heads, q_rows, const_cols))
    table = jnp.concatenate([far, table], axis=2)
    i_idx = np.arange(q_rows)[:, None]
    j_idx = np.arange(span_pad)[None, :]
    valid = j_idx < hist_rows + q_rows
    if band_chunk is not None:
        start = (i_idx // band_chunk) * band_chunk
        valid = valid & (j_idx >= start) & (j_idx < start + hist_rows + band_chunk)
    return jnp.where(jnp.asarray(np.broadcast_to(valid, (q_rows, span_pad)))[None], table / ATT_SCALE, MASK_VALUE)


def _layer(x, hist, s0, cache_k, cache_v, prm, rel_bias, *, n_streams, seq, prompt):
    t = n_streams * seq
    x2 = x.reshape(t, D_MODEL)
    proj = _in_projection(x2, prm["norm_w"], prm["w_qkvg"], prm["w_a"], prm["w_dt"], prm["qk_w"],
                          tm=PROMPT_PROJ_TM if prompt else SAMPLE_PROJ_TM, emit_kv=not prompt)
    if prompt:
        qkv, rest, dt = proj
        q_rows, sub_rows, n_chunks = SSD_CHUNK_ROWS, SSD_ROWS, seq // SSD_CHUNK_ROWS
        kv_rows = BAND_ROWS
        k_new, v_new = _kv_rows(x2, prm["norm_w"], prm["w_qkvg"], prm["qk_w"], first_row=t - kv_rows,
                                n_rows=kv_rows, tm=KV_ROWS_TM)
    else:
        qkv, rest, dt, k_new, v_new = proj
        q_rows, sub_rows, n_chunks = seq, seq, 1
        kv_rows = t
    y_ssm, s_new = _ssd_branch(rest, dt, hist, s0, prm["convp"], prm["hp"], prm["dexp"], prm["ssm_nw"],
                               prm["expand"], n_streams=n_streams, q_rows=q_rows, sub_rows=sub_rows,
                               n_chunks=n_chunks)
    if prompt:
        bias = _rel_bias_table(rel_bias, ATT_Q_ROWS, BAND_ROWS, BAND_ROWS + ATT_Q_ROWS, CHUNK)
        y_att = _prompt_attention(qkv, rest, bias)
    else:
        bias = _rel_bias_table(rel_bias, seq, BAND_ROWS, BAND_ROWS + LANES, None)
        y_att = _sample_attention(qkv, rest, cache_k, cache_v, bias, n_streams=n_streams, q_rows=seq)
    y = _out_projection(y_ssm, y_att, prm["w_out"], x2, tm=OUT_PROJ_TM, tn=OUT_PROJ_TN)
    new_conv = rest.reshape(n_streams, seq, R_COLS)[:, seq - (CONV_WIDTH - 1):, R_X:]
    kv_streams = kv_rows // n_streams
    kh = k_new.reshape(n_streams, kv_streams, ATT_HEADS, ATT_HEAD_DIM)
    vh = v_new.reshape(n_streams, kv_streams, ATT_HEADS, ATT_HEAD_DIM)
    new_ssm = s_new.reshape(n_streams, SSM_HEADS, SSM_HEADDIM, D_STATE)
    return y.reshape(n_streams, seq, D_MODEL), new_conv, new_ssm, kh, vh


def kernel(x_prompt, x_sample, state_conv, state_ssm, cache_k, cache_v, norm_w, w_in, conv_w, conv_b, dt_bias, a_log, d_skip, ssm_norm_w, q_norm_w, k_norm_w, rel_bias, w_out):
    bp, lp, _ = x_prompt.shape
    bs, ls, _ = x_sample.shape
    assert bp == 1 and lp >= BAND_ROWS and BAND_ROWS % KV_ROWS_TM == 0
    assert all(lp % rows == 0 for rows in (PROMPT_PROJ_TM, OUT_PROJ_TM, SSD_CHUNK_ROWS, 2 * ATT_Q_ROWS))
    assert ls % (2 * SUBLANES) == 0 and ls <= LANES and cache_k.shape[2] == BAND_ROWS
    assert all((bs * ls) % rows == 0 for rows in (SAMPLE_PROJ_TM, OUT_PROJ_TM))
    assert norm_w.shape[0] == 1
    prm = _prepare_params(norm_w[0], w_in[0], conv_w[0], conv_b[0], dt_bias[0], a_log[0], d_skip[0],
                          ssm_norm_w[0], q_norm_w[0], k_norm_w[0], w_out[0])
    rb = rel_bias[0]

    zero_hist = jnp.zeros((bp, SUBLANES, CONV_DIM), F32)
    zero_state = jnp.zeros((bp, D_SSM, D_STATE), F32)
    yp, c1, s1, k1, v1 = _layer(x_prompt, zero_hist, zero_state, None, None, prm, rb,
                                n_streams=bp, seq=lp, prompt=True)

    hist = jnp.pad(state_conv[0], ((0, 0), (SUBLANES - (CONV_WIDTH - 1), 0), (0, 0)))
    s0 = state_ssm[0].reshape(bs, D_SSM, D_STATE)
    half_heads = (bs, BAND_ROWS, ATT_HEADS // SUBLANES, SUBLANES, ATT_HEAD_DIM)
    ck = cache_k[0].reshape(half_heads)
    cv = cache_v[0].reshape(half_heads)
    ys, c2, s2, k2, v2 = _layer(x_sample, hist, s0, ck, cv, prm, rb, n_streams=bs, seq=ls, prompt=False)

    return (yp, ys, c1[None], s1[None], k1[None], v1[None], c2[None], s2[None], k2[None], v2[None])
```

```python
import functools

import jax
import jax.numpy as jnp
import numpy as np
from jax import lax
from jax.experimental import pallas as pl
from jax.experimental.pallas import tpu as pltpu

F32 = jnp.float32
BF16 = jnp.bfloat16

D_MODEL = 2048
CHUNK = 64
LEFT_CHUNKS = 8
BAND_ROWS = LEFT_CHUNKS * CHUNK
D_SSM = 2048
D_ATT = 2048
SSM_HEADDIM = 64
SSM_HEADS = 32
SSM_GROUPS = 4
HEADS_PER_GROUP = SSM_HEADS // SSM_GROUPS
GROUP_COLS = HEADS_PER_GROUP * SSM_HEADDIM
D_STATE = 128
BC_COLS = 2 * SSM_GROUPS * D_STATE
CONV_WIDTH = 4
CONV_DIM = D_SSM + BC_COLS
ATT_HEAD_DIM = 128
ATT_HEADS = 16
REL_CLIP = 128
N_REL = 2 * REL_CLIP + 1
EPS = 1e-6
OFF_Z = 0
OFF_XBC = OFF_Z + D_SSM
OFF_DT = OFF_XBC + CONV_DIM
OFF_Q = OFF_DT + SSM_HEADS
OFF_K = OFF_Q + D_ATT
OFF_V = OFF_K + D_ATT
OFF_G = OFF_V + D_ATT

LANES = 128
SUBLANES = 8
VMEM_LIMIT_BYTES = 56 * 1024 * 1024

R_G = 0
R_Z = R_G + D_ATT
R_X = R_Z + D_SSM
R_BC = R_X + D_SSM
R_COLS = R_BC + BC_COLS
DT_COLS = LANES
QKV_K, QKV_V, QKV_Q = 0, 1, 2
N_SPLIT = 3

PROJ_TN = 1024
PROMPT_PROJ_TM = 1024
SAMPLE_PROJ_TM = 512
KV_ROWS_TM = 512
OUT_PROJ_TM = 1024
OUT_PROJ_TN = 1024
CAST_CHUNK = 256
SSD_ROWS = 128
SSD_CHUNK_ROWS = 512
ATT_Q_ROWS = 128
SOFTMAX_SLAB = 32
HEAD_PIPELINE_SKEW = 2
MASK_VALUE = -1e30
ATT_SCALE = ATT_HEAD_DIM ** -0.5
LOG2E = 1.4426950408889634


def _silu(v):
    return v * (1.0 / (1.0 + jnp.exp2(v * (-LOG2E))))


def _compiler_params(semantics):
    return pltpu.CompilerParams(dimension_semantics=semantics, vmem_limit_bytes=VMEM_LIMIT_BYTES)


def _normed_rows(x_ref, nw_ref):
    x = x_ref[...]
    ms = jnp.mean(x * x, axis=-1, keepdims=True)
    return (x * lax.rsqrt(ms + EPS) * nw_ref[...]).astype(BF16)


def _store_cache_rows(out_ref, head_rows, head):
    out_ref[pl.ds(head, head_rows.shape[0], stride=ATT_HEADS), :] = head_rows


def _inproj_kernel(*refs, n_qk, n_qkv, n_g, emit_kv):
    x_ref, nw_ref, wqkvg_ref, wa_ref, wdt_ref, qkw_ref, qkv_ref, rest_ref, dt_ref = refs[:9]
    k_out, v_out = refs[9:11] if emit_kv else (None, None)
    h_scr = refs[-1]
    j = pl.program_id(1)
    heads_per_tile = PROJ_TN // ATT_HEAD_DIM
    tiles_per_proj = n_qkv - n_qk
    first_head = (j % tiles_per_proj) * heads_per_tile

    @pl.when(j == 0)
    def _():
        h = _normed_rows(x_ref, nw_ref)
        h_scr[...] = h
        dt_ref[...] = jnp.dot(h, wdt_ref[...], preferred_element_type=F32)

    def qkv_tile(use_norm):
        acc = jnp.dot(h_scr[...], wqkvg_ref[...], preferred_element_type=F32)
        for hh in range(heads_per_tile):
            sl = slice(hh * ATT_HEAD_DIM, (hh + 1) * ATT_HEAD_DIM)
            head = acc[:, sl]
            if use_norm:
                r = lax.rsqrt(jnp.mean(head * head, axis=-1, keepdims=True) + EPS)
                head = head * r * qkw_ref[:, sl]
            qkv_ref[0, hh] = head.astype(BF16)
            if emit_kv:
                _store_cache_rows(k_out if use_norm else v_out, head, first_head + hh)

    pl.when(j < n_qk)(functools.partial(qkv_tile, True))
    pl.when(jnp.logical_and(j >= n_qk, j < n_qkv))(functools.partial(qkv_tile, False))

    @pl.when(jnp.logical_and(j >= n_qkv, j < n_qkv + n_g))
    def _():
        rest_ref[...] = jnp.dot(h_scr[...], wqkvg_ref[...], preferred_element_type=F32)

    @pl.when(j >= n_qkv + n_g)
    def _():
        rest_ref[...] = jnp.dot(h_scr[...], wa_ref[...], preferred_element_type=F32)


def _in_projection(x, norm_w, w_qkvg, w_a, w_dt, qk_w, *, tm, emit_kv):
    t = x.shape[0]
    tn = PROJ_TN
    tiles_per_proj = D_ATT // tn
    n_qk, n_qkv, n_g = 2 * tiles_per_proj, 3 * tiles_per_proj, tiles_per_proj
    n_a = (D_SSM + CONV_DIM) // tn
    heads_per_tile = tn // ATT_HEAD_DIM
    grid = (t // tm, n_qkv + n_g + n_a)
    kern = functools.partial(_inproj_kernel, n_qk=n_qk, n_qkv=n_qkv, n_g=n_g, emit_kv=emit_kv)

    def qkv_index(i, j):
        jj = jnp.minimum(j, n_qkv - 1)
        plane = (jj // tiles_per_proj + QKV_Q) % 3
        return (plane, jj % tiles_per_proj, i, 0)

    out_specs = [
        pl.BlockSpec((1, heads_per_tile, tm, ATT_HEAD_DIM), qkv_index),
        pl.BlockSpec((tm, tn), lambda i, j: (i, jnp.maximum(j - n_qkv, 0))),
        pl.BlockSpec((tm, DT_COLS), lambda i, j: (i, 0)),
    ]
    out_shape = [
        jax.ShapeDtypeStruct((3, ATT_HEADS, t, ATT_HEAD_DIM), BF16),
        jax.ShapeDtypeStruct((t, R_COLS), F32),
        jax.ShapeDtypeStruct((t, DT_COLS), F32),
    ]
    if emit_kv:
        for _ in range(2):
            out_specs.append(pl.BlockSpec((tm * ATT_HEADS, ATT_HEAD_DIM), lambda i, j: (i, 0)))
            out_shape.append(jax.ShapeDtypeStruct((t * ATT_HEADS, ATT_HEAD_DIM), F32))
    return pl.pallas_call(
        kern,
        grid=grid,
        in_specs=[
            pl.BlockSpec((tm, D_MODEL), lambda i, j: (i, 0)),
            pl.BlockSpec((1, D_MODEL), lambda i, j: (0, 0)),
            pl.BlockSpec((D_MODEL, tn), lambda i, j: (0, jnp.minimum(j, n_qkv + n_g - 1))),
            pl.BlockSpec((D_MODEL, tn), lambda i, j: (0, jnp.clip(j - n_qkv - n_g, 0, n_a - 1))),
            pl.BlockSpec((D_MODEL, DT_COLS), lambda i, j: (0, 0)),
            pl.BlockSpec((1, tn), lambda i, j: (0, jnp.minimum(j, n_qk - 1))),
        ],
        out_specs=out_specs,
        out_shape=out_shape,
        scratch_shapes=[pltpu.VMEM((tm, D_MODEL), BF16)],
        compiler_params=_compiler_params(("parallel", "arbitrary")),
        name="in_projection",
    )(x, norm_w, w_qkvg, w_a, w_dt, qk_w)


def _kv_rows_kernel(x_ref, nw_ref, w_ref, qkw_ref, k_out, v_out, h_scr, *, n_k):
    j = pl.program_id(1)
    heads_per_tile = PROJ_TN // ATT_HEAD_DIM
    first_head = (j % n_k) * heads_per_tile

    @pl.when(j == 0)
    def _():
        h_scr[...] = _normed_rows(x_ref, nw_ref)

    def tile(is_k):
        acc = jnp.dot(h_scr[...], w_ref[...], preferred_element_type=F32)
        for hh in range(heads_per_tile):
            sl = slice(hh * ATT_HEAD_DIM, (hh + 1) * ATT_HEAD_DIM)
            head = acc[:, sl]
            if is_k:
                r = lax.rsqrt(jnp.mean(head * head, axis=-1, keepdims=True) + EPS)
                head = head * r * qkw_ref[:, sl]
            _store_cache_rows(k_out if is_k else v_out, head, first_head + hh)

    pl.when(j < n_k)(functools.partial(tile, True))
    pl.when(j >= n_k)(functools.partial(tile, False))


def _kv_rows(x, norm_w, w_qkvg, qk_w, *, first_row, n_rows, tm):
    tn = PROJ_TN
    tiles_per_proj = D_ATT // tn
    row0 = first_row // tm
    kern = functools.partial(_kv_rows_kernel, n_k=tiles_per_proj)
    out_spec = pl.BlockSpec((tm * ATT_HEADS, ATT_HEAD_DIM), lambda i, j: (i, 0))
    out_shape = jax.ShapeDtypeStruct((n_rows * ATT_HEADS, ATT_HEAD_DIM), F32)
    return pl.pallas_call(
        kern,
        grid=(n_rows // tm, 2 * tiles_per_proj),
        in_specs=[
            pl.BlockSpec((tm, D_MODEL), lambda i, j: (row0 + i, 0)),
            pl.BlockSpec((1, D_MODEL), lambda i, j: (0, 0)),
            pl.BlockSpec((D_MODEL, tn), lambda i, j: (0, tiles_per_proj + j)),
            pl.BlockSpec((1, tn), lambda i, j: (0, jnp.minimum(tiles_per_proj + j, 2 * tiles_per_proj - 1))),
        ],
        out_specs=[out_spec, out_spec],
        out_shape=[out_shape, out_shape],
        scratch_shapes=[pltpu.VMEM((tm, D_MODEL), BF16)],
        compiler_params=_compiler_params(("parallel", "arbitrary")),
        name="kv_rows",
    )(x, norm_w, w_qkvg, qk_w)


def _transpose_rows_to_lanes(v):
    q = v.shape[0]
    if q < LANES:
        v = jnp.concatenate([v, jnp.zeros((LANES - q, LANES), v.dtype)], axis=0)
    return v.T[:, 0:q]


def _split_bf16(v, axis=1):
    pieces = []
    rem = v
    for _ in range(N_SPLIT):
        piece = rem.astype(BF16)
        pieces.append(piece)
        rem = rem - piece.astype(F32)
    return jnp.concatenate(pieces, axis=axis)


def _ssd_kernel(x_ref, bc_ref, z_ref, dt_ref, hist_ref, s0_ref, convp_ref, hp_ref, dexp_ref, nw_ref, expand_ref,
                y_ref, sout_ref, conv_scr, act_scr, st_scr, yd_scr, exp_scr, *, q_rows, sub_rows, n_chunks):
    c = pl.program_id(1)
    q = q_rows

    @pl.when(c == 0)
    def _():
        conv_scr[0:SUBLANES, :] = hist_ref[0]
        for g in range(SSM_GROUPS):
            st_scr[g] = s0_ref[0, g * GROUP_COLS:(g + 1) * GROUP_COLS, :].T

    conv_scr[SUBLANES:SUBLANES + q, 0:D_SSM] = x_ref[...]
    conv_scr[SUBLANES:SUBLANES + q, D_SSM:] = bc_ref[...]

    for c0 in range(0, CONV_DIM, GROUP_COLS):
        cols = slice(c0, c0 + GROUP_COLS)
        xp = conv_scr[:, cols]
        conv = convp_ref[CONV_WIDTH:CONV_WIDTH + 1, cols] + convp_ref[CONV_WIDTH - 1:CONV_WIDTH, cols] * xp[SUBLANES:]
        for shift in range(1, CONV_WIDTH):
            tap = CONV_WIDTH - 1 - shift
            conv = conv + convp_ref[tap:tap + 1, cols] * pltpu.roll(xp, shift, axis=0)[SUBLANES:]
        act_scr[:, cols] = _silu(conv)
    conv_scr[0:SUBLANES, :] = conv_scr[q:q + SUBLANES, :]

    s = sub_rows
    ii = lax.broadcasted_iota(jnp.int32, (s, s), 0)
    jj = lax.broadcasted_iota(jnp.int32, (s, s), 1)
    causal = ii >= jj
    tril_pieces = jnp.concatenate([causal.astype(BF16)] * N_SPLIT, axis=1)
    for sc in range(q // s):
        rows = slice(sc * s, (sc + 1) * s)
        v = dt_ref[rows, :] + hp_ref[0:1, :]
        dt = jnp.maximum(v, 0.0) + jnp.log1p(jnp.exp(-jnp.abs(v)))
        a = dt * (-jnp.exp(hp_ref[1:2, :]))
        acum = jnp.dot(tril_pieces, _split_bf16(a, axis=0), preferred_element_type=F32)
        acum2 = acum * LOG2E
        row_t = _transpose_rows_to_lanes(acum2 - jnp.log(dt) * LOG2E)
        last = acum[s - 1:s, :]
        factors = jnp.concatenate(
            [jnp.exp(acum),
             jnp.exp(last - acum) * dt,
             jnp.broadcast_to(jnp.exp(last), (SUBLANES, LANES))], axis=0)
        exp_scr[sc] = jnp.dot(_split_bf16(factors), expand_ref[...], preferred_element_type=F32)

        for g in range(SSM_GROUPS):
            cols = slice(g * GROUP_COLS, (g + 1) * GROUP_COLS)
            xact = act_scr[rows, cols]
            bact = act_scr[rows, D_SSM + g * D_STATE:D_SSM + (g + 1) * D_STATE].astype(BF16)
            cact = act_scr[rows, D_SSM + (SSM_GROUPS + g) * D_STATE:D_SSM + (SSM_GROUPS + g + 1) * D_STATE].astype(BF16)
            cb = lax.dot_general(cact, bact, (((1,), (1,)), ((), ())), preferred_element_type=F32)
            st = st_scr[g]
            y_off = jnp.dot(cact, st.astype(BF16), preferred_element_type=F32)
            for r in range(HEADS_PER_GROUP):
                h = g * HEADS_PER_GROUP + r
                m = cb * jnp.exp2(jnp.where(causal, acum2[:, h:h + 1] - row_t[h:h + 1, :], -jnp.inf))
                xh = xact[:, r * SSM_HEADDIM:(r + 1) * SSM_HEADDIM]
                yd_scr[sc, :, h * SSM_HEADDIM:(h + 1) * SSM_HEADDIM] = jnp.dot(
                    m.astype(BF16), xh.astype(BF16), preferred_element_type=F32)
            y = yd_scr[sc, :, cols] + y_off * exp_scr[sc, 0:s, cols] + dexp_ref[:, cols] * xact
            xw = (xact * exp_scr[sc, s:2 * s, cols]).astype(BF16)
            upd = lax.dot_general(bact, xw, (((0,), (0,)), ((), ())), preferred_element_type=F32)
            st_scr[g] = st * exp_scr[sc, 2 * s:2 * s + 1, cols] + upd

            yg = y * _silu(z_ref[rows, cols])
            rn = lax.rsqrt(jnp.mean(yg * yg, axis=-1, keepdims=True) + EPS)
            y_ref[rows, cols] = (yg * rn * nw_ref[:, cols]).astype(y_ref.dtype)

    @pl.when(c == n_chunks - 1)
    def _():
        for g in range(SSM_GROUPS):
            sout_ref[0, g * GROUP_COLS:(g + 1) * GROUP_COLS, :] = st_scr[g].T


def _ssd_branch(rest, dt, hist, s0, convp, hp, dexp, ssm_nw, expand, *, n_streams, q_rows, sub_rows, n_chunks):
    t = rest.shape[0]
    n_sub = q_rows // sub_rows
    assert n_sub * sub_rows == q_rows

    def rows(b, c):
        return b * n_chunks + c

    def const(shape):
        return pl.BlockSpec(shape, lambda b, c: (0,) * len(shape))

    kern = functools.partial(_ssd_kernel, q_rows=q_rows, sub_rows=sub_rows, n_chunks=n_chunks)
    return pl.pallas_call(
        kern,
        grid=(n_streams, n_chunks),
        in_specs=[
            pl.BlockSpec((q_rows, D_SSM), lambda b, c: (rows(b, c), R_X // D_SSM)),
            pl.BlockSpec((q_rows, BC_COLS), lambda b, c: (rows(b, c), R_BC // BC_COLS)),
            pl.BlockSpec((q_rows, D_SSM), lambda b, c: (rows(b, c), R_Z // D_SSM)),
            pl.BlockSpec((q_rows, DT_COLS), lambda b, c: (rows(b, c), 0)),
            pl.BlockSpec((1, SUBLANES, CONV_DIM), lambda b, c: (b, 0, 0)),
            pl.BlockSpec((1, D_SSM, D_STATE), lambda b, c: (b, 0, 0)),
            const((SUBLANES, CONV_DIM)),
            const((SUBLANES, LANES)),
            const((1, D_SSM)),
            const((1, D_SSM)),
            const((N_SPLIT * LANES, D_SSM)),
        ],
        out_specs=[
            pl.BlockSpec((q_rows, D_SSM), lambda b, c: (rows(b, c), 0)),
            pl.BlockSpec((1, D_SSM, D_STATE), lambda b, c: (b, 0, 0)),
        ],
        out_shape=[
            jax.ShapeDtypeStruct((t, D_SSM), BF16),
            jax.ShapeDtypeStruct((n_streams, D_SSM, D_STATE), F32),
        ],
        scratch_shapes=[
            pltpu.VMEM((SUBLANES + q_rows, CONV_DIM), F32),
            pltpu.VMEM((q_rows, CONV_DIM), F32),
            pltpu.VMEM((SSM_GROUPS, D_STATE, GROUP_COLS), F32),
            pltpu.VMEM((n_sub, sub_rows, D_SSM), F32),
            pltpu.VMEM((n_sub, 2 * sub_rows + SUBLANES, D_SSM), F32),
        ],
        compiler_params=_compiler_params(("parallel", "arbitrary")),
        name="ssd_branch",
    )(rest, rest, rest, dt, hist, s0, convp, hp, dexp, ssm_nw, expand)


def _softmax_stage(s_scr, p_scr, inv_scr, bias_ref, h, q_rows):
    exp2_scale = ATT_SCALE * LOG2E
    slab = min(SOFTMAX_SLAB, q_rows)
    bias_rows = bias_ref.shape[1]
    for r0 in range(0, q_rows, slab):
        b0 = r0 % bias_rows
        u = s_scr[h, r0:r0 + slab, :] + bias_ref[h, b0:b0 + slab, :]
        m = jnp.max(u, axis=-1, keepdims=True)
        e = jnp.exp2((u - m) * exp2_scale)
        p_scr[h, r0:r0 + slab, :] = e.astype(BF16)
        inv = 1.0 / jnp.sum(e, axis=-1, keepdims=True)
        inv_scr[h, r0:r0 + slab, :] = jnp.broadcast_to(inv, (slab, ATT_HEAD_DIM))


def _head_pipeline(qk_stage, softmax_stage, pv_stage):
    skew = HEAD_PIPELINE_SKEW
    for step in range(ATT_HEADS + 2 * skew):
        if step < ATT_HEADS:
            qk_stage(step)
        if skew <= step < ATT_HEADS + skew:
            softmax_stage(step - skew)
        if step >= 2 * skew:
            pv_stage(step - 2 * skew)


def _prompt_attn_kernel(*refs, n_hist_blocks):
    n_win = n_hist_blocks + 2
    tq = ATT_Q_ROWS
    q_ref, g_ref = refs[0], refs[1]
    kv_refs = refs[2:2 + n_win]
    bias_ref, o_ref, s_scr, p_scr, inv_scr = refs[2 + n_win:]
    first_block = 2 * pl.program_id(0) - n_hist_blocks

    def block_rows(blk):
        lo = 0 if blk <= n_hist_blocks else tq
        hi = 2 * tq if blk >= 1 else tq
        return lo, hi

    def qk_stage(h, mask_history):
        for blk in range(n_win):
            lo, hi = block_rows(blk)
            s = lax.dot_general(q_ref[0, h, lo:hi, :], kv_refs[blk][QKV_K, h], (((1,), (1,)), ((), ())),
                                preferred_element_type=F32)
            if mask_history and blk < n_hist_blocks:
                s = jnp.where(first_block + blk >= 0, s, MASK_VALUE)
            if lo == 0:
                s_scr[h, 0:tq, blk * tq:(blk + 1) * tq] = s[0:tq]
            if hi == 2 * tq:
                s_scr[h, tq:2 * tq, (blk - 1) * tq:blk * tq] = s[tq - lo:2 * tq - lo]

    def pv_stage(h):
        sl = slice(h * ATT_HEAD_DIM, (h + 1) * ATT_HEAD_DIM)
        o_first = jnp.zeros((tq, ATT_HEAD_DIM), F32)
        o_second = jnp.zeros((tq, ATT_HEAD_DIM), F32)
        for blk in range(n_win):
            lo, hi = block_rows(blk)
            parts = []
            if lo == 0:
                parts.append(p_scr[h, 0:tq, blk * tq:(blk + 1) * tq])
            if hi == 2 * tq:
                parts.append(p_scr[h, tq:2 * tq, (blk - 1) * tq:blk * tq])
            p = parts[0] if len(parts) == 1 else jnp.concatenate(parts, axis=0)
            o = jnp.dot(p, kv_refs[blk][QKV_V, h], preferred_element_type=F32)
            if lo == 0:
                o_first = o_first + o[0:tq]
            if hi == 2 * tq:
                o_second = o_second + o[tq - lo:2 * tq - lo]
        o = jnp.concatenate([o_first, o_second], axis=0)
        o_ref[:, sl] = (o * inv_scr[h] * _silu(g_ref[:, sl])).astype(o_ref.dtype)

    softmax_stage = functools.partial(_softmax_stage, s_scr, p_scr, inv_scr, bias_ref, q_rows=2 * tq)

    @pl.when(first_block < 0)
    def _():
        _head_pipeline(functools.partial(qk_stage, mask_history=True), softmax_stage, pv_stage)

    @pl.when(first_block >= 0)
    def _():
        _head_pipeline(functools.partial(qk_stage, mask_history=False), softmax_stage, pv_stage)


def _prompt_attention(qkv, rest, bias):
    t = rest.shape[0]
    tq = ATT_Q_ROWS
    n_hist_blocks = BAND_ROWS // tq
    n_win = n_hist_blocks + 2
    span = BAND_ROWS + tq
    assert (QKV_K, QKV_V) == (0, 1)
    kv_block = (2, ATT_HEADS, tq, ATT_HEAD_DIM)

    def kv_spec(blk):
        return pl.BlockSpec(kv_block, lambda i: (0, 0, jnp.maximum(2 * i - n_hist_blocks + blk, 0), 0))

    in_specs = [pl.BlockSpec((1, ATT_HEADS, 2 * tq, ATT_HEAD_DIM), lambda i: (QKV_Q, 0, i, 0)),
                pl.BlockSpec((2 * tq, D_ATT), lambda i: (i, R_G // D_ATT))]
    in_specs += [kv_spec(blk) for blk in range(n_win)]
    in_specs += [pl.BlockSpec((ATT_HEADS, tq, span), lambda i: (0, 0, 0))]
    kern = functools.partial(_prompt_attn_kernel, n_hist_blocks=n_hist_blocks)
    return pl.pallas_call(
        kern,
        grid=(t // (2 * tq),),
        in_specs=in_specs,
        out_specs=pl.BlockSpec((2 * tq, D_ATT), lambda i: (i, 0)),
        out_shape=jax.ShapeDtypeStruct((t, D_ATT), BF16),
        scratch_shapes=[pltpu.VMEM((ATT_HEADS, 2 * tq, span), F32),
                        pltpu.VMEM((ATT_HEADS, 2 * tq, span), BF16),
                        pltpu.VMEM((ATT_HEADS, 2 * tq, ATT_HEAD_DIM), F32)],
        compiler_params=_compiler_params(("parallel",)),
        name="prompt_attention",
    )(qkv, rest, *([qkv] * n_win), bias)


def _sample_attn_kernel(q_ref, kn_ref, vn_ref, g_ref, kc_lo, kc_hi, vc_lo, vc_hi, bias_ref, o_ref,
                        s_scr, p_scr, inv_scr, *, q_rows):
    new_pad = LANES - q_rows
    zpad = jnp.zeros((new_pad, ATT_HEAD_DIM), BF16)

    def cached(lo_ref, hi_ref, h):
        ref = (lo_ref if h < SUBLANES else hi_ref).reshape(BAND_ROWS * SUBLANES, ATT_HEAD_DIM)
        return ref[pl.ds(h % SUBLANES, BAND_ROWS, stride=SUBLANES), :].astype(BF16)

    def qk_stage(h):
        qh = q_ref[0, h]
        s_scr[h, :, 0:BAND_ROWS] = lax.dot_general(qh, cached(kc_lo, kc_hi, h), (((1,), (1,)), ((), ())),
                                                   preferred_element_type=F32)
        k_new = jnp.concatenate([kn_ref[0, h], zpad], axis=0)
        s_scr[h, :, BAND_ROWS:] = lax.dot_general(qh, k_new, (((1,), (1,)), ((), ())), preferred_element_type=F32)

    def pv_stage(h):
        sl = slice(h * ATT_HEAD_DIM, (h + 1) * ATT_HEAD_DIM)
        v_new = jnp.concatenate([vn_ref[0, h], zpad], axis=0)
        o = jnp.dot(p_scr[h, :, 0:BAND_ROWS], cached(vc_lo, vc_hi, h), preferred_element_type=F32)
        o = o + jnp.dot(p_scr[h, :, BAND_ROWS:], v_new, preferred_element_type=F32)
        o_ref[:, sl] = (o * inv_scr[h] * _silu(g_ref[:, sl])).astype(o_ref.dtype)

    _head_pipeline(qk_stage, functools.partial(_softmax_stage, s_scr, p_scr, inv_scr, bias_ref, q_rows=q_rows),
                   pv_stage)


def _sample_attention(qkv, rest, cache_k, cache_v, bias, *, n_streams, q_rows):
    t = rest.shape[0]
    span_pad = BAND_ROWS + LANES
    kern = functools.partial(_sample_attn_kernel, q_rows=q_rows)

    def new_spec(which):
        return pl.BlockSpec((1, ATT_HEADS, q_rows, ATT_HEAD_DIM), lambda b: (which, 0, b, 0))

    def cache_spec(half):
        return pl.BlockSpec((None, BAND_ROWS, None, SUBLANES, ATT_HEAD_DIM), lambda b: (b, 0, half, 0, 0))

    return pl.pallas_call(
        kern,
        grid=(n_streams,),
        in_specs=[
            new_spec(QKV_Q), new_spec(QKV_K), new_spec(QKV_V),
            pl.BlockSpec((q_rows, D_ATT), lambda b: (b, R_G // D_ATT)),
            cache_spec(0), cache_spec(1), cache_spec(0), cache_spec(1),
            pl.BlockSpec((ATT_HEADS, q_rows, span_pad), lambda b: (0, 0, 0)),
        ],
        out_specs=pl.BlockSpec((q_rows, D_ATT), lambda b: (b, 0)),
        out_shape=jax.ShapeDtypeStruct((t, D_ATT), BF16),
        scratch_shapes=[pltpu.VMEM((ATT_HEADS, q_rows, span_pad), F32),
                        pltpu.VMEM((ATT_HEADS, q_rows, span_pad), BF16),
                        pltpu.VMEM((ATT_HEADS, q_rows, ATT_HEAD_DIM), F32)],
        compiler_params=_compiler_params(("parallel",)),
        name="sample_attention",
    )(qkv, qkv, qkv, rest, cache_k, cache_k, cache_v, cache_v, bias)


def _outproj_kernel(ys_ref, ya_ref, w1_ref, w2_ref, x_ref, o_ref):
    acc = jnp.dot(ys_ref[...], w1_ref[...], preferred_element_type=F32)
    acc = acc + jnp.dot(ya_ref[...], w2_ref[...], preferred_element_type=F32)
    o_ref[...] = x_ref[...] + acc


def _out_projection(y_ssm, y_att, w_out, x, *, tm, tn):
    t = x.shape[0]
    return pl.pallas_call(
        _outproj_kernel,
        grid=(t // tm, D_MODEL // tn),
        in_specs=[
            pl.BlockSpec((tm, D_SSM), lambda i, j: (i, 0)),
            pl.BlockSpec((tm, D_ATT), lambda i, j: (i, 0)),
            pl.BlockSpec((D_SSM, tn), lambda i, j: (0, j)),
            pl.BlockSpec((D_ATT, tn), lambda i, j: (D_SSM // D_ATT, j)),
            pl.BlockSpec((tm, tn), lambda i, j: (i, j)),
        ],
        out_specs=pl.BlockSpec((tm, tn), lambda i, j: (i, j)),
        out_shape=jax.ShapeDtypeStruct((t, D_MODEL), F32),
        compiler_params=_compiler_params(("parallel", "arbitrary")),
        name="out_projection",
    )(y_ssm, y_att, w_out, w_out, x)


def _pad_to(v, size, axis):
    pad = [(0, 0)] * v.ndim
    pad[axis] = (0, size - v.shape[axis])
    return jnp.pad(v, pad)


def _cast_w_in_kernel(main_ref, next_ref, wa_ref, wq_ref, wdt_ref, *, n_a, row_shift):
    j = pl.program_id(0)
    tn = main_ref.shape[0]
    chunk = CAST_CHUNK

    @pl.when(j == n_a)
    def _():
        row_id = lax.broadcasted_iota(jnp.int32, (DT_COLS, main_ref.shape[1]), 0)
        wdt_ref[...] = jnp.where(row_id < row_shift, main_ref[0:DT_COLS, :], 0.0).T.astype(BF16)

    def put(rows_of, out_ref):
        for r0 in range(0, tn, chunk):
            out_ref[:, r0:r0 + chunk] = rows_of(r0).T.astype(BF16)

    @pl.when(j < n_a)
    def _():
        put(lambda r0: main_ref[r0:r0 + chunk, :], wa_ref)

    @pl.when(j >= n_a)
    def _():
        def shifted(r0):
            lo = r0 + row_shift
            if lo + chunk <= tn:
                return main_ref[lo:lo + chunk, :]
            return jnp.concatenate([main_ref[lo:tn, :], next_ref[0:lo + chunk - tn, :]], axis=0)

        put(shifted, wq_ref)


def _cast_w_in(w_in_t):
    tn = PROJ_TN
    n_a = OFF_DT // tn
    n_q = (4 * D_ATT) // tn
    row_shift = OFF_Q - OFF_DT
    assert OFF_DT % tn == 0 and row_shift % SUBLANES == 0 and 0 < row_shift < LANES
    kern = functools.partial(_cast_w_in_kernel, n_a=n_a, row_shift=row_shift)
    return pl.pallas_call(
        kern,
        grid=(n_a + n_q,),
        in_specs=[
            pl.BlockSpec((tn, D_MODEL), lambda j: (j, 0)),
            pl.BlockSpec((LANES, D_MODEL), lambda j: ((j + 1) * (tn // LANES), 0)),
        ],
        out_specs=[
            pl.BlockSpec((D_MODEL, tn), lambda j: (0, jnp.minimum(j, n_a - 1))),
            pl.BlockSpec((D_MODEL, tn), lambda j: (0, jnp.maximum(j - n_a, 0))),
            pl.BlockSpec((D_MODEL, DT_COLS), lambda j: (0, 0)),
        ],
        out_shape=[
            jax.ShapeDtypeStruct((D_MODEL, n_a * tn), BF16),
            jax.ShapeDtypeStruct((D_MODEL, n_q * tn), BF16),
            jax.ShapeDtypeStruct((D_MODEL, DT_COLS), BF16),
        ],
        compiler_params=_compiler_params(("arbitrary",)),
        name="cast_w_in",
    )(w_in_t, w_in_t)


def _prepare_params(norm_w, w_in, conv_w, conv_b, dt_bias, a_log, d_skip, ssm_norm_w, q_norm_w, k_norm_w, w_out):
    w_a, w_qkvg, w_dt = _cast_w_in(w_in.T)
    qk_w = jnp.concatenate([jnp.tile(q_norm_w, ATT_HEADS), jnp.tile(k_norm_w, ATT_HEADS)]).reshape(1, 2 * D_ATT)
    convp = _pad_to(jnp.concatenate([conv_w, conv_b[None]], axis=0), SUBLANES, 0)
    hp = _pad_to(_pad_to(jnp.stack([dt_bias, a_log]), LANES, 1), SUBLANES, 0)
    dexp = jnp.repeat(d_skip, SSM_HEADDIM).reshape(1, D_SSM)
    head_of_col = np.arange(D_SSM) // SSM_HEADDIM
    expand = (np.arange(LANES)[:, None] == head_of_col[None, :]).astype(np.float32)
    expand = jnp.asarray(np.tile(expand, (N_SPLIT, 1)), dtype=BF16)
    return dict(norm_w=norm_w.reshape(1, D_MODEL), w_qkvg=w_qkvg, w_a=w_a, w_dt=w_dt, qk_w=qk_w,
                convp=convp, hp=hp, dexp=dexp, ssm_nw=ssm_norm_w.reshape(1, D_SSM), expand=expand,
                w_out=w_out.astype(BF16))


def _rel_bias_table(rel_bias, q_rows, hist_rows, span_pad, band_chunk):
    n_heads = rel_bias.shape[0]
    const_cols = max(0, (hist_rows - REL_CLIP) // LANES * LANES)
    hist_v, span_v = hist_rows - const_cols, span_pad - const_cols
    period = span_v + q_rows
    k = np.arange(period)
    rel = np.clip(hist_v + q_rows - 1 - k, -REL_CLIP, REL_CLIP) + REL_CLIP
    onehot = jnp.asarray(np.eye(N_REL, dtype=np.float32)[:, rel])
    diag_row = jnp.dot(rel_bias.astype(F32), onehot, precision=lax.Precision.HIGHEST)
    flat = jnp.tile(diag_row, (1, q_rows))[:, :q_rows * (period - 1)]
    table = flat.reshape(n_heads, q_rows, period - 1)[:, :, q_rows - 1:q_rows - 1 + span_v]
    far = jnp.broadcast_to(rel_bias[:, N_REL - 1].astype(F32)[:, None, None], (n_heads, q_rows, const_cols))
    table = jnp.concatenate([far, table], axis=2)
    i_idx = np.arange(q_rows)[:, None]
    j_idx = np.arange(span_pad)[None, :]
    valid = j_idx < hist_rows + q_rows
    if band_chunk is not None:
        start = (i_idx // band_chunk) * band_chunk
        valid = valid & (j_idx >= start) & (j_idx < start + hist_rows + band_chunk)
    return jnp.where(jnp.asarray(np.broadcast_to(valid, (q_rows, span_pad)))[None], table / ATT_SCALE, MASK_VALUE)


def _layer(x, hist, s0, cache_k, cache_v, prm, rel_bias, *, n_streams, seq, prompt):
    t = n_streams * seq
    x2 = x.reshape(t, D_MODEL)
    proj = _in_projection(x2, prm["norm_w"], prm["w_qkvg"], prm["w_a"], prm["w_dt"], prm["qk_w"],
                          tm=PROMPT_PROJ_TM if prompt else SAMPLE_PROJ_TM, emit_kv=not prompt)
    if prompt:
        qkv, rest, dt = proj
        q_rows, sub_rows, n_chunks = SSD_CHUNK_ROWS, SSD_ROWS, seq // SSD_CHUNK_ROWS
        kv_rows = BAND_ROWS
        k_new, v_new = _kv_rows(x2, prm["norm_w"], prm["w_qkvg"], prm["qk_w"], first_row=t - kv_rows,
                                n_rows=kv_rows, tm=KV_ROWS_TM)
    else:
        qkv, rest, dt, k_new, v_new = proj
        q_rows, sub_rows, n_chunks = seq, seq, 1
        kv_rows = t
    y_ssm, s_new = _ssd_branch(rest, dt, hist, s0, prm["convp"], prm["hp"], prm["dexp"], prm["ssm_nw"],
                               prm["expand"], n_streams=n_streams, q_rows=q_rows, sub_rows=sub_rows,
                               n_chunks=n_chunks)
    if prompt:
        bias = _rel_bias_table(rel_bias, ATT_Q_ROWS, BAND_ROWS, BAND_ROWS + ATT_Q_ROWS, CHUNK)
        y_att = _prompt_attention(qkv, rest, bias)
    else:
        bias = _rel_bias_table(rel_bias, seq, BAND_ROWS, BAND_ROWS + LANES, None)
        y_att = _sample_attention(qkv, rest, cache_k, cache_v, bias, n_streams=n_streams, q_rows=seq)
    y = _out_projection(y_ssm, y_att, prm["w_out"], x2, tm=OUT_PROJ_TM, tn=OUT_PROJ_TN)
    new_conv = rest.reshape(n_streams, seq, R_COLS)[:, seq - (CONV_WIDTH - 1):, R_X:]
    kv_streams = kv_rows // n_streams
    kh = k_new.reshape(n_streams, kv_streams, ATT_HEADS, ATT_HEAD_DIM)
    vh = v_new.reshape(n_streams, kv_streams, ATT_HEADS, ATT_HEAD_DIM)
    new_ssm = s_new.reshape(n_streams, SSM_HEADS, SSM_HEADDIM, D_STATE)
    return y.reshape(n_streams, seq, D_MODEL), new_conv, new_ssm, kh, vh


def kernel(x_prompt, x_sample, state_conv, state_ssm, cache_k, cache_v, norm_w, w_in, conv_w, conv_b, dt_bias, a_log, d_skip, ssm_norm_w, q_norm_w, k_norm_w, rel_bias, w_out):
    bp, lp, _ = x_prompt.shape
    bs, ls, _ = x_sample.shape
    assert bp == 1 and lp >= BAND_ROWS and BAND_ROWS % KV_ROWS_TM == 0
    assert all(lp % rows == 0 for rows in (PROMPT_PROJ_TM, OUT_PROJ_TM, SSD_CHUNK_ROWS, 2 * ATT_Q_ROWS))
    assert ls % (2 * SUBLANES) == 0 and ls <= LANES and cache_k.shape[2] == BAND_ROWS
    assert all((bs * ls) % rows == 0 for rows in (SAMPLE_PROJ_TM, OUT_PROJ_TM))
    assert norm_w.shape[0] == 1
    prm = _prepare_params(norm_w[0], w_in[0], conv_w[0], conv_b[0], dt_bias[0], a_log[0], d_skip[0],
                          ssm_norm_w[0], q_norm_w[0], k_norm_w[0], w_out[0])
    rb = rel_bias[0]

    zero_hist = jnp.zeros((bp, SUBLANES, CONV_DIM), F32)
    zero_state = jnp.zeros((bp, D_SSM, D_STATE), F32)
    yp, c1, s1, k1, v1 = _layer(x_prompt, zero_hist, zero_state, None, None, prm, rb,
                                n_streams=bp, seq=lp, prompt=True)

    hist = jnp.pad(state_conv[0], ((0, 0), (SUBLANES - (CONV_WIDTH - 1), 0), (0, 0)))
    s0 = state_ssm[0].reshape(bs, D_SSM, D_STATE)
    half_heads = (bs, BAND_ROWS, ATT_HEADS // SUBLANES, SUBLANES, ATT_HEAD_DIM)
    ck = cache_k[0].reshape(half_heads)
    cv = cache_v[0].reshape(half_heads)
    ys, c2, s2, k2, v2 = _layer(x_sample, hist, s0, ck, cv, prm, rb, n_streams=bs, seq=ls, prompt=False)

    return (yp, ys, c1[None], s1[None], k1[None], v1[None], c2[None], s2[None], k2[None], v2[None])
```
